```python
import math
import jax, jax.numpy as jnp
from jax import lax
import numpy as np

D_MODEL = 1024
BATCH = 8
SEQ = 2048
DEPTH = 1
DEC_BATCH = 128
DEC_SEQ = 4
PAST_LEN = 16384
PAGE_SIZE = 128

MIX_WIDTH = D_MODEL
HG_WIDTH = MIX_WIDTH // 2
GD_WIDTH = MIX_WIDTH - HG_WIDTH
HG_HEAD_DIM = 128
HG_HEADS = HG_WIDTH // HG_HEAD_DIM
GD_HEAD_DIM = 128
GD_HEADS = GD_WIDTH // GD_HEAD_DIM
GD_CONV = 4
N_MEM = 256
MEM_HEADS = 4
MEM_HEAD_DIM = D_MODEL // MEM_HEADS
D_FF = 2816
FFN_CONV = 3
CHUNK = 64
LN_EPS = 1e-5
RMS_EPS = 1e-6
ALPHA = (2.0 * DEPTH) ** 0.25
BETA = (8.0 * DEPTH) ** -0.25
IN_COLS = 4 * HG_WIDTH + 4 * GD_WIDTH + 2 * GD_HEADS

kernel_name = "hymba_hgrn2_gdn_deepnorm_step"


def _layer_norm(x, g, b):
    xf = x.astype(jnp.float32)
    mu = jnp.mean(xf, -1, keepdims=True)
    var = jnp.mean(jnp.square(xf - mu), -1, keepdims=True)
    y = (xf - mu) * lax.rsqrt(var + LN_EPS) * g.astype(jnp.float32) + b.astype(jnp.float32)
    return y.astype(x.dtype)


def _rms_norm(x, g):
    xf = x.astype(jnp.float32)
    return xf * lax.rsqrt(jnp.mean(jnp.square(xf), -1, keepdims=True) + RMS_EPS) * g.astype(jnp.float32)


def _l2norm(x):
    xf = x.astype(jnp.float32)
    return xf * lax.rsqrt(jnp.sum(jnp.square(xf), -1, keepdims=True) + RMS_EPS)


def _causal_dwconv(x, buf, w):
    k_w = w.shape[0]
    t = x.shape[1]
    xp = jnp.concatenate([buf.astype(x.dtype), x], axis=1)
    out = sum(xp[:, j:j + t] * w[j].astype(x.dtype) for j in range(k_w))
    return out, xp[:, t:]


def _to_chunks(a, c):
    b, t, h, d = a.shape
    n = -(-t // c)
    a = jnp.pad(a.astype(jnp.float32), ((0, 0), (0, n * c - t), (0, 0), (0, 0)))
    return a.reshape(b, n, c, h, d).transpose(1, 0, 3, 2, 4)


def _from_chunks(o, t):
    n, b, h, c, d = o.shape
    return o.transpose(1, 0, 3, 2, 4).reshape(b, n * c, h, d)[:, :t]


def _hgrn2_chunked(q, k, v, log_f, s0):
    t = q.shape[1]
    c = min(CHUNK, t)
    qc, kc, vc, gc = (_to_chunks(a, c) for a in (q, k, v, log_f))
    incl = jnp.tril(jnp.ones((c, c), dtype=bool))[:, :, None]

    def step(s, inp):
        qi, ki, vi, gi = inp
        g_cum = jnp.cumsum(gi, axis=2)
        rel = jnp.where(incl, g_cum[:, :, :, None, :] - g_cum[:, :, None, :, :], -jnp.inf)
        scores = jnp.einsum('bhtd,bhsd,bhtsd->bhts', qi, ki, jnp.exp(rel))
        o = (jnp.einsum('bhts,bhsv->bhtv', scores, vi)
             + jnp.einsum('bhtd,bhdv->bhtv', qi * jnp.exp(g_cum), s))
        g_last = g_cum[:, :, -1:, :]
        s = (jnp.exp(g_last[:, :, 0, :])[..., None] * s
             + jnp.einsum('bhsd,bhsv->bhdv', ki * jnp.exp(g_last - g_cum), vi))
        return s, o

    s_fin, o = lax.scan(step, s0.astype(jnp.float32), (qc, kc, vc, gc))
    return _from_chunks(o, t), s_fin.astype(s0.dtype)


def _gdn_chunked(q, k, v, log_a, beta, s0):
    t = q.shape[1]
    c = min(CHUNK, t)
    qc, kc, vc, gc, bc = (_to_chunks(a, c) for a in (q, k, v, log_a, beta))
    incl = jnp.tril(jnp.ones((c, c), dtype=bool))
    strict = jnp.tril(jnp.ones((c, c), dtype=bool), -1)
    eye = jnp.eye(c, dtype=jnp.float32)

    def step(s, inp):
        qi, ki, vi, gi, bi = inp
        g_cum = jnp.cumsum(gi[..., 0], axis=-1)
        gamma = jnp.exp(g_cum)[..., None]
        decay = jnp.exp(jnp.where(incl, g_cum[..., :, None] - g_cum[..., None, :], -jnp.inf))
        a_mat = jnp.where(strict, bi * jnp.einsum('bhtd,bhsd->bhts', ki, ki) * decay, 0.0)
        rhs = bi * (vi - gamma * jnp.einsum('bhtd,bhdv->bhtv', ki, s))
        u = lax.linalg.triangular_solve(eye + a_mat, rhs, left_side=True, lower=True,
                                        unit_diagonal=True)
        o = (gamma * jnp.einsum('bhtd,bhdv->bhtv', qi, s)
             + jnp.einsum('bhts,bhsv->bhtv', jnp.einsum('bhtd,bhsd->bhts', qi, ki) * decay, u))
        g_last = g_cum[..., -1:]
        s = (jnp.exp(g_last)[..., None] * s
             + jnp.einsum('bhsd,bhsv->bhdv', ki * jnp.exp(g_last - g_cum)[..., None], u))
        return s, o

    s_fin, o = lax.scan(step, s0.astype(jnp.float32), (qc, kc, vc, gc, bc))
    return _from_chunks(o, t), s_fin.astype(s0.dtype)


def _mem_attn(h, mem_k, mem_v, w_mq, w_mo):
    bsz, t, _ = h.shape
    q = jnp.einsum('btd,de->bte', h, w_mq).reshape(bsz, t, MEM_HEADS, MEM_HEAD_DIM)
    s = jnp.einsum('bthd,bmhd->bhtm', q, mem_k.astype(h.dtype)).astype(jnp.float32) * MEM_HEAD_DIM ** -0.5
    p = jax.nn.softmax(s, axis=-1).astype(h.dtype)
    o = jnp.einsum('bhtm,bmhd->bthd', p, mem_v.astype(h.dtype)).reshape(bsz, t, D_MODEL)
    return jnp.einsum('btd,de->bte', o, w_mo)


def _conv_ffn(h, buf, w_up, w_conv, b_conv, w_down):
    up = jnp.einsum('btd,df->btf', h, w_up)
    gate, val = jnp.split(up, 2, axis=-1)
    gate_c, new_buf = _causal_dwconv(gate, buf, w_conv)
    act = jax.nn.gelu(gate_c + b_conv.astype(h.dtype), approximate=False) * val
    return jnp.einsum('btf,fd->btd', act, w_down), new_buf


def _layer(x, s_hg, s_gd, buf_gd, buf_ffn, mem_k, mem_v, lb,
           w_in, w_gd_conv, gd_a_log, gd_dt_bias, hg_norm_g, gd_norm_g, w_out,
           ln1_g, ln1_b, w_mq, w_mo, ln2_g, ln2_b,
           w_up, w_ffn_conv, b_ffn_conv, w_down, ln3_g, ln3_b):
    bsz, t, _ = x.shape
    f32 = jnp.float32
    proj = jnp.einsum('btd,de->bte', x, w_in)
    splits = np.cumsum([HG_WIDTH] * 4 + [3 * GD_WIDTH, GD_WIDTH, GD_HEADS]).tolist()
    hq, hf, hi, hgate, gqkv, gz, gb, ga = jnp.split(proj, splits, axis=-1)

    def heads(a, n_h):
        return a.reshape(bsz, t, n_h, -1)

    f = lb + (1.0 - lb) * jax.nn.sigmoid(hf.astype(f32))
    o_hg, new_hg = _hgrn2_chunked(heads(jax.nn.silu(hq), HG_HEADS), heads(1.0 - f, HG_HEADS),
                                  heads(hi, HG_HEADS), heads(jnp.log(f), HG_HEADS), s_hg)
    o_hg = (_rms_norm(o_hg, hg_norm_g) * jax.nn.silu(heads(hgate, HG_HEADS).astype(f32))
            ).reshape(bsz, t, HG_WIDTH)

    qkv, new_buf_gd = _causal_dwconv(gqkv, buf_gd, w_gd_conv)
    gq, gk, gv = jnp.split(jax.nn.silu(qkv), 3, axis=-1)
    q = _l2norm(heads(gq, GD_HEADS)) * GD_HEAD_DIM ** -0.5
    k = _l2norm(heads(gk, GD_HEADS))
    beta = jax.nn.sigmoid(gb.astype(f32))[..., None]
    log_a = (-jnp.exp(gd_a_log.astype(f32))
             * jax.nn.softplus(ga.astype(f32) + gd_dt_bias.astype(f32)))[..., None]
    o_gd, new_gd = _gdn_chunked(q, k, heads(gv, GD_HEADS), log_a, beta, s_gd)
    o_gd = (_rms_norm(o_gd, gd_norm_g) * jax.nn.silu(heads(gz, GD_HEADS).astype(f32))
            ).reshape(bsz, t, GD_WIDTH)

    mix = jnp.concatenate([o_hg, o_gd], axis=-1).astype(x.dtype)
    h = _layer_norm(ALPHA * x + jnp.einsum('btd,de->bte', mix, w_out), ln1_g, ln1_b)
    h = _layer_norm(ALPHA * h + _mem_attn(h, mem_k, mem_v, w_mq, w_mo), ln2_g, ln2_b)
    ff, new_buf_ffn = _conv_ffn(h, buf_ffn, w_up, w_ffn_conv, b_ffn_conv, w_down)
    y = _layer_norm(ALPHA * h + ff, ln3_g, ln3_b)
    return y, new_hg, new_gd, new_buf_gd, new_buf_ffn


def setup_inputs(seed: int = 0) -> dict:
    key = jax.random.key(seed)
    ks = iter(jax.random.split(key, 40))
    L, D = DEPTH, D_MODEL

    def nrm(shape, scale):
        return jax.random.normal(next(ks), shape, jnp.float32) * scale

    dt = jnp.exp(jax.random.uniform(next(ks), (L, GD_HEADS), jnp.float32,
                                    minval=math.log(1e-3), maxval=math.log(1e-1)))
    a_init = jax.random.uniform(next(ks), (L, GD_HEADS), jnp.float32, minval=1.0, maxval=16.0)
    return {
        "x_prompt": nrm((BATCH, SEQ, D), 1.0),
        "x_sample": nrm((DEC_BATCH, DEC_SEQ, D), 1.0),
        "state_hgrn": nrm((L, DEC_BATCH, HG_HEADS, HG_HEAD_DIM, HG_HEAD_DIM), 0.5),
        "state_gdn": nrm((L, DEC_BATCH, GD_HEADS, GD_HEAD_DIM, GD_HEAD_DIM), 0.3),
        "state_gdn_conv": nrm((L, DEC_BATCH, GD_CONV - 1, 3 * GD_WIDTH), 1.0),
        "state_ffn_conv": nrm((L, DEC_BATCH, FFN_CONV - 1, D_FF), 1.0),
        "cache_mem_k": nrm((L, DEC_BATCH, N_MEM, MEM_HEADS, MEM_HEAD_DIM), 1.0),
        "cache_mem_v": nrm((L, DEC_BATCH, N_MEM, MEM_HEADS, MEM_HEAD_DIM), 1.0),
        "mem_prompt": nrm((BATCH, N_MEM, D), 1.0),
        "hgrn_lb_logits": nrm((L + 1, HG_WIDTH), 0.5),
        "w_in": nrm((L, D, IN_COLS), D ** -0.5),
        "w_gd_conv": nrm((L, GD_CONV, 3 * GD_WIDTH), GD_CONV ** -0.5),
        "gd_a_log": jnp.log(a_init),
        "gd_dt_bias": dt + jnp.log(-jnp.expm1(-dt)),
        "hg_norm_g": 1.0 + nrm((L, HG_HEAD_DIM), 0.02),
        "gd_norm_g": 1.0 + nrm((L, GD_HEAD_DIM), 0.02),
        "w_out": nrm((L, MIX_WIDTH, D), MIX_WIDTH ** -0.5 * BETA),
        "ln1_g": 1.0 + nrm((L, D), 0.02),
        "ln1_b": nrm((L, D), 0.02),
        "w_mq": nrm((L, D, D), D ** -0.5),
        "w_mkv": nrm((L, D, 2 * D), D ** -0.5),
        "w_mo": nrm((L, D, D), D ** -0.5 * BETA),
        "ln2_g": 1.0 + nrm((L, D), 0.02),
        "ln2_b": nrm((L, D), 0.02),
        "w_up": nrm((L, D, 2 * D_FF), D ** -0.5),
        "w_ffn_conv": nrm((L, FFN_CONV, D_FF), FFN_CONV ** -0.5),
        "b_ffn_conv": nrm((L, D_FF), 0.02),
        "w_down": nrm((L, D_FF, D), D_FF ** -0.5 * BETA),
        "ln3_g": 1.0 + nrm((L, D), 0.02),
        "ln3_b": nrm((L, D), 0.02),
    }


def reference(x_prompt, x_sample, state_hgrn, state_gdn, state_gdn_conv, state_ffn_conv,
              cache_mem_k, cache_mem_v, mem_prompt, hgrn_lb_logits, w_in, w_gd_conv,
              gd_a_log, gd_dt_bias, hg_norm_g, gd_norm_g, w_out, ln1_g, ln1_b,
              w_mq, w_mkv, w_mo, ln2_g, ln2_b, w_up, w_ffn_conv, b_ffn_conv, w_down,
              ln3_g, ln3_b):
    bp = x_prompt.shape[0]
    dt_p = x_prompt.dtype
    lb_all = jnp.cumsum(jax.nn.softmax(hgrn_lb_logits.astype(jnp.float32), axis=0), axis=0)

    xp, xs = x_prompt, x_sample
    p_hg, p_gd, p_bgd, p_bff, p_mk, p_mv = [], [], [], [], [], []
    s_hg, s_gd, s_bgd, s_bff = [], [], [], []
    for l in range(DEPTH):
        lp = (lb_all[l], w_in[l], w_gd_conv[l], gd_a_log[l], gd_dt_bias[l], hg_norm_g[l],
              gd_norm_g[l], w_out[l], ln1_g[l], ln1_b[l], w_mq[l], w_mo[l], ln2_g[l], ln2_b[l],
              w_up[l], w_ffn_conv[l], b_ffn_conv[l], w_down[l], ln3_g[l], ln3_b[l])
        mkv = jnp.einsum('bmd,de->bme', mem_prompt, w_mkv[l])
        mk, mv = jnp.split(mkv, 2, axis=-1)
        mk = mk.reshape(bp, N_MEM, MEM_HEADS, MEM_HEAD_DIM)
        mv = mv.reshape(bp, N_MEM, MEM_HEADS, MEM_HEAD_DIM)
        xp, a, b, c, d = _layer(
            xp,
            jnp.zeros((bp, HG_HEADS, HG_HEAD_DIM, HG_HEAD_DIM), dt_p),
            jnp.zeros((bp, GD_HEADS, GD_HEAD_DIM, GD_HEAD_DIM), dt_p),
            jnp.zeros((bp, GD_CONV - 1, 3 * GD_WIDTH), dt_p),
            jnp.zeros((bp, FFN_CONV - 1, D_FF), dt_p),
            mk, mv, *lp)
        p_hg.append(a); p_gd.append(b); p_bgd.append(c); p_bff.append(d)
        p_mk.append(mk); p_mv.append(mv)
        xs, a, b, c, d = _layer(xs, state_hgrn[l], state_gdn[l], state_gdn_conv[l],
                                state_ffn_conv[l], cache_mem_k[l], cache_mem_v[l], *lp)
        s_hg.append(a); s_gd.append(b); s_bgd.append(c); s_bff.append(d)

    return (xp, xs,
            jnp.stack(p_hg), jnp.stack(p_gd), jnp.stack(p_bgd), jnp.stack(p_bff),
            jnp.stack(p_mk), jnp.stack(p_mv),
            jnp.stack(s_hg), jnp.stack(s_gd), jnp.stack(s_bgd), jnp.stack(s_bff))
```

```python
import functools
import math

import jax
import jax.numpy as jnp
from jax import lax
from jax.experimental import pallas as pl
from jax.experimental.pallas import tpu as pltpu

F32 = jnp.float32
BF16 = jnp.bfloat16

D_MODEL = 1024
HEAD_DIM = 128
N_HEADS = 4
MIX_HALF = N_HEADS * HEAD_DIM
GD_CONV = 4
FFN_CONV = 3
D_FF = 2816
N_MEM = 256
MEM_HEADS = 4
MEM_HEAD_DIM = D_MODEL // MEM_HEADS
LN_EPS = 1e-5
RMS_EPS = 1e-6
DEPTH = 1
ALPHA = (2.0 * DEPTH) ** 0.25

SUBLANES = 8
VMEM_LIMIT = 56 * 1024 * 1024


def _params(n_axes):
    return pltpu.CompilerParams(dimension_semantics=("arbitrary",) * n_axes,
                                vmem_limit_bytes=VMEM_LIMIT)


def _dot(a, b):
    return jnp.dot(a.astype(BF16), b.astype(BF16), preferred_element_type=F32)


def _dot_nt(a, b):
    return lax.dot_general(a.astype(BF16), b.astype(BF16), (((1,), (1,)), ((), ())),
                           preferred_element_type=F32)


def _dot_tn(a, b):
    return lax.dot_general(a.astype(BF16), b.astype(BF16), (((0,), (0,)), ((), ())),
                           preferred_element_type=F32)


def _split3(x):
    x1 = x.astype(BF16)
    r = x - x1.astype(F32)
    x2 = r.astype(BF16)
    x3 = (r - x2.astype(F32)).astype(BF16)
    return x1, x2, x3


def _dot_hi(a, b):
    a1 = a.astype(BF16)
    a2 = (a - a1.astype(F32)).astype(BF16)
    b1 = b.astype(BF16)
    b2 = (b - b1.astype(F32)).astype(BF16)
    return (jnp.dot(a1, b1, preferred_element_type=F32)
            + (jnp.dot(a1, b2, preferred_element_type=F32)
               + jnp.dot(a2, b1, preferred_element_type=F32)))


def _cumsum_rows(tri_bf16, x):
    x1, x2, x3 = _split3(x)
    return (jnp.dot(tri_bf16, x1, preferred_element_type=F32)
            + jnp.dot(tri_bf16, x2, preferred_element_type=F32)
            + jnp.dot(tri_bf16, x3, preferred_element_type=F32))


def _silu(x):
    return x * jax.nn.sigmoid(x)


def _mm_kernel(x_ref, w_ref, o_ref, xb_ref):
    @pl.when(pl.program_id(1) == 0)
    def _():
        xb_ref[...] = x_ref[...].astype(BF16)

    o_ref[...] = jnp.dot(xb_ref[...], w_ref[...], preferred_element_type=F32).astype(o_ref.dtype)


def _matmul(x, w, tm, tn):
    m, k = x.shape
    n = w.shape[1]
    tm = min(tm, m)
    tn = min(tn, n)
    return pl.pallas_call(
        _mm_kernel,
        grid=(m // tm, n // tn),
        in_specs=[pl.BlockSpec((tm, k), lambda i, j: (i, 0)),
                  pl.BlockSpec((k, tn), lambda i, j: (0, j))],
        out_specs=pl.BlockSpec((tm, tn), lambda i, j: (i, j)),
        out_shape=jax.ShapeDtypeStruct((m, n), F32),
        scratch_shapes=[pltpu.VMEM((tm, k), BF16)],
        compiler_params=_params(2),
        name="proj_matmul",
    )(x, w)


def _mm_ln_kernel(*refs, n_in):
    a_refs = refs[:n_in]
    w_refs = refs[n_in:2 * n_in]
    res_ref, g_ref, b_ref, o_ref = refs[2 * n_in:]
    acc = ALPHA * res_ref[...]
    for a_ref, w_ref in zip(a_refs, w_refs):
        acc = acc + jnp.dot(a_ref[...].astype(BF16), w_ref[...], preferred_element_type=F32)
    mu = jnp.mean(acc, axis=-1, keepdims=True)
    xc = acc - mu
    var = jnp.mean(xc * xc, axis=-1, keepdims=True)
    o_ref[...] = xc * lax.rsqrt(var + LN_EPS) * g_ref[...] + b_ref[...]


def _matmul_res_ln(a_list, w_list, res, g, b, tm):
    m, d = res.shape
    tm = min(tm, m)
    n_in = len(a_list)
    in_specs = ([pl.BlockSpec((tm, a.shape[1]), lambda i: (i, 0)) for a in a_list]
                + [pl.BlockSpec(w.shape, lambda i: (0, 0)) for w in w_list]
                + [pl.BlockSpec((tm, d), lambda i: (i, 0)),
                   pl.BlockSpec((1, d), lambda i: (0, 0)),
                   pl.BlockSpec((1, d), lambda i: (0, 0))])
    return pl.pallas_call(
        functools.partial(_mm_ln_kernel, n_in=n_in),
        grid=(m // tm,),
        in_specs=in_specs,
        out_specs=pl.BlockSpec((tm, d), lambda i: (i, 0)),
        out_shape=jax.ShapeDtypeStruct((m, d), F32),
        compiler_params=_params(1),
        name="proj_res_ln",
    )(*a_list, *w_list, res, g.reshape(1, d), b.reshape(1, d))


def _ref_rows(gc_ref, col, m, c):
    blk = 2 * m
    if blk >= SUBLANES:
        parts = [jnp.broadcast_to(gc_ref[pl.ds(j * blk + m - 1, 1), col], (blk, HEAD_DIM))
                 for j in range(c // blk)]
    else:
        sub = lax.broadcasted_iota(jnp.int32, (SUBLANES, HEAD_DIM), 0)
        parts = []
        for i in range(c // SUBLANES):
            tile = None
            for j in range(SUBLANES // blk):
                row = jnp.broadcast_to(gc_ref[pl.ds(i * SUBLANES + j * blk + m - 1, 1), col],
                                       (SUBLANES, HEAD_DIM))
                tile = row if tile is None else jnp.where(sub >= j * blk, row, tile)
            parts.append(tile)
    return parts[0] if len(parts) == 1 else jnp.concatenate(parts, axis=0)


def _hgrn_kernel(*refs, c, nb, has_s0, row_lo, n_chunks):
    if has_s0:
        hq_ref, hf_ref, hi_ref, hg_ref, lbl_ref, ng_ref, s0_ref, o_ref, so_ref, st_ref, gc_ref = refs
    else:
        hq_ref, hf_ref, hi_ref, hg_ref, lbl_ref, ng_ref, o_ref, so_ref, st_ref, gc_ref = refs
        s0_ref = None
    ci = pl.program_id(1)

    lbl = lbl_ref[...]
    e = jnp.exp(lbl - jnp.max(lbl, axis=0, keepdims=True))
    lb = e[0:1] / jnp.sum(e, axis=0, keepdims=True)

    rows = lax.broadcasted_iota(jnp.int32, (c, 1), 0)
    ti = lax.broadcasted_iota(jnp.int32, (c, c), 0)
    si = lax.broadcasted_iota(jnp.int32, (c, c), 1)
    tri = (si <= ti).astype(F32).astype(BF16)
    eye = ti == si
    xr = ti ^ si
    ng = ng_ref[...]

    for b in range(nb):
        @pl.when(ci == 0)
        def _():
            for h in range(N_HEADS):
                if s0_ref is None:
                    st_ref[b, h] = jnp.zeros((HEAD_DIM, HEAD_DIM), F32)
                else:
                    st_ref[b, h] = s0_ref[b, h].T

        f = lb + (1.0 - lb) * jax.nn.sigmoid(hf_ref[b])
        g = jnp.log(f)
        k = 1.0 - f
        if row_lo:
            g = jnp.where(rows >= row_lo, g, 0.0)
            k = jnp.where(rows >= row_lo, k, 0.0)
        q = _silu(hq_ref[b])
        gc = _cumsum_rows(tri, g)
        gc_ref[...] = gc

        for h in range(N_HEADS):
            col = slice(h * HEAD_DIM, (h + 1) * HEAD_DIM)
            qh, kh, gch = q[:, col], k[:, col], gc[:, col]
            vh = hi_ref[b, :, col]
            sc = jnp.where(eye, jnp.sum(qh * kh, axis=1, keepdims=True), 0.0)
            m = c // 2
            while m >= 1:
                w = jnp.exp(-jnp.abs(gch - _ref_rows(gc_ref, col, m, c)))
                upper = (rows & m) != 0
                qd = jnp.where(upper, qh * w, 0.0)
                kd = jnp.where(upper, 0.0, kh * w)
                p = _dot_nt(qd, kd)
                sc = sc + (p if 2 * m == c else jnp.where(xr < 2 * m, p, 0.0))
                m //= 2
            st = st_ref[b, h]
            o = _dot(sc, vh) + _dot_nt(qh * jnp.exp(gch), st)
            g_last = gc_ref[pl.ds(c - 1, 1), col]
            st_ref[b, h] = st * jnp.exp(g_last) + _dot_tn(vh, kh * jnp.exp(g_last - gch))
            on = o * lax.rsqrt(jnp.mean(o * o, axis=-1, keepdims=True) + RMS_EPS) * ng
            o_ref[b, :, col] = on * _silu(hg_ref[b, :, col])

        @pl.when(ci == n_chunks - 1)
        def _():
            for h in range(N_HEADS):
                so_ref[b, h] = st_ref[b, h].T


def _hgrn(p3, lb_logits, norm_g, s0, *, c, nb, row_lo):
    bsz, t, _ = p3.shape
    n_chunks = t // c
    has_s0 = s0 is not None
    blk = lambda j: pl.BlockSpec((nb, c, MIX_HALF), lambda i, ci, j=j: (i, ci, j))
    st_spec = pl.BlockSpec((nb, N_HEADS, HEAD_DIM, HEAD_DIM), lambda i, ci: (i, 0, 0, 0))
    in_specs = [blk(0), blk(1), blk(2), blk(3),
                pl.BlockSpec(lb_logits.shape, lambda i, ci: (0, 0)),
                pl.BlockSpec((1, HEAD_DIM), lambda i, ci: (0, 0))]
    args = [p3, p3, p3, p3, lb_logits, norm_g.reshape(1, HEAD_DIM)]
    if has_s0:
        in_specs.append(st_spec)
        args.append(s0)
    return pl.pallas_call(
        functools.partial(_hgrn_kernel, c=c, nb=nb, has_s0=has_s0, row_lo=row_lo, n_chunks=n_chunks),
        grid=(bsz // nb, n_chunks),
        in_specs=in_specs,
        out_specs=[pl.BlockSpec((nb, c, MIX_HALF), lambda i, ci: (i, ci, 0)), st_spec],
        out_shape=[jax.ShapeDtypeStruct((bsz, t, MIX_HALF), F32),
                   jax.ShapeDtypeStruct((bsz, N_HEADS, HEAD_DIM, HEAD_DIM), F32)],
        scratch_shapes=[pltpu.VMEM((nb, N_HEADS, HEAD_DIM, HEAD_DIM), F32),
                        pltpu.VMEM((c, MIX_HALF), F32)],
        compiler_params=_params(2),
        name="hgrn2_chunk",
    )(*args)


def _gdn_kernel(*refs, c, nb, has_state, row_lo, n_chunks):
    if has_state:
        (gq_ref, gk_ref, gv_ref, gz_ref, tail_ref, cw_ref, par_ref, ng_ref,
         hq_ref, hk_ref, hv_ref, s0_ref, o_ref, so_ref, cb_ref, s_ref, xp_ref) = refs
        hist_refs = (hq_ref, hk_ref, hv_ref)
    else:
        (gq_ref, gk_ref, gv_ref, gz_ref, tail_ref, cw_ref, par_ref, ng_ref,
         o_ref, so_ref, cb_ref, s_ref, xp_ref) = refs
        hist_refs = None
        s0_ref = None
    ci = pl.program_id(1)
    n_hist = GD_CONV - 1

    rows = lax.broadcasted_iota(jnp.int32, (c, 1), 0)
    ti = lax.broadcasted_iota(jnp.int32, (c, c), 0)
    si = lax.broadcasted_iota(jnp.int32, (c, c), 1)
    tri = (si <= ti).astype(F32).astype(BF16)
    eye = ti == si
    eye_f = eye.astype(F32)
    incl = si <= ti
    strict = si < ti
    ng = ng_ref[...]
    neg_a = -jnp.exp(par_ref[0:1, :])
    dt_bias = par_ref[1:2, :]

    for b in range(nb):
        @pl.when(ci == 0)
        def _():
            xp_ref[b, 0:SUBLANES, :] = jnp.zeros((SUBLANES, 3 * MIX_HALF), F32)
            for h in range(N_HEADS):
                if s0_ref is None:
                    s_ref[b, h] = jnp.zeros((HEAD_DIM, HEAD_DIM), F32)
                else:
                    s_ref[b, h] = s0_ref[b, h]

        qkv = []
        for j, x_ref in enumerate((gq_ref, gk_ref, gv_ref)):
            col = slice(j * MIX_HALF, (j + 1) * MIX_HALF)
            x = x_ref[b]
            if hist_refs is not None:
                x = jnp.where((rows >= row_lo - n_hist) & (rows < row_lo), hist_refs[j][b], x)
            xp_ref[b, SUBLANES:SUBLANES + c, col] = x
            y = x * cw_ref[n_hist:n_hist + 1, col]
            for tap in range(n_hist):
                off = SUBLANES - n_hist + tap
                y = y + xp_ref[b, off:off + c, col] * cw_ref[tap:tap + 1, col]
            qkv.append(_silu(y))
        cb_ref[b] = xp_ref[b, SUBLANES + c - n_hist:SUBLANES + c, :]
        xp_ref[b, 0:SUBLANES, :] = xp_ref[b, c:c + SUBLANES, :]
        q_all, k_all, v_all = qkv

        tail = tail_ref[b]
        beta_all = jax.nn.sigmoid(tail)
        la_all = neg_a * jax.nn.softplus(tail + dt_bias)
        if row_lo:
            beta_all = jnp.where(rows >= row_lo, beta_all, 0.0)
            la_all = jnp.where(rows >= row_lo, la_all, 0.0)
        gc_all = _cumsum_rows(tri, la_all)

        for h in range(N_HEADS):
            col = slice(h * HEAD_DIM, (h + 1) * HEAD_DIM)
            qh, kh, vh = q_all[:, col], k_all[:, col], v_all[:, col]
            qh = qh * lax.rsqrt(jnp.sum(qh * qh, axis=-1, keepdims=True) + RMS_EPS) * (HEAD_DIM ** -0.5)
            kh = kh * lax.rsqrt(jnp.sum(kh * kh, axis=-1, keepdims=True) + RMS_EPS)
            beta = beta_all[:, h:h + 1]
            gcol = gc_all[:, N_HEADS + h:N_HEADS + h + 1]
            grow = jnp.sum(eye_f * gcol, axis=0, keepdims=True)
            decay = jnp.where(incl, jnp.exp(jnp.where(incl, gcol - grow, 0.0)), 0.0)
            gamma = jnp.exp(gcol)
            s = s_ref[b, h]
            a_mat = jnp.where(strict, beta * _dot_nt(kh, kh) * decay, 0.0)
            rhs = beta * (vh - gamma * _dot(kh, s))
            p = -a_mat
            t_inv = eye_f + p
            span = 2
            while span < c:
                p = _dot_hi(p, p)
                t_inv = t_inv + _dot_hi(t_inv, p)
                span *= 2
            u = _dot_hi(t_inv, rhs)
            o = gamma * _dot(qh, s) + _dot(_dot_nt(qh, kh) * decay, u)
            g_last = gc_all[c - 1:c, N_HEADS + h:N_HEADS + h + 1]
            s_ref[b, h] = jnp.exp(g_last) * s + _dot_tn(kh * jnp.exp(g_last - gcol), u)
            on = o * lax.rsqrt(jnp.mean(o * o, axis=-1, keepdims=True) + RMS_EPS) * ng
            o_ref[b, :, col] = on * _silu(gz_ref[b, :, col])

        @pl.when(ci == n_chunks - 1)
        def _():
            for h in range(N_HEADS):
                so_ref[b, h] = s_ref[b, h]


def _gdn(p3, tail3, conv_w, a_log, dt_bias, norm_g, hist, s0, *, c, nb, row_lo):
    bsz, t, _ = p3.shape
    n_chunks = t // c
    has_state = s0 is not None
    n_hist = GD_CONV - 1
    blk = lambda j: pl.BlockSpec((nb, c, MIX_HALF), lambda i, ci, j=j: (i, ci, j))
    st_spec = pl.BlockSpec((nb, N_HEADS, HEAD_DIM, HEAD_DIM), lambda i, ci: (i, 0, 0, 0))
    par = jnp.zeros((SUBLANES, HEAD_DIM), F32)
    par = par.at[0, N_HEADS:2 * N_HEADS].set(a_log).at[1, N_HEADS:2 * N_HEADS].set(dt_bias)
    in_specs = [blk(4), blk(5), blk(6), blk(7),
                pl.BlockSpec((nb, c, HEAD_DIM), lambda i, ci: (i, ci, 0)),
                pl.BlockSpec(conv_w.shape, lambda i, ci: (0, 0)),
                pl.BlockSpec(par.shape, lambda i, ci: (0, 0)),
                pl.BlockSpec((1, HEAD_DIM), lambda i, ci: (0, 0))]
    args = [p3, p3, p3, p3, tail3, conv_w, par, norm_g.reshape(1, HEAD_DIM)]
    if has_state:
        in_specs += [blk(0), blk(1), blk(2), st_spec]
        args += [hist, hist, hist, s0]
    return pl.pallas_call(
        functools.partial(_gdn_kernel, c=c, nb=nb, has_state=has_state, row_lo=row_lo, n_chunks=n_chunks),
        grid=(bsz // nb, n_chunks),
        in_specs=in_specs,
        out_specs=[pl.BlockSpec((nb, c, MIX_HALF), lambda i, ci: (i, ci, 0)), st_spec,
                   pl.BlockSpec((nb, n_hist, 3 * MIX_HALF), lambda i, ci: (i, 0, 0))],
        out_shape=[jax.ShapeDtypeStruct((bsz, t, MIX_HALF), F32),
                   jax.ShapeDtypeStruct((bsz, N_HEADS, HEAD_DIM, HEAD_DIM), F32),
                   jax.ShapeDtypeStruct((bsz, n_hist, 3 * MIX_HALF), F32)],
        scratch_shapes=[pltpu.VMEM((nb, N_HEADS, HEAD_DIM, HEAD_DIM), F32),
                        pltpu.VMEM((nb, SUBLANES + c, 3 * MIX_HALF), F32)],
        compiler_params=_params(2),
        name="gdn_chunk",
    )(*args)


def _attn_kernel(q_ref, k_ref, v_ref, o_ref, *, nb):
    scale = MEM_HEAD_DIM ** -0.5
    for b in range(nb):
        for h in range(MEM_HEADS):
            col = slice(h * MEM_HEAD_DIM, (h + 1) * MEM_HEAD_DIM)
            s = _dot_nt(q_ref[b, :, col], k_ref[b, :, col]) * scale
            e = jnp.exp(s - jnp.max(s, axis=-1, keepdims=True))
            p = e / jnp.sum(e, axis=-1, keepdims=True)
            o_ref[b, :, col] = _dot(p, v_ref[b, :, col])


def _mem_attn_core(q3, mk, mv, *, tq, nb):
    bsz, t, d = q3.shape
    tq = min(tq, t)
    kv_spec = pl.BlockSpec((nb, N_MEM, d), lambda i, r: (i, 0, 0))
    return pl.pallas_call(
        functools.partial(_attn_kernel, nb=nb),
        grid=(bsz // nb, t // tq),
        in_specs=[pl.BlockSpec((nb, tq, d), lambda i, r: (i, r, 0)), kv_spec, kv_spec],
        out_specs=pl.BlockSpec((nb, tq, d), lambda i, r: (i, r, 0)),
        out_shape=jax.ShapeDtypeStruct((bsz, t, d), F32),
        compiler_params=_params(2),
        name="mem_attn_core",
    )(q3, mk, mv)


def _ffn_up_kernel(*refs, tm, grp, row_lo, has_hist):
    if has_hist:
        h_ref, wg_ref, wv_ref, cw_ref, cb_ref, hist_ref, act_ref, buf_ref, xp_ref = refs
    else:
        h_ref, wg_ref, wv_ref, cw_ref, cb_ref, act_ref, buf_ref, xp_ref = refs
        hist_ref = None
    n_hist = FFN_CONV - 1
    r = pl.program_id(2)

    @pl.when(r == 0)
    def _():
        xp_ref[0:SUBLANES, :] = jnp.zeros((SUBLANES, xp_ref.shape[1]), F32)

    hb = h_ref[0].astype(BF16)
    gate = jnp.dot(hb, wg_ref[...], preferred_element_type=F32)
    val = jnp.dot(hb, wv_ref[...], preferred_element_type=F32)
    if hist_ref is not None:
        pos = lax.broadcasted_iota(jnp.int32, (tm, 1), 0) % grp
        gate = jnp.where((pos >= row_lo - n_hist) & (pos < row_lo), hist_ref[0], gate)
    xp_ref[SUBLANES:SUBLANES + tm, :] = gate
    y = gate * cw_ref[n_hist:n_hist + 1, :] + cb_ref[...]
    for tap in range(n_hist):
        off = SUBLANES - n_hist + tap
        y = y + xp_ref[off:off + tm, :] * cw_ref[tap:tap + 1, :]
    gelu = 0.5 * y * (1.0 + lax.erf(y * (2.0 ** -0.5)))
    act_ref[0] = (gelu * val).astype(act_ref.dtype)
    if buf_ref.shape[1] == n_hist:
        buf_ref[0] = xp_ref[SUBLANES + tm - n_hist:SUBLANES + tm, :]
    else:
        buf_ref[0] = gate
    xp_ref[0:SUBLANES, :] = xp_ref[tm:tm + SUBLANES, :]


def _ffn_up(h3, w_gate, w_val, conv_w, conv_b, hist, *, tm, tn, grp, row_lo):
    g, r, d = h3.shape
    f = w_gate.shape[1]
    tm = min(tm, r)
    n_hist = FFN_CONV - 1
    has_hist = hist is not None
    assert grp == r or grp <= tm
    if grp == r:
        buf_spec = pl.BlockSpec((1, n_hist, tn), lambda gi, fi, ri: (gi, 0, fi))
        buf_rows = n_hist
    else:
        buf_spec = pl.BlockSpec((1, tm, tn), lambda gi, fi, ri: (gi, ri, fi))
        buf_rows = r
    w_spec = pl.BlockSpec((d, tn), lambda gi, fi, ri: (0, fi))
    in_specs = [pl.BlockSpec((1, tm, d), lambda gi, fi, ri: (gi, ri, 0)), w_spec, w_spec,
                pl.BlockSpec((FFN_CONV, tn), lambda gi, fi, ri: (0, fi)),
                pl.BlockSpec((1, tn), lambda gi, fi, ri: (0, fi))]
    args = [h3, w_gate, w_val, conv_w, conv_b.reshape(1, f)]
    if has_hist:
        in_specs.append(pl.BlockSpec((1, tm, tn), lambda gi, fi, ri: (gi, ri, fi)))
        args.append(hist)
    return pl.pallas_call(
        functools.partial(_ffn_up_kernel, tm=tm, grp=grp, row_lo=row_lo, has_hist=has_hist),
        grid=(g, f // tn, r // tm),
        in_specs=in_specs,
        out_specs=[pl.BlockSpec((1, tm, tn), lambda gi, fi, ri: (gi, ri, fi)),
                   buf_spec],
        out_shape=[jax.ShapeDtypeStruct((g, r, f), BF16),
                   jax.ShapeDtypeStruct((g, buf_rows, f), F32)],
        scratch_shapes=[pltpu.VMEM((SUBLANES + tm, tn), F32)],
        compiler_params=_params(3),
        name="ffn_up",
    )(*args)


def _layer(x3, s_hg, s_gd, hist_gd, hist_ffn, mk, mv, weights, *, row_lo, grp, ffn_streams, c_hg, c_gd,
           nb_rec, nb_attn):
    (lb_logits, w_in_main, w_in_tail, w_gd_conv, gd_a_log, gd_dt_bias, hg_norm_g, gd_norm_g,
     w_out_hg, w_out_gd, ln1_g, ln1_b, w_mq, w_mo, ln2_g, ln2_b,
     w_gate, w_val, w_ffn_conv, b_ffn_conv, w_down, ln3_g, ln3_b) = weights
    bsz, t, d = x3.shape
    rows = bsz * t
    x2 = x3.reshape(rows, d)

    proj = _matmul(x2, w_in_main, 1024, 512).reshape(bsz, t, -1)
    tail = _matmul(x2, w_in_tail, 1024, HEAD_DIM).reshape(bsz, t, HEAD_DIM)

    o_hg, new_hg = _hgrn(proj, lb_logits, hg_norm_g, s_hg, c=c_hg, nb=nb_rec, row_lo=row_lo)
    o_gd, new_gd, new_buf_gd = _gdn(proj, tail, w_gd_conv, gd_a_log, gd_dt_bias, gd_norm_g, hist_gd, s_gd,
                                     c=c_gd, nb=nb_rec, row_lo=row_lo)

    h1 = _matmul_res_ln([o_hg.reshape(rows, MIX_HALF), o_gd.reshape(rows, MIX_HALF)],
                        [w_out_hg, w_out_gd], x2, ln1_g, ln1_b, 512)
    q = _matmul(h1, w_mq, 1024, 1024)
    att = _mem_attn_core(q.reshape(bsz, t, d), mk, mv, tq=512, nb=nb_attn)
    h2 = _matmul_res_ln([att.reshape(rows, d)], [w_mo], h1, ln2_g, ln2_b, 512)

    h2_3 = h2.reshape(ffn_streams, rows // ffn_streams, d)
    hist3 = None if hist_ffn is None else hist_ffn.reshape(ffn_streams, rows // ffn_streams, D_FF)
    act, new_buf_ffn = _ffn_up(h2_3, w_gate, w_val, w_ffn_conv, b_ffn_conv, hist3,
                               tm=512, tn=D_FF // 2, grp=grp, row_lo=row_lo)
    y = _matmul_res_ln([act.reshape(rows, D_FF)], [w_down], h2, ln3_g, ln3_b, 512)
    return y.reshape(bsz, t, d), new_hg, new_gd, new_buf_gd, new_buf_ffn


def kernel(x_prompt, x_sample, state_hgrn, state_gdn, state_gdn_conv, state_ffn_conv, cache_mem_k, cache_mem_v, mem_prompt, hgrn_lb_logits, w_in, w_gd_conv, gd_a_log, gd_dt_bias, hg_norm_g, gd_norm_g, w_out, ln1_g, ln1_b, w_mq, w_mkv, w_mo, ln2_g, ln2_b, w_up, w_ffn_conv, b_ffn_conv, w_down, ln3_g, ln3_b):
    bp, tp, d = x_prompt.shape
    bs, ts, _ = x_sample.shape
    tpad = SUBLANES
    row_lo = tpad - ts
    l = 0

    main_cols = 8 * MIX_HALF
    w_in_tail = jnp.pad(w_in[l][:, main_cols:], ((0, 0), (0, HEAD_DIM - 2 * N_HEADS))).astype(BF16)
    weights = (hgrn_lb_logits, w_in[l][:, :main_cols].astype(BF16), w_in_tail, w_gd_conv[l],
               gd_a_log[l], gd_dt_bias[l], hg_norm_g[l], gd_norm_g[l],
               w_out[l][:MIX_HALF].astype(BF16), w_out[l][MIX_HALF:].astype(BF16), ln1_g[l], ln1_b[l],
               w_mq[l].astype(BF16), w_mo[l].astype(BF16), ln2_g[l], ln2_b[l],
               w_up[l][:, :D_FF].astype(BF16), w_up[l][:, D_FF:].astype(BF16),
               w_ffn_conv[l], b_ffn_conv[l], w_down[l].astype(BF16), ln3_g[l], ln3_b[l])

    mem2 = mem_prompt.reshape(bp * N_MEM, d)
    mk = _matmul(mem2, w_mkv[l][:, :d].astype(BF16), 1024, 1024).reshape(bp, N_MEM, d)
    mv = _matmul(mem2, w_mkv[l][:, d:].astype(BF16), 1024, 1024).reshape(bp, N_MEM, d)
    yp, p_hg, p_gd, p_bgd, p_bff = _layer(x_prompt, None, None, None, None, mk, mv, weights,
                                          row_lo=0, grp=tp, ffn_streams=bp, c_hg=128, c_gd=64,
                                          nb_rec=1, nb_attn=1)

    n_hg = GD_CONV - 1
    n_hf = FFN_CONV - 1
    xs = jnp.pad(x_sample, ((0, 0), (row_lo, 0), (0, 0)))
    hist_gd = jnp.pad(state_gdn_conv[l], ((0, 0), (row_lo - n_hg, ts), (0, 0)))
    hist_ffn = jnp.pad(state_ffn_conv[l], ((0, 0), (row_lo - n_hf, ts), (0, 0)))
    ys, s_hg, s_gd, s_bgd, s_bff = _layer(
        xs, state_hgrn[l], state_gdn[l], hist_gd, hist_ffn,
        cache_mem_k[l].reshape(bs, N_MEM, d), cache_mem_v[l].reshape(bs, N_MEM, d), weights,
        row_lo=row_lo, grp=tpad, ffn_streams=1, c_hg=tpad, c_gd=tpad, nb_rec=8, nb_attn=4)
    ys = ys[:, row_lo:]
    s_bff = s_bff.reshape(bs, tpad, D_FF)[:, tpad - n_hf:]

    mem_shape = (1, bp, N_MEM, MEM_HEADS, MEM_HEAD_DIM)
    return (yp, ys, p_hg[None], p_gd[None], p_bgd[None], p_bff[None],
            mk.reshape(mem_shape), mv.reshape(mem_shape),
            s_hg[None], s_gd[None], s_bgd[None], s_bff[None])
```

```python
import functools
import math

import jax
import jax.numpy as jnp
from jax import lax
from jax.experimental import pallas as pl
from jax.experimental.pallas import tpu as pltpu

F32 = jnp.float32
BF16 = jnp.bfloat16

D_MODEL = 1024
HEAD_DIM = 128
N_HEADS = 4
MIX_HALF = N_HEADS * HEAD_DIM
GD_CONV = 4
FFN_CONV = 3
D_FF = 2816
N_MEM = 256
MEM_HEADS = 4
MEM_HEAD_DIM = D_MODEL // MEM_HEADS
LN_EPS = 1e-5
RMS_EPS = 1e-6
DEPTH = 1
ALPHA = (2.0 * DEPTH) ** 0.25

SUBLANES = 8
VMEM_LIMIT = 56 * 1024 * 1024


def _params(n_axes):
    return pltpu.CompilerParams(dimension_semantics=("arbitrary",) * n_axes,
                                vmem_limit_bytes=VMEM_LIMIT)


def _dot(a, b):
    return jnp.dot(a.astype(BF16), b.astype(BF16), preferred_element_type=F32)


def _dot_nt(a, b):
    return lax.dot_general(a.astype(BF16), b.astype(BF16), (((1,), (1,)), ((), ())),
                           preferred_element_type=F32)


def _dot_tn(a, b):
    return lax.dot_general(a.astype(BF16), b.astype(BF16), (((0,), (0,)), ((), ())),
                           preferred_element_type=F32)


def _split3(x):
    x1 = x.astype(BF16)
    r = x - x1.astype(F32)
    x2 = r.astype(BF16)
    x3 = (r - x2.astype(F32)).astype(BF16)
    return x1, x2, x3


def _dot_hi(a, b):
    a1 = a.astype(BF16)
    a2 = (a - a1.astype(F32)).astype(BF16)
    b1 = b.astype(BF16)
    b2 = (b - b1.astype(F32)).astype(BF16)
    return (jnp.dot(a1, b1, preferred_element_type=F32)
            + (jnp.dot(a1, b2, preferred_element_type=F32)
               + jnp.dot(a2, b1, preferred_element_type=F32)))


_dot_inv = _dot


def _cumsum_rows(tri_bf16, x):
    x1, x2, x3 = _split3(x)
    return (jnp.dot(tri_bf16, x1, preferred_element_type=F32)
            + jnp.dot(tri_bf16, x2, preferred_element_type=F32)
            + jnp.dot(tri_bf16, x3, preferred_element_type=F32))


def _silu(x):
    return x * jax.nn.sigmoid(x)


def _mm_kernel(x_ref, w_ref, o_ref, xb_ref):
    @pl.when(pl.program_id(1) == 0)
    def _():
        xb_ref[...] = x_ref[...].astype(BF16)

    o_ref[...] = jnp.dot(xb_ref[...], w_ref[...], preferred_element_type=F32).astype(o_ref.dtype)


def _matmul(x, w, tm, tn):
    m, k = x.shape
    n = w.shape[1]
    tm = min(tm, m)
    tn = min(tn, n)
    return pl.pallas_call(
        _mm_kernel,
        grid=(m // tm, n // tn),
        in_specs=[pl.BlockSpec((tm, k), lambda i, j: (i, 0)),
                  pl.BlockSpec((k, tn), lambda i, j: (0, j))],
        out_specs=pl.BlockSpec((tm, tn), lambda i, j: (i, j)),
        out_shape=jax.ShapeDtypeStruct((m, n), F32),
        scratch_shapes=[pltpu.VMEM((tm, k), BF16)],
        compiler_params=_params(2),
        name="proj_matmul",
    )(x, w)


def _mm_ln_kernel(*refs, n_in):
    a_refs = refs[:n_in]
    w_refs = refs[n_in:2 * n_in]
    res_ref, g_ref, b_ref, o_ref = refs[2 * n_in:]
    acc = ALPHA * res_ref[...]
    for a_ref, w_ref in zip(a_refs, w_refs):
        acc = acc + jnp.dot(a_ref[...].astype(BF16), w_ref[...], preferred_element_type=F32)
    mu = jnp.mean(acc, axis=-1, keepdims=True)
    xc = acc - mu
    var = jnp.mean(xc * xc, axis=-1, keepdims=True)
    o_ref[...] = xc * lax.rsqrt(var + LN_EPS) * g_ref[...] + b_ref[...]


def _matmul_res_ln(a_list, w_list, res, g, b, tm):
    m, d = res.shape
    tm = min(tm, m)
    n_in = len(a_list)
    in_specs = ([pl.BlockSpec((tm, a.shape[1]), lambda i: (i, 0)) for a in a_list]
                + [pl.BlockSpec(w.shape, lambda i: (0, 0)) for w in w_list]
                + [pl.BlockSpec((tm, d), lambda i: (i, 0)),
                   pl.BlockSpec((1, d), lambda i: (0, 0)),
                   pl.BlockSpec((1, d), lambda i: (0, 0))])
    return pl.pallas_call(
        functools.partial(_mm_ln_kernel, n_in=n_in),
        grid=(m // tm,),
        in_specs=in_specs,
        out_specs=pl.BlockSpec((tm, d), lambda i: (i, 0)),
        out_shape=jax.ShapeDtypeStruct((m, d), F32),
        compiler_params=_params(1),
        name="proj_res_ln",
    )(*a_list, *w_list, res, g.reshape(1, d), b.reshape(1, d))


def _ref_rows(gc_ref, col, m, c):
    blk = 2 * m
    if blk >= SUBLANES:
        parts = [jnp.broadcast_to(gc_ref[pl.ds(j * blk + m - 1, 1), col], (blk, HEAD_DIM))
                 for j in range(c // blk)]
    else:
        sub = lax.broadcasted_iota(jnp.int32, (SUBLANES, HEAD_DIM), 0)
        parts = []
        for i in range(c // SUBLANES):
            tile = None
            for j in range(SUBLANES // blk):
                row = jnp.broadcast_to(gc_ref[pl.ds(i * SUBLANES + j * blk + m - 1, 1), col],
                                       (SUBLANES, HEAD_DIM))
                tile = row if tile is None else jnp.where(sub >= j * blk, row, tile)
            parts.append(tile)
    return parts[0] if len(parts) == 1 else jnp.concatenate(parts, axis=0)


def _hgrn_kernel(*refs, c, nb, has_s0, row_lo, n_chunks):
    if has_s0:
        hq_ref, hf_ref, hi_ref, hg_ref, lbl_ref, ng_ref, s0_ref, o_ref, so_ref, st_ref, gc_ref = refs
    else:
        hq_ref, hf_ref, hi_ref, hg_ref, lbl_ref, ng_ref, o_ref, so_ref, st_ref, gc_ref = refs
        s0_ref = None
    ci = pl.program_id(1)

    lbl = lbl_ref[...]
    e = jnp.exp(lbl - jnp.max(lbl, axis=0, keepdims=True))
    lb = e[0:1] / jnp.sum(e, axis=0, keepdims=True)

    rows = lax.broadcasted_iota(jnp.int32, (c, 1), 0)
    ti = lax.broadcasted_iota(jnp.int32, (c, c), 0)
    si = lax.broadcasted_iota(jnp.int32, (c, c), 1)
    tri = (si <= ti).astype(F32).astype(BF16)
    eye = ti == si
    xr = ti ^ si
    ng = ng_ref[...]

    for b in range(nb):
        @pl.when(ci == 0)
        def _():
            for h in range(N_HEADS):
                if s0_ref is None:
                    st_ref[b, h] = jnp.zeros((HEAD_DIM, HEAD_DIM), F32)
                else:
                    st_ref[b, h] = s0_ref[b, h].T

        f = lb + (1.0 - lb) * jax.nn.sigmoid(hf_ref[b])
        g = jnp.log(f)
        k = 1.0 - f
        if row_lo:
            g = jnp.where(rows >= row_lo, g, 0.0)
            k = jnp.where(rows >= row_lo, k, 0.0)
        q = _silu(hq_ref[b])
        gc = _cumsum_rows(tri, g)
        gc_ref[...] = gc

        for h in range(N_HEADS):
            col = slice(h * HEAD_DIM, (h + 1) * HEAD_DIM)
            qh, kh, gch = q[:, col], k[:, col], gc[:, col]
            vh = hi_ref[b, :, col]
            sc = jnp.where(eye, jnp.sum(qh * kh, axis=1, keepdims=True), 0.0)
            m = c // 2
            while m >= 1:
                w = jnp.exp(-jnp.abs(gch - _ref_rows(gc_ref, col, m, c)))
                upper = (rows & m) != 0
                qd = jnp.where(upper, qh * w, 0.0)
                kd = jnp.where(upper, 0.0, kh * w)
                p = _dot_nt(qd, kd)
                sc = sc + (p if 2 * m == c else jnp.where(xr < 2 * m, p, 0.0))
                m //= 2
            st = st_ref[b, h]
            o = _dot(sc, vh) + _dot_nt(qh * jnp.exp(gch), st)
            g_last = gc_ref[pl.ds(c - 1, 1), col]
            st_ref[b, h] = st * jnp.exp(g_last) + _dot_tn(vh, kh * jnp.exp(g_last - gch))
            on = o * lax.rsqrt(jnp.mean(o * o, axis=-1, keepdims=True) + RMS_EPS) * ng
            o_ref[b, :, col] = on * _silu(hg_ref[b, :, col])

        @pl.when(ci == n_chunks - 1)
        def _():
            for h in range(N_HEADS):
                so_ref[b, h] = st_ref[b, h].T


def _hgrn(p3, lb_logits, norm_g, s0, *, c, nb, row_lo):
    bsz, t, _ = p3.shape
    n_chunks = t // c
    has_s0 = s0 is not None
    blk = lambda j: pl.BlockSpec((nb, c, MIX_HALF), lambda i, ci, j=j: (i, ci, j))
    st_spec = pl.BlockSpec((nb, N_HEADS, HEAD_DIM, HEAD_DIM), lambda i, ci: (i, 0, 0, 0))
    in_specs = [blk(0), blk(1), blk(2), blk(3),
                pl.BlockSpec(lb_logits.shape, lambda i, ci: (0, 0)),
                pl.BlockSpec((1, HEAD_DIM), lambda i, ci: (0, 0))]
    args = [p3, p3, p3, p3, lb_logits, norm_g.reshape(1, HEAD_DIM)]
    if has_s0:
        in_specs.append(st_spec)
        args.append(s0)
    return pl.pallas_call(
        functools.partial(_hgrn_kernel, c=c, nb=nb, has_s0=has_s0, row_lo=row_lo, n_chunks=n_chunks),
        grid=(bsz // nb, n_chunks),
        in_specs=in_specs,
        out_specs=[pl.BlockSpec((nb, c, MIX_HALF), lambda i, ci: (i, ci, 0)), st_spec],
        out_shape=[jax.ShapeDtypeStruct((bsz, t, MIX_HALF), F32),
                   jax.ShapeDtypeStruct((bsz, N_HEADS, HEAD_DIM, HEAD_DIM), F32)],
        scratch_shapes=[pltpu.VMEM((nb, N_HEADS, HEAD_DIM, HEAD_DIM), F32),
                        pltpu.VMEM((c, MIX_HALF), F32)],
        compiler_params=_params(2),
        name="hgrn2_chunk",
    )(*args)


def _gdn_kernel(*refs, c, nc, nb, has_state, row_lo, n_steps):
    if has_state:
        (gq_ref, gk_ref, gv_ref, gz_ref, tail_ref, cw_ref, par_ref, ng_ref,
         hq_ref, hk_ref, hv_ref, s0_ref, o_ref, so_ref, cb_ref, s_ref, xp_ref) = refs
        hist_refs = (hq_ref, hk_ref, hv_ref)
    else:
        (gq_ref, gk_ref, gv_ref, gz_ref, tail_ref, cw_ref, par_ref, ng_ref,
         o_ref, so_ref, cb_ref, s_ref, xp_ref) = refs
        hist_refs = None
        s0_ref = None
    step = pl.program_id(1)
    n_hist = GD_CONV - 1
    rb = c * nc

    rows = lax.broadcasted_iota(jnp.int32, (rb, 1), 0) % c
    ti = lax.broadcasted_iota(jnp.int32, (c, c), 0)
    si = lax.broadcasted_iota(jnp.int32, (c, c), 1)
    tri = (si <= ti).astype(F32).astype(BF16)
    eye_f = (ti == si).astype(F32)
    incl = si <= ti
    ng = ng_ref[...]
    neg_a = -jnp.exp(par_ref[0:1, :])
    dt_bias = par_ref[1:2, :]

    chains = [(b, j, h) for b in range(nb) for j in range(nc) for h in range(N_HEADS)]
    q_all, k_all, v_all, beta_all, la_all = {}, {}, {}, {}, {}
    for b in range(nb):
        @pl.when(step == 0)
        def _():
            xp_ref[b, 0:SUBLANES, :] = jnp.zeros((SUBLANES, 3 * MIX_HALF), F32)
            for h in range(N_HEADS):
                if s0_ref is None:
                    s_ref[b, h] = jnp.zeros((HEAD_DIM, HEAD_DIM), F32)
                else:
                    s_ref[b, h] = s0_ref[b, h]

        qkv = []
        for j, x_ref in enumerate((gq_ref, gk_ref, gv_ref)):
            col = slice(j * MIX_HALF, (j + 1) * MIX_HALF)
            x = x_ref[b]
            if hist_refs is not None:
                x = jnp.where((rows >= row_lo - n_hist) & (rows < row_lo), hist_refs[j][b], x)
            xp_ref[b, SUBLANES:SUBLANES + rb, col] = x
            y = x * cw_ref[n_hist:n_hist + 1, col]
            for tap in range(n_hist):
                off = SUBLANES - n_hist + tap
                y = y + xp_ref[b, off:off + rb, col] * cw_ref[tap:tap + 1, col]
            qkv.append(_silu(y))
        cb_ref[b] = xp_ref[b, SUBLANES + rb - n_hist:SUBLANES + rb, :]
        xp_ref[b, 0:SUBLANES, :] = xp_ref[b, rb:rb + SUBLANES, :]
        q_all[b], k_all[b], v_all[b] = qkv

        tail = tail_ref[b]
        beta_all[b] = jax.nn.sigmoid(tail)
        la_all[b] = neg_a * jax.nn.softplus(tail + dt_bias)
        if row_lo:
            beta_all[b] = jnp.where(rows >= row_lo, beta_all[b], 0.0)
            la_all[b] = jnp.where(rows >= row_lo, la_all[b], 0.0)

    gc = {(b, j): _cumsum_rows(tri, la_all[b][j * c:(j + 1) * c]) for b in range(nb) for j in range(nc)}

    pre = {}
    for ch in chains:
        b, j, h = ch
        rs = slice(j * c, (j + 1) * c)
        col = slice(h * HEAD_DIM, (h + 1) * HEAD_DIM)
        qh, kh, vh = q_all[b][rs, col], k_all[b][rs, col], v_all[b][rs, col]
        qh = qh * lax.rsqrt(jnp.sum(qh * qh, axis=-1, keepdims=True) + RMS_EPS) * (HEAD_DIM ** -0.5)
        kh = kh * lax.rsqrt(jnp.sum(kh * kh, axis=-1, keepdims=True) + RMS_EPS)
        beta = beta_all[b][rs, h:h + 1]
        gcol = gc[b, j][:, N_HEADS + h:N_HEADS + h + 1]
        grow = jnp.sum(eye_f * gcol, axis=0, keepdims=True)
        decay = jnp.where(incl, jnp.exp(jnp.where(incl, gcol - grow, 0.0)), 0.0)
        gamma = jnp.exp(gcol)
        g_last = gcol[c - 1:c, :]
        kb = kh.astype(BF16)
        pre[ch] = dict(
            a=beta * _dot_nt(kb, kb) * decay,
            rhs=jnp.concatenate([(beta * gamma) * kh, beta * vh], axis=1).astype(BF16),
            gq=(gamma * qh).astype(BF16),
            aqk=(_dot_nt(qh, kb) * decay).astype(BF16),
            kd=(kh * jnp.exp(g_last - gcol)).astype(BF16),
            eg=jnp.exp(g_last))

    xr = ti ^ si
    first = (xr < 2) & (si < ti)
    t_inv = {ch: eye_f - jnp.where(first, pre[ch]["a"], 0.0) for ch in chains}
    n = 2
    while n < c:
        lower = (xr < 2 * n) & ((ti & n) != 0) & ((si & n) == 0)
        x = {ch: _dot_inv(jnp.where(lower, pre[ch]["a"], 0.0), t_inv[ch]) for ch in chains}
        t_inv = {ch: t_inv[ch] - _dot_inv(t_inv[ch], x[ch]) for ch in chains}
        n *= 2
    wu = {ch: _dot_inv(t_inv[ch], pre[ch]["rhs"]) for ch in chains}

    for j in range(nc):
        rs = slice(j * c, (j + 1) * c)
        group = [(b, j, h) for b in range(nb) for h in range(N_HEADS)]
        s_old = {ch: s_ref[ch[0], ch[2]] for ch in group}
        ws = {ch: jnp.dot(jnp.concatenate([wu[ch][:, :HEAD_DIM].astype(BF16), pre[ch]["gq"]], axis=0),
                          s_old[ch].astype(BF16), preferred_element_type=F32) for ch in group}
        u = {ch: (wu[ch][:, HEAD_DIM:] - ws[ch][:c]).astype(BF16) for ch in group}
        for ch in group:
            b, _, h = ch
            col = slice(h * HEAD_DIM, (h + 1) * HEAD_DIM)
            s_ref[b, h] = pre[ch]["eg"] * s_old[ch] + _dot_tn(pre[ch]["kd"], u[ch])
            o = ws[ch][c:] + jnp.dot(pre[ch]["aqk"], u[ch], preferred_element_type=F32)
            on = o * lax.rsqrt(jnp.mean(o * o, axis=-1, keepdims=True) + RMS_EPS) * ng
            o_ref[b, rs, col] = on * _silu(gz_ref[b, rs, col])

    @pl.when(step == n_steps - 1)
    def _():
        for b in range(nb):
            for h in range(N_HEADS):
                so_ref[b, h] = s_ref[b, h]


def _gdn(p3, tail3, conv_w, a_log, dt_bias, norm_g, hist, s0, *, c, nc, nb, row_lo):
    bsz, t, _ = p3.shape
    rb = c * nc
    n_steps = t // rb
    has_state = s0 is not None
    n_hist = GD_CONV - 1
    blk = lambda j: pl.BlockSpec((nb, rb, MIX_HALF), lambda i, si, j=j: (i, si, j))
    st_spec = pl.BlockSpec((nb, N_HEADS, HEAD_DIM, HEAD_DIM), lambda i, si: (i, 0, 0, 0))
    par = jnp.zeros((SUBLANES, HEAD_DIM), F32)
    par = par.at[0, N_HEADS:2 * N_HEADS].set(a_log).at[1, N_HEADS:2 * N_HEADS].set(dt_bias)
    in_specs = [blk(4), blk(5), blk(6), blk(7),
                pl.BlockSpec((nb, rb, HEAD_DIM), lambda i, si: (i, si, 0)),
                pl.BlockSpec(conv_w.shape, lambda i, si: (0, 0)),
                pl.BlockSpec(par.shape, lambda i, si: (0, 0)),
                pl.BlockSpec((1, HEAD_DIM), lambda i, si: (0, 0))]
    args = [p3, p3, p3, p3, tail3, conv_w, par, norm_g.reshape(1, HEAD_DIM)]
    if has_state:
        in_specs += [blk(0), blk(1), blk(2), st_spec]
        args += [hist, hist, hist, s0]
    return pl.pallas_call(
        functools.partial(_gdn_kernel, c=c, nc=nc, nb=nb, has_state=has_state, row_lo=row_lo, n_steps=n_steps),
        grid=(bsz // nb, n_steps),
        in_specs=in_specs,
        out_specs=[pl.BlockSpec((nb, rb, MIX_HALF), lambda i, si: (i, si, 0)), st_spec,
                   pl.BlockSpec((nb, n_hist, 3 * MIX_HALF), lambda i, si: (i, 0, 0))],
        out_shape=[jax.ShapeDtypeStruct((bsz, t, MIX_HALF), F32),
                   jax.ShapeDtypeStruct((bsz, N_HEADS, HEAD_DIM, HEAD_DIM), F32),
                   jax.ShapeDtypeStruct((bsz, n_hist, 3 * MIX_HALF), F32)],
        scratch_shapes=[pltpu.VMEM((nb, N_HEADS, HEAD_DIM, HEAD_DIM), F32),
                        pltpu.VMEM((nb, SUBLANES + rb, 3 * MIX_HALF), F32)],
        compiler_params=_params(2),
        name="gdn_chunk",
    )(*args)


def _attn_kernel(q_ref, k_ref, v_ref, o_ref, *, nb):
    scale = MEM_HEAD_DIM ** -0.5
    for b in range(nb):
        for h in range(MEM_HEADS):
            col = slice(h * MEM_HEAD_DIM, (h + 1) * MEM_HEAD_DIM)
            s = _dot_nt(q_ref[b, :, col], k_ref[b, :, col]) * scale
            e = jnp.exp(s - jnp.max(s, axis=-1, keepdims=True))
            p = e / jnp.sum(e, axis=-1, keepdims=True)
            o_ref[b, :, col] = _dot(p, v_ref[b, :, col])


def _mem_attn_core(q3, mk, mv, *, tq, nb):
    bsz, t, d = q3.shape
    tq = min(tq, t)
    kv_spec = pl.BlockSpec((nb, N_MEM, d), lambda i, r: (i, 0, 0))
    return pl.pallas_call(
        functools.partial(_attn_kernel, nb=nb),
        grid=(bsz // nb, t // tq),
        in_specs=[pl.BlockSpec((nb, tq, d), lambda i, r: (i, r, 0)), kv_spec, kv_spec],
        out_specs=pl.BlockSpec((nb, tq, d), lambda i, r: (i, r, 0)),
        out_shape=jax.ShapeDtypeStruct((bsz, t, d), F32),
        compiler_params=_params(2),
        name="mem_attn_core",
    )(q3, mk, mv)


def _ffn_up_kernel(*refs, tm, grp, row_lo, has_hist):
    if has_hist:
        h_ref, wg_ref, wv_ref, cw_ref, cb_ref, hist_ref, act_ref, buf_ref, xp_ref = refs
    else:
        h_ref, wg_ref, wv_ref, cw_ref, cb_ref, act_ref, buf_ref, xp_ref = refs
        hist_ref = None
    n_hist = FFN_CONV - 1
    r = pl.program_id(2)

    @pl.when(r == 0)
    def _():
        xp_ref[0:SUBLANES, :] = jnp.zeros((SUBLANES, xp_ref.shape[1]), F32)

    hb = h_ref[0].astype(BF16)
    gate = jnp.dot(hb, wg_ref[...], preferred_element_type=F32)
    val = jnp.dot(hb, wv_ref[...], preferred_element_type=F32)
    if hist_ref is not None:
        pos = lax.broadcasted_iota(jnp.int32, (tm, 1), 0) % grp
        gate = jnp.where((pos >= row_lo - n_hist) & (pos < row_lo), hist_ref[0], gate)
    xp_ref[SUBLANES:SUBLANES + tm, :] = gate
    y = gate * cw_ref[n_hist:n_hist + 1, :] + cb_ref[...]
    for tap in range(n_hist):
        off = SUBLANES - n_hist + tap
        y = y + xp_ref[off:off + tm, :] * cw_ref[tap:tap + 1, :]
    gelu = 0.5 * y * (1.0 + lax.erf(y * (2.0 ** -0.5)))
    act_ref[0] = (gelu * val).astype(act_ref.dtype)
    if buf_ref.shape[1] == n_hist:
        buf_ref[0] = xp_ref[SUBLANES + tm - n_hist:SUBLANES + tm, :]
    else:
        buf_ref[0] = gate
    xp_ref[0:SUBLANES, :] = xp_ref[tm:tm + SUBLANES, :]


def _ffn_up(h3, w_gate, w_val, conv_w, conv_b, hist, *, tm, tn, grp, row_lo):
    g, r, d = h3.shape
    f = w_gate.shape[1]
    tm = min(tm, r)
    n_hist = FFN_CONV - 1
    has_hist = hist is not None
    assert grp == r or grp <= tm
    if grp == r:
        buf_spec = pl.BlockSpec((1, n_hist, tn), lambda gi, fi, ri: (gi, 0, fi))
        buf_rows = n_hist
    else:
        buf_spec = pl.BlockSpec((1, tm, tn), lambda gi, fi, ri: (gi, ri, fi))
        buf_rows = r
    w_spec = pl.BlockSpec((d, tn), lambda gi, fi, ri: (0, fi))
    in_specs = [pl.BlockSpec((1, tm, d), lambda gi, fi, ri: (gi, ri, 0)), w_spec, w_spec,
                pl.BlockSpec((FFN_CONV, tn), lambda gi, fi, ri: (0, fi)),
                pl.BlockSpec((1, tn), lambda gi, fi, ri: (0, fi))]
    args = [h3, w_gate, w_val, conv_w, conv_b.reshape(1, f)]
    if has_hist:
        in_specs.append(pl.BlockSpec((1, tm, tn), lambda gi, fi, ri: (gi, ri, fi)))
        args.append(hist)
    return pl.pallas_call(
        functools.partial(_ffn_up_kernel, tm=tm, grp=grp, row_lo=row_lo, has_hist=has_hist),
        grid=(g, f // tn, r // tm),
        in_specs=in_specs,
        out_specs=[pl.BlockSpec((1, tm, tn), lambda gi, fi, ri: (gi, ri, fi)),
                   buf_spec],
        out_shape=[jax.ShapeDtypeStruct((g, r, f), BF16),
                   jax.ShapeDtypeStruct((g, buf_rows, f), F32)],
        scratch_shapes=[pltpu.VMEM((SUBLANES + tm, tn), F32)],
        compiler_params=_params(3),
        name="ffn_up",
    )(*args)


def _layer(x3, s_hg, s_gd, hist_gd, hist_ffn, mk, mv, weights, *, row_lo, grp, ffn_streams, c_hg, c_gd,
           nc_gd, nb_rec, nb_attn):
    (lb_logits, w_in_main, w_in_tail, w_gd_conv, gd_a_log, gd_dt_bias, hg_norm_g, gd_norm_g,
     w_out_hg, w_out_gd, ln1_g, ln1_b, w_mq, w_mo, ln2_g, ln2_b,
     w_gate, w_val, w_ffn_conv, b_ffn_conv, w_down, ln3_g, ln3_b) = weights
    bsz, t, d = x3.shape
    rows = bsz * t
    x2 = x3.reshape(rows, d)

    proj = _matmul(x2, w_in_main, 1024, 512).reshape(bsz, t, -1)
    tail = _matmul(x2, w_in_tail, 1024, HEAD_DIM).reshape(bsz, t, HEAD_DIM)

    o_hg, new_hg = _hgrn(proj, lb_logits, hg_norm_g, s_hg, c=c_hg, nb=nb_rec, row_lo=row_lo)
    o_gd, new_gd, new_buf_gd = _gdn(proj, tail, w_gd_conv, gd_a_log, gd_dt_bias, gd_norm_g, hist_gd, s_gd,
                                     c=c_gd, nc=nc_gd, nb=nb_rec, row_lo=row_lo)

    h1 = _matmul_res_ln([o_hg.reshape(rows, MIX_HALF), o_gd.reshape(rows, MIX_HALF)],
                        [w_out_hg, w_out_gd], x2, ln1_g, ln1_b, 512)
    q = _matmul(h1, w_mq, 1024, 1024)
    att = _mem_attn_core(q.reshape(bsz, t, d), mk, mv, tq=512, nb=nb_attn)
    h2 = _matmul_res_ln([att.reshape(rows, d)], [w_mo], h1, ln2_g, ln2_b, 512)

    h2_3 = h2.reshape(ffn_streams, rows // ffn_streams, d)
    hist3 = None if hist_ffn is None else hist_ffn.reshape(ffn_streams, rows // ffn_streams, D_FF)
    act, new_buf_ffn = _ffn_up(h2_3, w_gate, w_val, w_ffn_conv, b_ffn_conv, hist3,
                               tm=512, tn=D_FF // 2, grp=grp, row_lo=row_lo)
    y = _matmul_res_ln([act.reshape(rows, D_FF)], [w_down], h2, ln3_g, ln3_b, 512)
    return y.reshape(bsz, t, d), new_hg, new_gd, new_buf_gd, new_buf_ffn


def kernel(x_prompt, x_sample, state_hgrn, state_gdn, state_gdn_conv, state_ffn_conv, cache_mem_k, cache_mem_v, mem_prompt, hgrn_lb_logits, w_in, w_gd_conv, gd_a_log, gd_dt_bias, hg_norm_g, gd_norm_g, w_out, ln1_g, ln1_b, w_mq, w_mkv, w_mo, ln2_g, ln2_b, w_up, w_ffn_conv, b_ffn_conv, w_down, ln3_g, ln3_b):
    bp, tp, d = x_prompt.shape
    bs, ts, _ = x_sample.shape
    tpad = SUBLANES
    row_lo = tpad - ts
    l = 0

    main_cols = 8 * MIX_HALF
    w_in_tail = jnp.pad(w_in[l][:, main_cols:], ((0, 0), (0, HEAD_DIM - 2 * N_HEADS))).astype(BF16)
    weights = (hgrn_lb_logits, w_in[l][:, :main_cols].astype(BF16), w_in_tail, w_gd_conv[l],
               gd_a_log[l], gd_dt_bias[l], hg_norm_g[l], gd_norm_g[l],
               w_out[l][:MIX_HALF].astype(BF16), w_out[l][MIX_HALF:].astype(BF16), ln1_g[l], ln1_b[l],
               w_mq[l].astype(BF16), w_mo[l].astype(BF16), ln2_g[l], ln2_b[l],
               w_up[l][:, :D_FF].astype(BF16), w_up[l][:, D_FF:].astype(BF16),
               w_ffn_conv[l], b_ffn_conv[l], w_down[l].astype(BF16), ln3_g[l], ln3_b[l])

    mem2 = mem_prompt.reshape(bp * N_MEM, d)
    mk = _matmul(mem2, w_mkv[l][:, :d].astype(BF16), 1024, 1024).reshape(bp, N_MEM, d)
    mv = _matmul(mem2, w_mkv[l][:, d:].astype(BF16), 1024, 1024).reshape(bp, N_MEM, d)
    yp, p_hg, p_gd, p_bgd, p_bff = _layer(x_prompt, None, None, None, None, mk, mv, weights,
                                          row_lo=0, grp=tp, ffn_streams=bp, c_hg=128, c_gd=128, nc_gd=2,
                                          nb_rec=1, nb_attn=1)

    n_hg = GD_CONV - 1
    n_hf = FFN_CONV - 1
    xs = jnp.pad(x_sample, ((0, 0), (row_lo, 0), (0, 0)))
    hist_gd = jnp.pad(state_gdn_conv[l], ((0, 0), (row_lo - n_hg, ts), (0, 0)))
    hist_ffn = jnp.pad(state_ffn_conv[l], ((0, 0), (row_lo - n_hf, ts), (0, 0)))
    ys, s_hg, s_gd, s_bgd, s_bff = _layer(
        xs, state_hgrn[l], state_gdn[l], hist_gd, hist_ffn,
        cache_mem_k[l].reshape(bs, N_MEM, d), cache_mem_v[l].reshape(bs, N_MEM, d), weights,
        row_lo=row_lo, grp=tpad, ffn_streams=1, c_hg=tpad, c_gd=tpad, nc_gd=1,
        nb_rec=8, nb_attn=4)
    ys = ys[:, row_lo:]
    s_bff = s_bff.reshape(bs, tpad, D_FF)[:, tpad - n_hf:]

    mem_shape = (1, bp, N_MEM, MEM_HEADS, MEM_HEAD_DIM)
    return (yp, ys, p_hg[None], p_gd[None], p_bgd[None], p_bff[None],
            mk.reshape(mem_shape), mv.reshape(mem_shape),
            s_hg[None], s_gd[None], s_bgd[None], s_bff[None])
```

```python
import functools
import math

import jax
import jax.numpy as jnp
from jax import lax
from jax.experimental import pallas as pl
from jax.experimental.pallas import tpu as pltpu

F32 = jnp.float32
BF16 = jnp.bfloat16

D_MODEL = 1024
HEAD_DIM = 128
N_HEADS = 4
MIX_HALF = N_HEADS * HEAD_DIM
GD_CONV = 4
FFN_CONV = 3
D_FF = 2816
N_MEM = 256
MEM_HEADS = 4
MEM_HEAD_DIM = D_MODEL // MEM_HEADS
LN_EPS = 1e-5
RMS_EPS = 1e-6
DEPTH = 1
ALPHA = (2.0 * DEPTH) ** 0.25

SUBLANES = 8
LANES = 128
VMEM_LIMIT = 56 * 1024 * 1024


def _params(n_axes):
    return pltpu.CompilerParams(dimension_semantics=("arbitrary",) * n_axes,
                                vmem_limit_bytes=VMEM_LIMIT)


def _dot(a, b):
    return jnp.dot(a.astype(BF16), b.astype(BF16), preferred_element_type=F32)


def _dot_nt(a, b):
    return lax.dot_general(a.astype(BF16), b.astype(BF16), (((1,), (1,)), ((), ())),
                           preferred_element_type=F32)


def _dot_tn(a, b):
    return lax.dot_general(a.astype(BF16), b.astype(BF16), (((0,), (0,)), ((), ())),
                           preferred_element_type=F32)


def _split3(x):
    x1 = x.astype(BF16)
    r = x - x1.astype(F32)
    x2 = r.astype(BF16)
    x3 = (r - x2.astype(F32)).astype(BF16)
    return x1, x2, x3


def _dot_hi(a, b):
    a1 = a.astype(BF16)
    a2 = (a - a1.astype(F32)).astype(BF16)
    b1 = b.astype(BF16)
    b2 = (b - b1.astype(F32)).astype(BF16)
    return (jnp.dot(a1, b1, preferred_element_type=F32)
            + (jnp.dot(a1, b2, preferred_element_type=F32)
               + jnp.dot(a2, b1, preferred_element_type=F32)))


_dot_inv = _dot


def _cumsum_rows(tri_bf16, x):
    x1, x2, x3 = _split3(x)
    return (jnp.dot(tri_bf16, x1, preferred_element_type=F32)
            + jnp.dot(tri_bf16, x2, preferred_element_type=F32)
            + jnp.dot(tri_bf16, x3, preferred_element_type=F32))


def _silu(x):
    return x * jax.nn.sigmoid(x)


def _mm_kernel(x_ref, w_ref, o_ref, xb_ref):
    @pl.when(pl.program_id(1) == 0)
    def _():
        xb_ref[...] = x_ref[...].astype(BF16)

    o_ref[...] = jnp.dot(xb_ref[...], w_ref[...], preferred_element_type=F32).astype(o_ref.dtype)


def _matmul(x, w, tm, tn):
    m, k = x.shape
    n = w.shape[1]
    tm = min(tm, m)
    tn = min(tn, n)
    return pl.pallas_call(
        _mm_kernel,
        grid=(m // tm, n // tn),
        in_specs=[pl.BlockSpec((tm, k), lambda i, j: (i, 0)),
                  pl.BlockSpec((k, tn), lambda i, j: (0, j))],
        out_specs=pl.BlockSpec((tm, tn), lambda i, j: (i, j)),
        out_shape=jax.ShapeDtypeStruct((m, n), F32),
        scratch_shapes=[pltpu.VMEM((tm, k), BF16)],
        compiler_params=_params(2),
        name="proj_matmul",
    )(x, w)


def _mm_ln_kernel(*refs, n_in):
    a_refs = refs[:n_in]
    w_refs = refs[n_in:2 * n_in]
    res_ref, g_ref, b_ref, o_ref = refs[2 * n_in:]
    acc = ALPHA * res_ref[...]
    for a_ref, w_ref in zip(a_refs, w_refs):
        acc = acc + jnp.dot(a_ref[...].astype(BF16), w_ref[...], preferred_element_type=F32)
    mu = jnp.mean(acc, axis=-1, keepdims=True)
    xc = acc - mu
    var = jnp.mean(xc * xc, axis=-1, keepdims=True)
    o_ref[...] = xc * lax.rsqrt(var + LN_EPS) * g_ref[...] + b_ref[...]


def _matmul_res_ln(a_list, w_list, res, g, b, tm):
    m, d = res.shape
    tm = min(tm, m)
    n_in = len(a_list)
    in_specs = ([pl.BlockSpec((tm, a.shape[1]), lambda i: (i, 0)) for a in a_list]
                + [pl.BlockSpec(w.shape, lambda i: (0, 0)) for w in w_list]
                + [pl.BlockSpec((tm, d), lambda i: (i, 0)),
                   pl.BlockSpec((1, d), lambda i: (0, 0)),
                   pl.BlockSpec((1, d), lambda i: (0, 0))])
    return pl.pallas_call(
        functools.partial(_mm_ln_kernel, n_in=n_in),
        grid=(m // tm,),
        in_specs=in_specs,
        out_specs=pl.BlockSpec((tm, d), lambda i: (i, 0)),
        out_shape=jax.ShapeDtypeStruct((m, d), F32),
        compiler_params=_params(1),
        name="proj_res_ln",
    )(*a_list, *w_list, res, g.reshape(1, d), b.reshape(1, d))


def _ref_rows(gc_ref, b, col, m, c):
    blk = 2 * m
    if blk >= SUBLANES:
        parts = [jnp.broadcast_to(gc_ref[b, pl.ds(j * blk + m - 1, 1), col], (blk, HEAD_DIM))
                 for j in range(c // blk)]
    else:
        sub = lax.broadcasted_iota(jnp.int32, (SUBLANES, HEAD_DIM), 0)
        parts = []
        for i in range(c // SUBLANES):
            tile = None
            for j in range(SUBLANES // blk):
                row = jnp.broadcast_to(gc_ref[b, pl.ds(i * SUBLANES + j * blk + m - 1, 1), col],
                                       (SUBLANES, HEAD_DIM))
                tile = row if tile is None else jnp.where(sub >= j * blk, row, tile)
            parts.append(tile)
    return parts[0] if len(parts) == 1 else jnp.concatenate(parts, axis=0)


def _hgrn_kernel(*refs, c, nb, has_s0, row_lo, n_chunks):
    if has_s0:
        hq_ref, hf_ref, hi_ref, hg_ref, lbl_ref, ng_ref, s0_ref, o_ref, so_ref, st_ref, gc_ref = refs
    else:
        hq_ref, hf_ref, hi_ref, hg_ref, lbl_ref, ng_ref, o_ref, so_ref, st_ref, gc_ref = refs
        s0_ref = None
    ci = pl.program_id(1)

    lbl = lbl_ref[...]
    e = jnp.exp(lbl - jnp.max(lbl, axis=0, keepdims=True))
    lb = e[0:1] / jnp.sum(e, axis=0, keepdims=True)

    rows = lax.broadcasted_iota(jnp.int32, (c, 1), 0)
    ti = lax.broadcasted_iota(jnp.int32, (c, c), 0)
    si = lax.broadcasted_iota(jnp.int32, (c, c), 1)
    tri = (si <= ti).astype(F32).astype(BF16)
    eye = ti == si
    xr = ti ^ si
    ng = ng_ref[...]
    pairs = [(b, h) for b in range(nb) for h in range(N_HEADS)]
    cols = [slice(h * HEAD_DIM, (h + 1) * HEAD_DIM) for h in range(N_HEADS)]

    @pl.when(ci == 0)
    def _():
        for b, h in pairs:
            if s0_ref is None:
                st_ref[b, h] = jnp.zeros((HEAD_DIM, HEAD_DIM), F32)
            else:
                st_ref[b, h] = s0_ref[b, h].T

    q, k, gc = {}, {}, {}
    for b in range(nb):
        f = lb + (1.0 - lb) * jax.nn.sigmoid(hf_ref[b])
        g = jnp.log(f)
        kb = 1.0 - f
        if row_lo:
            g = jnp.where(rows >= row_lo, g, 0.0)
            kb = jnp.where(rows >= row_lo, kb, 0.0)
        qb = _silu(hq_ref[b])
        gcb = _cumsum_rows(tri, g)
        gc_ref[b] = gcb
        for h in range(N_HEADS):
            q[b, h], k[b, h], gc[b, h] = qb[:, cols[h]], kb[:, cols[h]], gcb[:, cols[h]]

    sc = {bh: jnp.where(eye, jnp.sum(q[bh] * k[bh], axis=1, keepdims=True), 0.0) for bh in pairs}
    m = c // 2
    while m >= 1:
        level = (xr >= m) & (xr < 2 * m) & (si < ti)
        prod = {}
        for bh in pairs:
            w = jnp.exp(-jnp.abs(gc[bh] - _ref_rows(gc_ref, bh[0], cols[bh[1]], m, c)))
            prod[bh] = _dot_nt(q[bh] * w, k[bh] * w)
        sc = {bh: jnp.where(level, prod[bh], sc[bh]) for bh in pairs}
        m //= 2

    v = {(b, h): hi_ref[b, :, cols[h]] for b, h in pairs}
    st = {bh: st_ref[bh] for bh in pairs}
    o = {bh: _dot(sc[bh], v[bh]) + _dot_nt(q[bh] * jnp.exp(gc[bh]), st[bh]) for bh in pairs}
    for b, h in pairs:
        g_last = gc_ref[b, pl.ds(c - 1, 1), cols[h]]
        st_ref[b, h] = st[b, h] * jnp.exp(g_last) + _dot_tn(v[b, h], k[b, h] * jnp.exp(g_last - gc[b, h]))
    for b, h in pairs:
        ob = o[b, h]
        on = ob * lax.rsqrt(jnp.mean(ob * ob, axis=-1, keepdims=True) + RMS_EPS) * ng
        o_ref[b, :, cols[h]] = on * _silu(hg_ref[b, :, cols[h]])

    @pl.when(ci == n_chunks - 1)
    def _():
        for b, h in pairs:
            so_ref[b, h] = st_ref[b, h].T


def _hgrn(p3, lb_logits, norm_g, s0, *, c, nb, row_lo):
    bsz, t, _ = p3.shape
    n_chunks = t // c
    has_s0 = s0 is not None
    blk = lambda j: pl.BlockSpec((nb, c, MIX_HALF), lambda i, ci, j=j: (i, ci, j))
    st_spec = pl.BlockSpec((nb, N_HEADS, HEAD_DIM, HEAD_DIM), lambda i, ci: (i, 0, 0, 0))
    in_specs = [blk(0), blk(1), blk(2), blk(3),
                pl.BlockSpec(lb_logits.shape, lambda i, ci: (0, 0)),
                pl.BlockSpec((1, HEAD_DIM), lambda i, ci: (0, 0))]
    args = [p3, p3, p3, p3, lb_logits, norm_g.reshape(1, HEAD_DIM)]
    if has_s0:
        in_specs.append(st_spec)
        args.append(s0)
    return pl.pallas_call(
        functools.partial(_hgrn_kernel, c=c, nb=nb, has_s0=has_s0, row_lo=row_lo, n_chunks=n_chunks),
        grid=(bsz // nb, n_chunks),
        in_specs=in_specs,
        out_specs=[pl.BlockSpec((nb, c, MIX_HALF), lambda i, ci: (i, ci, 0)), st_spec],
        out_shape=[jax.ShapeDtypeStruct((bsz, t, MIX_HALF), F32),
                   jax.ShapeDtypeStruct((bsz, N_HEADS, HEAD_DIM, HEAD_DIM), F32)],
        scratch_shapes=[pltpu.VMEM((nb, N_HEADS, HEAD_DIM, HEAD_DIM), F32),
                        pltpu.VMEM((nb, c, MIX_HALF), F32)],
        compiler_params=_params(2),
        name="hgrn2_chunk",
    )(*args)


def _gdn_kernel(*refs, c, nc, nb, has_state, row_lo, n_steps):
    if has_state:
        (gq_ref, gk_ref, gv_ref, gz_ref, tail_ref, cw_ref, par_ref, ng_ref,
         hq_ref, hk_ref, hv_ref, s0_ref, o_ref, so_ref, cb_ref, s_ref, xp_ref) = refs
        hist_refs = (hq_ref, hk_ref, hv_ref)
    else:
        (gq_ref, gk_ref, gv_ref, gz_ref, tail_ref, cw_ref, par_ref, ng_ref,
         o_ref, so_ref, cb_ref, s_ref, xp_ref) = refs
        hist_refs = None
        s0_ref = None
    step = pl.program_id(1)
    n_hist = GD_CONV - 1
    rb = c * nc

    rows = lax.broadcasted_iota(jnp.int32, (rb, 1), 0) % c
    ti = lax.broadcasted_iota(jnp.int32, (c, c), 0)
    si = lax.broadcasted_iota(jnp.int32, (c, c), 1)
    tri = (si <= ti).astype(F32).astype(BF16)
    eye_f = (ti == si).astype(F32)
    incl = si <= ti
    ng = ng_ref[...]
    neg_a = -jnp.exp(par_ref[0:1, :])
    dt_bias = par_ref[1:2, :]

    chains = [(b, j, h) for b in range(nb) for j in range(nc) for h in range(N_HEADS)]
    q_all, k_all, v_all, beta_all, la_all = {}, {}, {}, {}, {}
    for b in range(nb):
        @pl.when(step == 0)
        def _():
            xp_ref[b, 0:SUBLANES, :] = jnp.zeros((SUBLANES, 3 * MIX_HALF), F32)
            for h in range(N_HEADS):
                if s0_ref is None:
                    s_ref[b, h] = jnp.zeros((HEAD_DIM, HEAD_DIM), F32)
                else:
                    s_ref[b, h] = s0_ref[b, h]

        qkv = []
        for j, x_ref in enumerate((gq_ref, gk_ref, gv_ref)):
            col = slice(j * MIX_HALF, (j + 1) * MIX_HALF)
            x = x_ref[b]
            if hist_refs is not None:
                x = jnp.where((rows >= row_lo - n_hist) & (rows < row_lo), hist_refs[j][b], x)
            xp_ref[b, SUBLANES:SUBLANES + rb, col] = x
            y = x * cw_ref[n_hist:n_hist + 1, col]
            for tap in range(n_hist):
                off = SUBLANES - n_hist + tap
                y = y + xp_ref[b, off:off + rb, col] * cw_ref[tap:tap + 1, col]
            qkv.append(_silu(y))
        cb_ref[b] = xp_ref[b, SUBLANES + rb - n_hist:SUBLANES + rb, :]
        xp_ref[b, 0:SUBLANES, :] = xp_ref[b, rb:rb + SUBLANES, :]
        q_all[b], k_all[b], v_all[b] = qkv

        tail = tail_ref[b]
        beta_all[b] = jax.nn.sigmoid(tail)
        la_all[b] = neg_a * jax.nn.softplus(tail + dt_bias)
        if row_lo:
            beta_all[b] = jnp.where(rows >= row_lo, beta_all[b], 0.0)
            la_all[b] = jnp.where(rows >= row_lo, la_all[b], 0.0)

    gc = {(b, j): _cumsum_rows(tri, la_all[b][j * c:(j + 1) * c]) for b in range(nb) for j in range(nc)}

    pre = {}
    for ch in chains:
        b, j, h = ch
        rs = slice(j * c, (j + 1) * c)
        col = slice(h * HEAD_DIM, (h + 1) * HEAD_DIM)
        qh, kh, vh = q_all[b][rs, col], k_all[b][rs, col], v_all[b][rs, col]
        qh = qh * lax.rsqrt(jnp.sum(qh * qh, axis=-1, keepdims=True) + RMS_EPS) * (HEAD_DIM ** -0.5)
        kh = kh * lax.rsqrt(jnp.sum(kh * kh, axis=-1, keepdims=True) + RMS_EPS)
        beta = beta_all[b][rs, h:h + 1]
        gcol = gc[b, j][:, N_HEADS + h:N_HEADS + h + 1]
        grow = jnp.sum(eye_f * gcol, axis=0, keepdims=True)
        decay = jnp.where(incl, jnp.exp(jnp.where(incl, gcol - grow, 0.0)), 0.0)
        gamma = jnp.exp(gcol)
        g_last = gcol[c - 1:c, :]
        kb = kh.astype(BF16)
        pre[ch] = dict(
            a=beta * _dot_nt(kb, kb) * decay,
            rhs=jnp.concatenate([(beta * gamma) * kh, beta * vh], axis=1).astype(BF16),
            gq=(gamma * qh).astype(BF16),
            aqk=(_dot_nt(qh, kb) * decay).astype(BF16),
            kd=(kh * jnp.exp(g_last - gcol)).astype(BF16),
            eg=jnp.exp(g_last))

    xr = ti ^ si
    first = (xr < 2) & (si < ti)
    t_inv = {ch: eye_f - jnp.where(first, pre[ch]["a"], 0.0) for ch in chains}
    n = 2
    while n < c:
        lower = (xr < 2 * n) & ((ti & n) != 0) & ((si & n) == 0)
        x = {ch: _dot_inv(jnp.where(lower, pre[ch]["a"], 0.0), t_inv[ch]) for ch in chains}
        t_inv = {ch: t_inv[ch] - _dot_inv(t_inv[ch], x[ch]) for ch in chains}
        n *= 2
    wu = {ch: _dot_inv(t_inv[ch], pre[ch]["rhs"]) for ch in chains}

    for j in range(nc):
        rs = slice(j * c, (j + 1) * c)
        group = [(b, j, h) for b in range(nb) for h in range(N_HEADS)]
        s_old = {ch: s_ref[ch[0], ch[2]] for ch in group}
        ws = {ch: jnp.dot(jnp.concatenate([wu[ch][:, :HEAD_DIM].astype(BF16), pre[ch]["gq"]], axis=0),
                          s_old[ch].astype(BF16), preferred_element_type=F32) for ch in group}
        u = {ch: (wu[ch][:, HEAD_DIM:] - ws[ch][:c]).astype(BF16) for ch in group}
        for ch in group:
            b, _, h = ch
            col = slice(h * HEAD_DIM, (h + 1) * HEAD_DIM)
            s_ref[b, h] = pre[ch]["eg"] * s_old[ch] + _dot_tn(pre[ch]["kd"], u[ch])
            o = ws[ch][c:] + jnp.dot(pre[ch]["aqk"], u[ch], preferred_element_type=F32)
            on = o * lax.rsqrt(jnp.mean(o * o, axis=-1, keepdims=True) + RMS_EPS) * ng
            o_ref[b, rs, col] = on * _silu(gz_ref[b, rs, col])

    @pl.when(step == n_steps - 1)
    def _():
        for b in range(nb):
            for h in range(N_HEADS):
                so_ref[b, h] = s_ref[b, h]


def _gdn(p3, tail3, conv_w, a_log, dt_bias, norm_g, hist, s0, *, c, nc, nb, row_lo):
    bsz, t, _ = p3.shape
    rb = c * nc
    n_steps = t // rb
    has_state = s0 is not None
    n_hist = GD_CONV - 1
    blk = lambda j: pl.BlockSpec((nb, rb, MIX_HALF), lambda i, si, j=j: (i, si, j))
    st_spec = pl.BlockSpec((nb, N_HEADS, HEAD_DIM, HEAD_DIM), lambda i, si: (i, 0, 0, 0))
    par = jnp.zeros((SUBLANES, HEAD_DIM), F32)
    par = par.at[0, N_HEADS:2 * N_HEADS].set(a_log).at[1, N_HEADS:2 * N_HEADS].set(dt_bias)
    in_specs = [blk(4), blk(5), blk(6), blk(7),
                pl.BlockSpec((nb, rb, HEAD_DIM), lambda i, si: (i, si, 0)),
                pl.BlockSpec(conv_w.shape, lambda i, si: (0, 0)),
                pl.BlockSpec(par.shape, lambda i, si: (0, 0)),
                pl.BlockSpec((1, HEAD_DIM), lambda i, si: (0, 0))]
    args = [p3, p3, p3, p3, tail3, conv_w, par, norm_g.reshape(1, HEAD_DIM)]
    if has_state:
        in_specs += [blk(0), blk(1), blk(2), st_spec]
        args += [hist, hist, hist, s0]
    return pl.pallas_call(
        functools.partial(_gdn_kernel, c=c, nc=nc, nb=nb, has_state=has_state, row_lo=row_lo, n_steps=n_steps),
        grid=(bsz // nb, n_steps),
        in_specs=in_specs,
        out_specs=[pl.BlockSpec((nb, rb, MIX_HALF), lambda i, si: (i, si, 0)), st_spec,
                   pl.BlockSpec((nb, n_hist, 3 * MIX_HALF), lambda i, si: (i, 0, 0))],
        out_shape=[jax.ShapeDtypeStruct((bsz, t, MIX_HALF), F32),
                   jax.ShapeDtypeStruct((bsz, N_HEADS, HEAD_DIM, HEAD_DIM), F32),
                   jax.ShapeDtypeStruct((bsz, n_hist, 3 * MIX_HALF), F32)],
        scratch_shapes=[pltpu.VMEM((nb, N_HEADS, HEAD_DIM, HEAD_DIM), F32),
                        pltpu.VMEM((nb, SUBLANES + rb, 3 * MIX_HALF), F32)],
        compiler_params=_params(2),
        name="gdn_chunk",
    )(*args)


def _attn_kernel(q_ref, k_ref, v_ref, o_ref, *, nb):
    scale = MEM_HEAD_DIM ** -0.5
    pairs = [(b, h) for b in range(nb) for h in range(MEM_HEADS)]

    def head_block(ref, b, h):
        if ref.shape[-1] == LANES:
            rows_per_mem = MEM_HEADS * MEM_HEAD_DIM // LANES
            return jnp.concatenate([ref[b, pl.ds(h + MEM_HEADS * j, N_MEM, stride=rows_per_mem), :]
                                    for j in range(MEM_HEAD_DIM // LANES)], axis=1)
        return ref[b, :, h * MEM_HEAD_DIM:(h + 1) * MEM_HEAD_DIM]

    s = {bh: _dot_nt(q_ref[bh[0], :, bh[1] * MEM_HEAD_DIM:(bh[1] + 1) * MEM_HEAD_DIM],
                     head_block(k_ref, *bh)) * scale for bh in pairs}
    p = {}
    for bh in pairs:
        e = jnp.exp(s[bh] - jnp.max(s[bh], axis=-1, keepdims=True))
        p[bh] = e / jnp.sum(e, axis=-1, keepdims=True)
    for b, h in pairs:
        o_ref[b, :, h * MEM_HEAD_DIM:(h + 1) * MEM_HEAD_DIM] = _dot(p[b, h], head_block(v_ref, b, h))


def _mem_attn_core(q3, mk, mv, *, tq, nb):
    bsz, t, d = q3.shape
    tq = min(tq, t)
    kv_spec = pl.BlockSpec((nb,) + mk.shape[1:], lambda i, r: (i, 0, 0))
    return pl.pallas_call(
        functools.partial(_attn_kernel, nb=nb),
        grid=(bsz // nb, t // tq),
        in_specs=[pl.BlockSpec((nb, tq, d), lambda i, r: (i, r, 0)), kv_spec, kv_spec],
        out_specs=pl.BlockSpec((nb, tq, d), lambda i, r: (i, r, 0)),
        out_shape=jax.ShapeDtypeStruct((bsz, t, d), F32),
        compiler_params=_params(2),
        name="mem_attn_core",
    )(q3, mk, mv)


def _ffn_up_kernel(*refs, tm, grp, row_lo, has_hist):
    if has_hist:
        h_ref, wg_ref, wv_ref, cw_ref, cb_ref, hist_ref, act_ref, buf_ref, xp_ref = refs
    else:
        h_ref, wg_ref, wv_ref, cw_ref, cb_ref, act_ref, buf_ref, xp_ref = refs
        hist_ref = None
    n_hist = FFN_CONV - 1
    r = pl.program_id(2)

    @pl.when(r == 0)
    def _():
        xp_ref[0:SUBLANES, :] = jnp.zeros((SUBLANES, xp_ref.shape[1]), F32)

    hb = h_ref[0].astype(BF16)
    gate = jnp.dot(hb, wg_ref[...], preferred_element_type=F32)
    val = jnp.dot(hb, wv_ref[...], preferred_element_type=F32)
    if hist_ref is not None:
        pos = lax.broadcasted_iota(jnp.int32, (tm, 1), 0) % grp
        gate = jnp.where((pos >= row_lo - n_hist) & (pos < row_lo), hist_ref[0], gate)
    xp_ref[SUBLANES:SUBLANES + tm, :] = gate
    y = gate * cw_ref[n_hist:n_hist + 1, :] + cb_ref[...]
    for tap in range(n_hist):
        off = SUBLANES - n_hist + tap
        y = y + xp_ref[off:off + tm, :] * cw_ref[tap:tap + 1, :]
    gelu = 0.5 * y * (1.0 + lax.erf(y * (2.0 ** -0.5)))
    act_ref[0] = (gelu * val).astype(act_ref.dtype)
    if buf_ref.shape[1] == n_hist:
        buf_ref[0] = xp_ref[SUBLANES + tm - n_hist:SUBLANES + tm, :]
    else:
        buf_ref[0] = gate
    xp_ref[0:SUBLANES, :] = xp_ref[tm:tm + SUBLANES, :]


def _ffn_up(h3, w_gate, w_val, conv_w, conv_b, hist, *, tm, tn, grp, row_lo):
    g, r, d = h3.shape
    f = w_gate.shape[1]
    tm = min(tm, r)
    n_hist = FFN_CONV - 1
    has_hist = hist is not None
    assert grp == r or grp <= tm
    if grp == r:
        buf_spec = pl.BlockSpec((1, n_hist, tn), lambda gi, fi, ri: (gi, 0, fi))
        buf_rows = n_hist
    else:
        buf_spec = pl.BlockSpec((1, tm, tn), lambda gi, fi, ri: (gi, ri, fi))
        buf_rows = r
    w_spec = pl.BlockSpec((d, tn), lambda gi, fi, ri: (0, fi))
    in_specs = [pl.BlockSpec((1, tm, d), lambda gi, fi, ri: (gi, ri, 0)), w_spec, w_spec,
                pl.BlockSpec((FFN_CONV, tn), lambda gi, fi, ri: (0, fi)),
                pl.BlockSpec((1, tn), lambda gi, fi, ri: (0, fi))]
    args = [h3, w_gate, w_val, conv_w, conv_b.reshape(1, f)]
    if has_hist:
        in_specs.append(pl.BlockSpec((1, tm, tn), lambda gi, fi, ri: (gi, ri, fi)))
        args.append(hist)
    return pl.pallas_call(
        functools.partial(_ffn_up_kernel, tm=tm, grp=grp, row_lo=row_lo, has_hist=has_hist),
        grid=(g, f // tn, r // tm),
        in_specs=in_specs,
        out_specs=[pl.BlockSpec((1, tm, tn), lambda gi, fi, ri: (gi, ri, fi)),
                   buf_spec],
        out_shape=[jax.ShapeDtypeStruct((g, r, f), BF16),
                   jax.ShapeDtypeStruct((g, buf_rows, f), F32)],
        scratch_shapes=[pltpu.VMEM((SUBLANES + tm, tn), F32)],
        compiler_params=_params(3),
        name="ffn_up",
    )(*args)


def _layer(x3, s_hg, s_gd, hist_gd, hist_ffn, mk, mv, weights, *, row_lo, grp, ffn_streams, c_hg, c_gd,
           nc_gd, nb_rec, nb_attn):
    (lb_logits, w_in_main, w_in_tail, w_gd_conv, gd_a_log, gd_dt_bias, hg_norm_g, gd_norm_g,
     w_out_hg, w_out_gd, ln1_g, ln1_b, w_mq, w_mo, ln2_g, ln2_b,
     w_gate, w_val, w_ffn_conv, b_ffn_conv, w_down, ln3_g, ln3_b) = weights
    bsz, t, d = x3.shape
    rows = bsz * t
    x2 = x3.reshape(rows, d)

    proj = _matmul(x2, w_in_main, 1024, 512).reshape(bsz, t, -1)
    tail = _matmul(x2, w_in_tail, 1024, HEAD_DIM).reshape(bsz, t, HEAD_DIM)

    o_hg, new_hg = _hgrn(proj, lb_logits, hg_norm_g, s_hg, c=c_hg, nb=nb_rec, row_lo=row_lo)
    o_gd, new_gd, new_buf_gd = _gdn(proj, tail, w_gd_conv, gd_a_log, gd_dt_bias, gd_norm_g, hist_gd, s_gd,
                                     c=c_gd, nc=nc_gd, nb=nb_rec, row_lo=row_lo)

    h1 = _matmul_res_ln([o_hg.reshape(rows, MIX_HALF), o_gd.reshape(rows, MIX_HALF)],
                        [w_out_hg, w_out_gd], x2, ln1_g, ln1_b, 512)
    q = _matmul(h1, w_mq, 1024, 1024)
    att = _mem_attn_core(q.reshape(bsz, t, d), mk, mv, tq=512, nb=nb_attn)
    h2 = _matmul_res_ln([att.reshape(rows, d)], [w_mo], h1, ln2_g, ln2_b, 512)

    h2_3 = h2.reshape(ffn_streams, rows // ffn_streams, d)
    hist3 = None if hist_ffn is None else hist_ffn.reshape(ffn_streams, rows // ffn_streams, D_FF)
    act, new_buf_ffn = _ffn_up(h2_3, w_gate, w_val, w_ffn_conv, b_ffn_conv, hist3,
                               tm=512, tn=D_FF // 2, grp=grp, row_lo=row_lo)
    y = _matmul_res_ln([act.reshape(rows, D_FF)], [w_down], h2, ln3_g, ln3_b, 512)
    return y.reshape(bsz, t, d), new_hg, new_gd, new_buf_gd, new_buf_ffn


def _cache_rows(cache):
    b, m, h, d = cache.shape
    return cache.reshape(b, m, h, d // LANES, LANES).transpose(0, 1, 3, 2, 4).reshape(b, m * h * d // LANES, LANES)


def kernel(x_prompt, x_sample, state_hgrn, state_gdn, state_gdn_conv, state_ffn_conv, cache_mem_k, cache_mem_v, mem_prompt, hgrn_lb_logits, w_in, w_gd_conv, gd_a_log, gd_dt_bias, hg_norm_g, gd_norm_g, w_out, ln1_g, ln1_b, w_mq, w_mkv, w_mo, ln2_g, ln2_b, w_up, w_ffn_conv, b_ffn_conv, w_down, ln3_g, ln3_b):
    bp, tp, d = x_prompt.shape
    bs, ts, _ = x_sample.shape
    tpad = SUBLANES
    row_lo = tpad - ts
    l = 0

    main_cols = 8 * MIX_HALF
    w_in_tail = jnp.pad(w_in[l][:, main_cols:], ((0, 0), (0, HEAD_DIM - 2 * N_HEADS))).astype(BF16)
    weights = (hgrn_lb_logits, w_in[l][:, :main_cols].astype(BF16), w_in_tail, w_gd_conv[l],
               gd_a_log[l], gd_dt_bias[l], hg_norm_g[l], gd_norm_g[l],
               w_out[l][:MIX_HALF].astype(BF16), w_out[l][MIX_HALF:].astype(BF16), ln1_g[l], ln1_b[l],
               w_mq[l].astype(BF16), w_mo[l].astype(BF16), ln2_g[l], ln2_b[l],
               w_up[l][:, :D_FF].astype(BF16), w_up[l][:, D_FF:].astype(BF16),
               w_ffn_conv[l], b_ffn_conv[l], w_down[l].astype(BF16), ln3_g[l], ln3_b[l])

    mem2 = mem_prompt.reshape(bp * N_MEM, d)
    mk = _matmul(mem2, w_mkv[l][:, :d].astype(BF16), 1024, 1024).reshape(bp, N_MEM, d)
    mv = _matmul(mem2, w_mkv[l][:, d:].astype(BF16), 1024, 1024).reshape(bp, N_MEM, d)
    yp, p_hg, p_gd, p_bgd, p_bff = _layer(x_prompt, None, None, None, None, mk, mv, weights,
                                          row_lo=0, grp=tp, ffn_streams=bp, c_hg=128, c_gd=128, nc_gd=2,
                                          nb_rec=1, nb_attn=1)

    n_hg = GD_CONV - 1
    n_hf = FFN_CONV - 1
    xs = jnp.pad(x_sample, ((0, 0), (row_lo, 0), (0, 0)))
    hist_gd = jnp.pad(state_gdn_conv[l], ((0, 0), (row_lo - n_hg, ts), (0, 0)))
    hist_ffn = jnp.pad(state_ffn_conv[l], ((0, 0), (row_lo - n_hf, ts), (0, 0)))
    ys, s_hg, s_gd, s_bgd, s_bff = _layer(
        xs, state_hgrn[l], state_gdn[l], hist_gd, hist_ffn,
        _cache_rows(cache_mem_k[l]), _cache_rows(cache_mem_v[l]), weights,
        row_lo=row_lo, grp=tpad, ffn_streams=1, c_hg=tpad, c_gd=tpad, nc_gd=1,
        nb_rec=8, nb_attn=4)
    ys = ys[:, row_lo:]
    s_bff = s_bff.reshape(bs, tpad, D_FF)[:, tpad - n_hf:]

    mem_shape = (1, bp, N_MEM, MEM_HEADS, MEM_HEAD_DIM)
    return (yp, ys, p_hg[None], p_gd[None], p_bgd[None], p_bff[None],
            mk.reshape(mem_shape), mv.reshape(mem_shape),
            s_hg[None], s_gd[None], s_bgd[None], s_bff[None])
```

```python
import functools

import jax
import jax.numpy as jnp
from jax import lax
from jax.experimental import pallas as pl
from jax.experimental.pallas import tpu as pltpu

F32 = jnp.float32
BF16 = jnp.bfloat16

D_MODEL = 1024
HEAD_DIM = 128
N_HEADS = 4
MIX_HALF = N_HEADS * HEAD_DIM
GD_CONV = 4
FFN_CONV = 3
D_FF = 2816
N_MEM = 256
MEM_HEADS = 4
MEM_HEAD_DIM = D_MODEL // MEM_HEADS
LN_EPS = 1e-5
RMS_EPS = 1e-6
DEPTH = 1
ALPHA = (2.0 * DEPTH) ** 0.25

SUBLANES = 8
LANES = 128
VMEM_LIMIT = 56 * 1024 * 1024


def _params(n_axes):
    return pltpu.CompilerParams(dimension_semantics=("arbitrary",) * n_axes,
                                vmem_limit_bytes=VMEM_LIMIT)


def _dot(a, b):
    return jnp.dot(a.astype(BF16), b.astype(BF16), preferred_element_type=F32)


def _dot_nt(a, b):
    return lax.dot_general(a.astype(BF16), b.astype(BF16), (((1,), (1,)), ((), ())),
                           preferred_element_type=F32)


def _dot_tn(a, b):
    return lax.dot_general(a.astype(BF16), b.astype(BF16), (((0,), (0,)), ((), ())),
                           preferred_element_type=F32)


def _split3(x):
    x1 = x.astype(BF16)
    r = x - x1.astype(F32)
    x2 = r.astype(BF16)
    x3 = (r - x2.astype(F32)).astype(BF16)
    return x1, x2, x3


def _cumsum_rows(tri_bf16, x):
    x1, x2, x3 = _split3(x)
    return (jnp.dot(tri_bf16, x1, preferred_element_type=F32)
            + jnp.dot(tri_bf16, x2, preferred_element_type=F32)
            + jnp.dot(tri_bf16, x3, preferred_element_type=F32))


def _silu(x):
    return x * jax.nn.sigmoid(x)


def _mm_kernel(x_ref, w_ref, o_ref, xb_ref):
    @pl.when(pl.program_id(1) == 0)
    def _():
        xb_ref[...] = x_ref[...].astype(BF16)

    o_ref[...] = jnp.dot(xb_ref[...], w_ref[...], preferred_element_type=F32).astype(o_ref.dtype)


def _matmul(x, w, tm, tn):
    m, k = x.shape
    n = w.shape[1]
    tm = min(tm, m)
    tn = min(tn, n)
    return pl.pallas_call(
        _mm_kernel,
        grid=(m // tm, n // tn),
        in_specs=[pl.BlockSpec((tm, k), lambda i, j: (i, 0)),
                  pl.BlockSpec((k, tn), lambda i, j: (0, j))],
        out_specs=pl.BlockSpec((tm, tn), lambda i, j: (i, j)),
        out_shape=jax.ShapeDtypeStruct((m, n), F32),
        scratch_shapes=[pltpu.VMEM((tm, k), BF16)],
        compiler_params=_params(2),
        name="proj_matmul",
    )(x, w)


def _mm_ln_kernel(*refs, n_in):
    a_refs = refs[:n_in]
    w_refs = refs[n_in:2 * n_in]
    res_ref, g_ref, b_ref, o_ref = refs[2 * n_in:]
    acc = ALPHA * res_ref[...]
    for a_ref, w_ref in zip(a_refs, w_refs):
        acc = acc + jnp.dot(a_ref[...].astype(BF16), w_ref[...], preferred_element_type=F32)
    mu = jnp.mean(acc, axis=-1, keepdims=True)
    xc = acc - mu
    var = jnp.mean(xc * xc, axis=-1, keepdims=True)
    o_ref[...] = xc * lax.rsqrt(var + LN_EPS) * g_ref[...] + b_ref[...]


def _matmul_res_ln(a_list, w_list, res, g, b, tm):
    m, d = res.shape
    tm = min(tm, m)
    n_in = len(a_list)
    in_specs = ([pl.BlockSpec((tm, a.shape[1]), lambda i: (i, 0)) for a in a_list]
                + [pl.BlockSpec(w.shape, lambda i: (0, 0)) for w in w_list]
                + [pl.BlockSpec((tm, d), lambda i: (i, 0)),
                   pl.BlockSpec((1, d), lambda i: (0, 0)),
                   pl.BlockSpec((1, d), lambda i: (0, 0))])
    return pl.pallas_call(
        functools.partial(_mm_ln_kernel, n_in=n_in),
        grid=(m // tm,),
        in_specs=in_specs,
        out_specs=pl.BlockSpec((tm, d), lambda i: (i, 0)),
        out_shape=jax.ShapeDtypeStruct((m, d), F32),
        compiler_params=_params(1),
        name="proj_res_ln",
    )(*a_list, *w_list, res, g.reshape(1, d), b.reshape(1, d))


HG_COL = 0
GD_COL = 4 * MIX_HALF


def _ref_rows(gc_ref, b, col, m, c):
    blk = 2 * m
    if blk >= SUBLANES:
        parts = [jnp.broadcast_to(gc_ref[b, pl.ds(j * blk + m - 1, 1), col], (blk, HEAD_DIM))
                 for j in range(c // blk)]
    else:
        sub = lax.broadcasted_iota(jnp.int32, (SUBLANES, HEAD_DIM), 0)
        parts = []
        for i in range(c // SUBLANES):
            tile = None
            for j in range(SUBLANES // blk):
                row = jnp.broadcast_to(gc_ref[b, pl.ds(i * SUBLANES + j * blk + m - 1, 1), col],
                                       (SUBLANES, HEAD_DIM))
                tile = row if tile is None else jnp.where(sub >= j * blk, row, tile)
            parts.append(tile)
    return parts[0] if len(parts) == 1 else jnp.concatenate(parts, axis=0)


def _hgrn_chunk(p_ref, mix_ref, lb, ng, st_ref, gc_ref, *, r0, c, nb, row_lo):
    rows = lax.broadcasted_iota(jnp.int32, (c, 1), 0)
    ti = lax.broadcasted_iota(jnp.int32, (c, c), 0)
    si = lax.broadcasted_iota(jnp.int32, (c, c), 1)
    tri = (si <= ti).astype(F32).astype(BF16)
    eye = ti == si
    xr = ti ^ si
    pairs = [(b, h) for b in range(nb) for h in range(N_HEADS)]
    cols = [slice(h * HEAD_DIM, (h + 1) * HEAD_DIM) for h in range(N_HEADS)]
    rs = pl.ds(r0, c)

    def proj(b, j):
        return p_ref[b, rs, HG_COL + j * MIX_HALF:HG_COL + (j + 1) * MIX_HALF]

    q, k, gc = {}, {}, {}
    for b in range(nb):
        f = lb + (1.0 - lb) * jax.nn.sigmoid(proj(b, 1))
        g = jnp.log(f)
        kb = 1.0 - f
        if row_lo:
            g = jnp.where(rows >= row_lo, g, 0.0)
            kb = jnp.where(rows >= row_lo, kb, 0.0)
        qb = _silu(proj(b, 0))
        gcb = _cumsum_rows(tri, g)
        gc_ref[b] = gcb
        for h in range(N_HEADS):
            q[b, h], k[b, h], gc[b, h] = qb[:, cols[h]], kb[:, cols[h]], gcb[:, cols[h]]

    sc = {bh: jnp.where(eye, jnp.sum(q[bh] * k[bh], axis=1, keepdims=True), 0.0) for bh in pairs}
    m = c // 2
    while m >= 1:
        level = (xr >= m) & (xr < 2 * m) & (si < ti)
        prod = {}
        for bh in pairs:
            w = jnp.exp(-jnp.abs(gc[bh] - _ref_rows(gc_ref, bh[0], cols[bh[1]], m, c)))
            prod[bh] = _dot_nt(q[bh] * w, k[bh] * w)
        sc = {bh: jnp.where(level, prod[bh], sc[bh]) for bh in pairs}
        m //= 2

    v = {(b, h): p_ref[b, rs, HG_COL + 2 * MIX_HALF + h * HEAD_DIM:HG_COL + 2 * MIX_HALF + (h + 1) * HEAD_DIM]
         for b, h in pairs}
    st = {bh: st_ref[bh] for bh in pairs}
    o = {bh: _dot(sc[bh], v[bh]) + _dot_nt(q[bh] * jnp.exp(gc[bh]), st[bh]) for bh in pairs}
    for b, h in pairs:
        g_last = gc_ref[b, pl.ds(c - 1, 1), cols[h]]
        st_ref[b, h] = st[b, h] * jnp.exp(g_last) + _dot_tn(v[b, h], k[b, h] * jnp.exp(g_last - gc[b, h]))
    for b, h in pairs:
        ob = o[b, h]
        on = ob * lax.rsqrt(jnp.mean(ob * ob, axis=-1, keepdims=True) + RMS_EPS) * ng
        gate = p_ref[b, rs, HG_COL + 3 * MIX_HALF + h * HEAD_DIM:HG_COL + 3 * MIX_HALF + (h + 1) * HEAD_DIM]
        mix_ref[b, rs, cols[h]] = on * _silu(gate)


def _gdn_rows(p_ref, t_ref, hist_ref, mix_ref, cb_ref, cw_ref, neg_a, dt_bias, ng, s_ref, xp_ref,
              *, c, nc, nb, row_lo):
    n_hist = GD_CONV - 1
    rb = c * nc
    rows = lax.broadcasted_iota(jnp.int32, (rb, 1), 0) % c
    ti = lax.broadcasted_iota(jnp.int32, (c, c), 0)
    si = lax.broadcasted_iota(jnp.int32, (c, c), 1)
    tri = (si <= ti).astype(F32).astype(BF16)
    eye_f = (ti == si).astype(F32)
    incl = si <= ti

    chains = [(b, j, h) for b in range(nb) for j in range(nc) for h in range(N_HEADS)]
    q_all, k_all, v_all, beta_all, la_all = {}, {}, {}, {}, {}
    for b in range(nb):
        qkv = []
        for j in range(3):
            col = slice(j * MIX_HALF, (j + 1) * MIX_HALF)
            x = p_ref[b, :, GD_COL + j * MIX_HALF:GD_COL + (j + 1) * MIX_HALF]
            if hist_ref is not None:
                x = jnp.where((rows >= row_lo - n_hist) & (rows < row_lo), hist_ref[b, :, col], x)
            xp_ref[b, SUBLANES:SUBLANES + rb, col] = x
            y = x * cw_ref[n_hist:n_hist + 1, col]
            for tap in range(n_hist):
                off = SUBLANES - n_hist + tap
                y = y + xp_ref[b, off:off + rb, col] * cw_ref[tap:tap + 1, col]
            qkv.append(_silu(y))
        cb_ref[b] = xp_ref[b, SUBLANES + rb - n_hist:SUBLANES + rb, :]
        xp_ref[b, 0:SUBLANES, :] = xp_ref[b, rb:rb + SUBLANES, :]
        q_all[b], k_all[b], v_all[b] = qkv

        tail = t_ref[b]
        beta_all[b] = jax.nn.sigmoid(tail)
        la_all[b] = neg_a * jax.nn.softplus(tail + dt_bias)
        if row_lo:
            beta_all[b] = jnp.where(rows >= row_lo, beta_all[b], 0.0)
            la_all[b] = jnp.where(rows >= row_lo, la_all[b], 0.0)

    gc = {(b, j): _cumsum_rows(tri, la_all[b][j * c:(j + 1) * c]) for b in range(nb) for j in range(nc)}

    pre = {}
    for ch in chains:
        b, j, h = ch
        rs = slice(j * c, (j + 1) * c)
        col = slice(h * HEAD_DIM, (h + 1) * HEAD_DIM)
        qh, kh, vh = q_all[b][rs, col], k_all[b][rs, col], v_all[b][rs, col]
        qh = qh * lax.rsqrt(jnp.sum(qh * qh, axis=-1, keepdims=True) + RMS_EPS) * (HEAD_DIM ** -0.5)
        kh = kh * lax.rsqrt(jnp.sum(kh * kh, axis=-1, keepdims=True) + RMS_EPS)
        beta = beta_all[b][rs, h:h + 1]
        gcol = gc[b, j][:, N_HEADS + h:N_HEADS + h + 1]
        grow = jnp.sum(eye_f * gcol, axis=0, keepdims=True)
        decay = jnp.where(incl, jnp.exp(jnp.where(incl, gcol - grow, 0.0)), 0.0)
        gamma = jnp.exp(gcol)
        g_last = gcol[c - 1:c, :]
        kb = kh.astype(BF16)
        pre[ch] = dict(
            a=beta * _dot_nt(kb, kb) * decay,
            rhs=jnp.concatenate([(beta * gamma) * kh, beta * vh], axis=1).astype(BF16),
            gq=(gamma * qh).astype(BF16),
            aqk=(_dot_nt(qh, kb) * decay).astype(BF16),
            kd=(kh * jnp.exp(g_last - gcol)).astype(BF16),
            eg=jnp.exp(g_last))

    xr = ti ^ si
    first = (xr < 2) & (si < ti)
    t_inv = {ch: eye_f - jnp.where(first, pre[ch]["a"], 0.0) for ch in chains}
    n = 2
    while n < c:
        lower = (xr < 2 * n) & ((ti & n) != 0) & ((si & n) == 0)
        x = {ch: _dot(jnp.where(lower, pre[ch]["a"], 0.0), t_inv[ch]) for ch in chains}
        t_inv = {ch: t_inv[ch] - _dot(t_inv[ch], x[ch]) for ch in chains}
        n *= 2
    wu = {ch: _dot(t_inv[ch], pre[ch]["rhs"]) for ch in chains}

    for j in range(nc):
        rs = slice(j * c, (j + 1) * c)
        group = [(b, j, h) for b in range(nb) for h in range(N_HEADS)]
        s_old = {ch: s_ref[ch[0], ch[2]] for ch in group}
        ws = {ch: jnp.dot(jnp.concatenate([wu[ch][:, :HEAD_DIM].astype(BF16), pre[ch]["gq"]], axis=0),
                          s_old[ch].astype(BF16), preferred_element_type=F32) for ch in group}
        u = {ch: (wu[ch][:, HEAD_DIM:] - ws[ch][:c]).astype(BF16) for ch in group}
        for ch in group:
            b, _, h = ch
            s_ref[b, h] = pre[ch]["eg"] * s_old[ch] + _dot_tn(pre[ch]["kd"], u[ch])
            o = ws[ch][c:] + jnp.dot(pre[ch]["aqk"], u[ch], preferred_element_type=F32)
            on = o * lax.rsqrt(jnp.mean(o * o, axis=-1, keepdims=True) + RMS_EPS) * ng
            gate = p_ref[b, rs, GD_COL + 3 * MIX_HALF + h * HEAD_DIM:GD_COL + 3 * MIX_HALF + (h + 1) * HEAD_DIM]
            mix_ref[b, rs, MIX_HALF + h * HEAD_DIM:MIX_HALF + (h + 1) * HEAD_DIM] = on * _silu(gate)


def _mixer_kernel(*refs, c, nc, nb, fuse_proj, has_state, row_lo, n_steps):
    refs = list(refs)
    if fuse_proj:
        x_ref, w_ref, wt_ref = refs[:3]
        del refs[:3]
    else:
        p_ref, t_ref = refs[:2]
        del refs[:2]
    lbl_ref, hng_ref, cw_ref, par_ref, gng_ref = refs[:5]
    del refs[:5]
    if has_state:
        hist_ref, shg0_ref, sgd0_ref = refs[:3]
        del refs[:3]
    else:
        hist_ref = shg0_ref = sgd0_ref = None
    mix_ref, shg_ref, sgd_ref, cb_ref, st_ref, s_ref, gc_ref, xp_ref = refs[:8]
    if fuse_proj:
        p_ref, t_ref = refs[8:]
    step = pl.program_id(1)
    pairs = [(b, h) for b in range(nb) for h in range(N_HEADS)]

    @pl.when(step == 0)
    def _():
        for b, h in pairs:
            if has_state:
                st_ref[b, h] = shg0_ref[b, h].T
                s_ref[b, h] = sgd0_ref[b, h]
            else:
                st_ref[b, h] = jnp.zeros((HEAD_DIM, HEAD_DIM), F32)
                s_ref[b, h] = jnp.zeros((HEAD_DIM, HEAD_DIM), F32)
        for b in range(nb):
            xp_ref[b, 0:SUBLANES, :] = jnp.zeros((SUBLANES, 3 * MIX_HALF), F32)

    if fuse_proj:
        for b in range(nb):
            xb = x_ref[b].astype(BF16)
            p_ref[b] = jnp.dot(xb, w_ref[...], preferred_element_type=F32)
            t_ref[b] = jnp.dot(xb, wt_ref[...], preferred_element_type=F32)

    lbl = lbl_ref[...]
    e = jnp.exp(lbl - jnp.max(lbl, axis=0, keepdims=True))
    lb = e[0:1] / jnp.sum(e, axis=0, keepdims=True)
    neg_a = -jnp.exp(par_ref[0:1, :])
    dt_bias = par_ref[1:2, :]

    for j in range(nc):
        _hgrn_chunk(p_ref, mix_ref, lb, hng_ref[...], st_ref, gc_ref, r0=j * c, c=c, nb=nb, row_lo=row_lo)
    _gdn_rows(p_ref, t_ref, hist_ref, mix_ref, cb_ref, cw_ref, neg_a, dt_bias, gng_ref[...], s_ref, xp_ref,
              c=c, nc=nc, nb=nb, row_lo=row_lo)

    @pl.when(step == n_steps - 1)
    def _():
        for b, h in pairs:
            shg_ref[b, h] = st_ref[b, h].T
            sgd_ref[b, h] = s_ref[b, h]


def _mixer(x3, proj, tail, weights, hist, s_hg, s_gd, *, c, nc, nb, row_lo):
    lb_logits, w_main, w_tail, conv_w, a_log, dt_bias, hg_norm_g, gd_norm_g = weights
    fuse_proj = proj is None
    bsz, t = (x3 if fuse_proj else proj).shape[:2]
    rb = c * nc
    n_steps = t // rb
    has_state = s_hg is not None
    n_hist = GD_CONV - 1
    n_proj = w_main.shape[1]
    const = lambda shape: pl.BlockSpec(shape, lambda i, si: (0,) * len(shape))
    rows3 = lambda width: pl.BlockSpec((nb, rb, width), lambda i, si: (i, si, 0))
    st_spec = pl.BlockSpec((nb, N_HEADS, HEAD_DIM, HEAD_DIM), lambda i, si: (i, 0, 0, 0))
    par = jnp.zeros((SUBLANES, HEAD_DIM), F32)
    par = par.at[0, N_HEADS:2 * N_HEADS].set(a_log).at[1, N_HEADS:2 * N_HEADS].set(dt_bias)
    if fuse_proj:
        in_specs = [rows3(D_MODEL), const(w_main.shape), const(w_tail.shape)]
        args = [x3, w_main, w_tail]
    else:
        in_specs = [rows3(n_proj), rows3(HEAD_DIM)]
        args = [proj, tail]
    in_specs += [const(lb_logits.shape), const((1, HEAD_DIM)), const(conv_w.shape), const(par.shape),
                 const((1, HEAD_DIM))]
    args += [lb_logits, hg_norm_g.reshape(1, HEAD_DIM), conv_w, par, gd_norm_g.reshape(1, HEAD_DIM)]
    if has_state:
        in_specs += [rows3(3 * MIX_HALF), st_spec, st_spec]
        args += [hist, s_hg, s_gd]
    scratch = [pltpu.VMEM((nb, N_HEADS, HEAD_DIM, HEAD_DIM), F32),
               pltpu.VMEM((nb, N_HEADS, HEAD_DIM, HEAD_DIM), F32),
               pltpu.VMEM((nb, c, MIX_HALF), F32),
               pltpu.VMEM((nb, SUBLANES + rb, 3 * MIX_HALF), F32)]
    if fuse_proj:
        scratch += [pltpu.VMEM((nb, rb, n_proj), F32), pltpu.VMEM((nb, rb, HEAD_DIM), F32)]
    return pl.pallas_call(
        functools.partial(_mixer_kernel, c=c, nc=nc, nb=nb, fuse_proj=fuse_proj, has_state=has_state,
                          row_lo=row_lo, n_steps=n_steps),
        grid=(bsz // nb, n_steps),
        in_specs=in_specs,
        out_specs=[rows3(D_MODEL), st_spec, st_spec,
                   pl.BlockSpec((nb, n_hist, 3 * MIX_HALF), lambda i, si: (i, 0, 0))],
        out_shape=[jax.ShapeDtypeStruct((bsz, t, D_MODEL), F32),
                   jax.ShapeDtypeStruct((bsz, N_HEADS, HEAD_DIM, HEAD_DIM), F32),
                   jax.ShapeDtypeStruct((bsz, N_HEADS, HEAD_DIM, HEAD_DIM), F32),
                   jax.ShapeDtypeStruct((bsz, n_hist, 3 * MIX_HALF), F32)],
        scratch_shapes=scratch,
        compiler_params=_params(2),
        name="mixer",
    )(*args)


def _attn_kernel(q_ref, k_ref, v_ref, o_ref, *, nb):
    scale = MEM_HEAD_DIM ** -0.5
    pairs = [(b, h) for b in range(nb) for h in range(MEM_HEADS)]

    def head_block(ref, b, h):
        if ref.shape[-1] == LANES:
            rows_per_mem = MEM_HEADS * MEM_HEAD_DIM // LANES
            return jnp.concatenate([ref[b, pl.ds(h + MEM_HEADS * j, N_MEM, stride=rows_per_mem), :]
                                    for j in range(MEM_HEAD_DIM // LANES)], axis=1)
        return ref[b, :, h * MEM_HEAD_DIM:(h + 1) * MEM_HEAD_DIM]

    s = {bh: _dot_nt(q_ref[bh[0], :, bh[1] * MEM_HEAD_DIM:(bh[1] + 1) * MEM_HEAD_DIM],
                     head_block(k_ref, *bh)) * scale for bh in pairs}
    p = {}
    for bh in pairs:
        e = jnp.exp(s[bh] - jnp.max(s[bh], axis=-1, keepdims=True))
        p[bh] = e / jnp.sum(e, axis=-1, keepdims=True)
    for b, h in pairs:
        o_ref[b, :, h * MEM_HEAD_DIM:(h + 1) * MEM_HEAD_DIM] = _dot(p[b, h], head_block(v_ref, b, h))


def _mem_attn_core(q3, mk, mv, *, tq, nb):
    bsz, t, d = q3.shape
    tq = min(tq, t)
    kv_spec = pl.BlockSpec((nb,) + mk.shape[1:], lambda i, r: (i, 0, 0))
    return pl.pallas_call(
        functools.partial(_attn_kernel, nb=nb),
        grid=(bsz // nb, t // tq),
        in_specs=[pl.BlockSpec((nb, tq, d), lambda i, r: (i, r, 0)), kv_spec, kv_spec],
        out_specs=pl.BlockSpec((nb, tq, d), lambda i, r: (i, r, 0)),
        out_shape=jax.ShapeDtypeStruct((bsz, t, d), F32),
        compiler_params=_params(2),
        name="mem_attn_core",
    )(q3, mk, mv)


def _ffn_up_kernel(*refs, tm, grp, row_lo, has_hist):
    if has_hist:
        h_ref, wg_ref, wv_ref, cw_ref, cb_ref, hist_ref, act_ref, buf_ref, xp_ref = refs
    else:
        h_ref, wg_ref, wv_ref, cw_ref, cb_ref, act_ref, buf_ref, xp_ref = refs
        hist_ref = None
    n_hist = FFN_CONV - 1
    r = pl.program_id(2)

    @pl.when(r == 0)
    def _():
        xp_ref[0:SUBLANES, :] = jnp.zeros((SUBLANES, xp_ref.shape[1]), F32)

    hb = h_ref[0].astype(BF16)
    gate = jnp.dot(hb, wg_ref[...], preferred_element_type=F32)
    val = jnp.dot(hb, wv_ref[...], preferred_element_type=F32)
    if hist_ref is not None:
        pos = lax.broadcasted_iota(jnp.int32, (tm, 1), 0) % grp
        gate = jnp.where((pos >= row_lo - n_hist) & (pos < row_lo), hist_ref[0], gate)
    xp_ref[SUBLANES:SUBLANES + tm, :] = gate
    y = gate * cw_ref[n_hist:n_hist + 1, :] + cb_ref[...]
    for tap in range(n_hist):
        off = SUBLANES - n_hist + tap
        y = y + xp_ref[off:off + tm, :] * cw_ref[tap:tap + 1, :]
    gelu = 0.5 * y * (1.0 + lax.erf(y * (2.0 ** -0.5)))
    act_ref[0] = (gelu * val).astype(act_ref.dtype)
    if buf_ref.shape[1] == n_hist:
        buf_ref[0] = xp_ref[SUBLANES + tm - n_hist:SUBLANES + tm, :]
    else:
        buf_ref[0] = gate
    xp_ref[0:SUBLANES, :] = xp_ref[tm:tm + SUBLANES, :]


def _ffn_up(h3, w_gate, w_val, conv_w, conv_b, hist, *, tm, tn, grp, row_lo):
    g, r, d = h3.shape
    f = w_gate.shape[1]
    tm = min(tm, r)
    n_hist = FFN_CONV - 1
    has_hist = hist is not None
    assert grp == r or grp <= tm
    if grp == r:
        buf_spec = pl.BlockSpec((1, n_hist, tn), lambda gi, fi, ri: (gi, 0, fi))
        buf_rows = n_hist
    else:
        buf_spec = pl.BlockSpec((1, tm, tn), lambda gi, fi, ri: (gi, ri, fi))
        buf_rows = r
    w_spec = pl.BlockSpec((d, tn), lambda gi, fi, ri: (0, fi))
    in_specs = [pl.BlockSpec((1, tm, d), lambda gi, fi, ri: (gi, ri, 0)), w_spec, w_spec,
                pl.BlockSpec((FFN_CONV, tn), lambda gi, fi, ri: (0, fi)),
                pl.BlockSpec((1, tn), lambda gi, fi, ri: (0, fi))]
    args = [h3, w_gate, w_val, conv_w, conv_b.reshape(1, f)]
    if has_hist:
        in_specs.append(pl.BlockSpec((1, tm, tn), lambda gi, fi, ri: (gi, ri, fi)))
        args.append(hist)
    return pl.pallas_call(
        functools.partial(_ffn_up_kernel, tm=tm, grp=grp, row_lo=row_lo, has_hist=has_hist),
        grid=(g, f // tn, r // tm),
        in_specs=in_specs,
        out_specs=[pl.BlockSpec((1, tm, tn), lambda gi, fi, ri: (gi, ri, fi)),
                   buf_spec],
        out_shape=[jax.ShapeDtypeStruct((g, r, f), BF16),
                   jax.ShapeDtypeStruct((g, buf_rows, f), F32)],
        scratch_shapes=[pltpu.VMEM((SUBLANES + tm, tn), F32)],
        compiler_params=_params(3),
        name="ffn_up",
    )(*args)


def _layer(x3, s_hg, s_gd, hist_gd, hist_ffn, mk, mv, weights, *, row_lo, grp, ffn_streams, c_mix, nc_mix,
           nb_mix, fuse_proj, nb_attn):
    (mixer_w, w_out, ln1_g, ln1_b, w_mq, w_mo, ln2_g, ln2_b,
     w_gate, w_val, w_ffn_conv, b_ffn_conv, w_down, ln3_g, ln3_b) = weights
    bsz, t, d = x3.shape
    rows = bsz * t
    x2 = x3.reshape(rows, d)

    if fuse_proj:
        proj = tail = None
    else:
        proj = _matmul(x2, mixer_w[1], 1024, 512).reshape(bsz, t, -1)
        tail = _matmul(x2, mixer_w[2], 1024, HEAD_DIM).reshape(bsz, t, HEAD_DIM)
    mix, new_hg, new_gd, new_buf_gd = _mixer(x3, proj, tail, mixer_w, hist_gd, s_hg, s_gd,
                                             c=c_mix, nc=nc_mix, nb=nb_mix, row_lo=row_lo)

    h1 = _matmul_res_ln([mix.reshape(rows, d)], [w_out], x2, ln1_g, ln1_b, 512)
    q = _matmul(h1, w_mq, 1024, 1024)
    att = _mem_attn_core(q.reshape(bsz, t, d), mk, mv, tq=512, nb=nb_attn)
    h2 = _matmul_res_ln([att.reshape(rows, d)], [w_mo], h1, ln2_g, ln2_b, 512)

    h2_3 = h2.reshape(ffn_streams, rows // ffn_streams, d)
    hist3 = None if hist_ffn is None else hist_ffn.reshape(ffn_streams, rows // ffn_streams, D_FF)
    act, new_buf_ffn = _ffn_up(h2_3, w_gate, w_val, w_ffn_conv, b_ffn_conv, hist3,
                               tm=512, tn=D_FF // 2, grp=grp, row_lo=row_lo)
    y = _matmul_res_ln([act.reshape(rows, D_FF)], [w_down], h2, ln3_g, ln3_b, 512)
    return y.reshape(bsz, t, d), new_hg, new_gd, new_buf_gd, new_buf_ffn


def _cache_rows(cache):
    b, m, h, d = cache.shape
    return cache.reshape(b, m, h, d // LANES, LANES).transpose(0, 1, 3, 2, 4).reshape(b, m * h * d // LANES, LANES)


def kernel(x_prompt, x_sample, state_hgrn, state_gdn, state_gdn_conv, state_ffn_conv, cache_mem_k, cache_mem_v, mem_prompt, hgrn_lb_logits, w_in, w_gd_conv, gd_a_log, gd_dt_bias, hg_norm_g, gd_norm_g, w_out, ln1_g, ln1_b, w_mq, w_mkv, w_mo, ln2_g, ln2_b, w_up, w_ffn_conv, b_ffn_conv, w_down, ln3_g, ln3_b):
    bp, tp, d = x_prompt.shape
    bs, ts, _ = x_sample.shape
    tpad = SUBLANES
    row_lo = tpad - ts
    l = 0

    main_cols = 8 * MIX_HALF
    w_in_tail = jnp.pad(w_in[l][:, main_cols:], ((0, 0), (0, HEAD_DIM - 2 * N_HEADS))).astype(BF16)
    mixer_w = (hgrn_lb_logits, w_in[l][:, :main_cols].astype(BF16), w_in_tail, w_gd_conv[l],
               gd_a_log[l], gd_dt_bias[l], hg_norm_g[l], gd_norm_g[l])
    weights = (mixer_w, w_out[l].astype(BF16), ln1_g[l], ln1_b[l],
               w_mq[l].astype(BF16), w_mo[l].astype(BF16), ln2_g[l], ln2_b[l],
               w_up[l][:, :D_FF].astype(BF16), w_up[l][:, D_FF:].astype(BF16),
               w_ffn_conv[l], b_ffn_conv[l], w_down[l].astype(BF16), ln3_g[l], ln3_b[l])

    mem2 = mem_prompt.reshape(bp * N_MEM, d)
    mk = _matmul(mem2, w_mkv[l][:, :d].astype(BF16), 1024, 1024).reshape(bp, N_MEM, d)
    mv = _matmul(mem2, w_mkv[l][:, d:].astype(BF16), 1024, 1024).reshape(bp, N_MEM, d)
    yp, p_hg, p_gd, p_bgd, p_bff = _layer(x_prompt, None, None, None, None, mk, mv, weights,
                                          row_lo=0, grp=tp, ffn_streams=bp, c_mix=128, nc_mix=2, nb_mix=1,
                                          fuse_proj=True, nb_attn=1)

    n_hg = GD_CONV - 1
    n_hf = FFN_CONV - 1
    xs = jnp.pad(x_sample, ((0, 0), (row_lo, 0), (0, 0)))
    hist_gd = jnp.pad(state_gdn_conv[l], ((0, 0), (row_lo - n_hg, ts), (0, 0)))
    hist_ffn = jnp.pad(state_ffn_conv[l], ((0, 0), (row_lo - n_hf, ts), (0, 0)))
    ys, s_hg, s_gd, s_bgd, s_bff = _layer(
        xs, state_hgrn[l], state_gdn[l], hist_gd, hist_ffn,
        _cache_rows(cache_mem_k[l]), _cache_rows(cache_mem_v[l]), weights,
        row_lo=row_lo, grp=tpad, ffn_streams=1, c_mix=tpad, nc_mix=1, nb_mix=8, fuse_proj=False,
        nb_attn=4)
    ys = ys[:, row_lo:]
    s_bff = s_bff.reshape(bs, tpad, D_FF)[:, tpad - n_hf:]

    mem_shape = (1, bp, N_MEM, MEM_HEADS, MEM_HEAD_DIM)
    return (yp, ys, p_hg[None], p_gd[None], p_bgd[None], p_bff[None],
            mk.reshape(mem_shape), mv.reshape(mem_shape),
            s_hg[None], s_gd[None], s_bgd[None], s_bff[None])
```

```python
import functools

import jax
import jax.numpy as jnp
from jax import lax
from jax.experimental import pallas as pl
from jax.experimental.pallas import tpu as pltpu

F32 = jnp.float32
BF16 = jnp.bfloat16

D_MODEL = 1024
HEAD_DIM = 128
N_HEADS = 4
MIX_HALF = N_HEADS * HEAD_DIM
GD_CONV = 4
FFN_CONV = 3
D_FF = 2816
N_MEM = 256
MEM_HEADS = 4
MEM_HEAD_DIM = D_MODEL // MEM_HEADS
LN_EPS = 1e-5
RMS_EPS = 1e-6
LOG2_E = 1.4426950408889634
DEPTH = 1
ALPHA = (2.0 * DEPTH) ** 0.25

SUBLANES = 8
LANES = 128
VMEM_LIMIT = 56 * 1024 * 1024


def _params(n_axes):
    return pltpu.CompilerParams(dimension_semantics=("arbitrary",) * n_axes,
                                vmem_limit_bytes=VMEM_LIMIT)


def _dot(a, b):
    return jnp.dot(a.astype(BF16), b.astype(BF16), preferred_element_type=F32)


def _dot_nt(a, b):
    return lax.dot_general(a.astype(BF16), b.astype(BF16), (((1,), (1,)), ((), ())),
                           preferred_element_type=F32)


def _dot_tn(a, b):
    return lax.dot_general(a.astype(BF16), b.astype(BF16), (((0,), (0,)), ((), ())),
                           preferred_element_type=F32)


def _split3(x):
    x1 = x.astype(BF16)
    r = x - x1.astype(F32)
    x2 = r.astype(BF16)
    x3 = (r - x2.astype(F32)).astype(BF16)
    return x1, x2, x3


def _cumsum_rows(tri_bf16, x):
    x1, x2, x3 = _split3(x)
    return (jnp.dot(tri_bf16, x1, preferred_element_type=F32)
            + jnp.dot(tri_bf16, x2, preferred_element_type=F32)
            + jnp.dot(tri_bf16, x3, preferred_element_type=F32))


def _multi_dot(a, b, keys, nt=False):
    dot = _dot_nt if nt else _dot
    return {k: dot(a[k], b[k]) for k in keys}


def _silu(x):
    return x * jax.nn.sigmoid(x)


def _mm_kernel(x_ref, w_ref, o_ref, xb_ref):
    @pl.when(pl.program_id(1) == 0)
    def _():
        xb_ref[...] = x_ref[...].astype(BF16)

    o_ref[...] = jnp.dot(xb_ref[...], w_ref[...], preferred_element_type=F32).astype(o_ref.dtype)


def _matmul(x, w, tm, tn):
    m, k = x.shape
    n = w.shape[1]
    tm = min(tm, m)
    tn = min(tn, n)
    return pl.pallas_call(
        _mm_kernel,
        grid=(m // tm, n // tn),
        in_specs=[pl.BlockSpec((tm, k), lambda i, j: (i, 0)),
                  pl.BlockSpec((k, tn), lambda i, j: (0, j))],
        out_specs=pl.BlockSpec((tm, tn), lambda i, j: (i, j)),
        out_shape=jax.ShapeDtypeStruct((m, n), F32),
        scratch_shapes=[pltpu.VMEM((tm, k), BF16)],
        compiler_params=_params(2),
        name="proj_matmul",
    )(x, w)


def _mm_ln_kernel(*refs, n_in):
    a_refs = refs[:n_in]
    w_refs = refs[n_in:2 * n_in]
    res_ref, g_ref, b_ref, o_ref = refs[2 * n_in:]
    acc = ALPHA * res_ref[...]
    for a_ref, w_ref in zip(a_refs, w_refs):
        acc = acc + jnp.dot(a_ref[...].astype(BF16), w_ref[...], preferred_element_type=F32)
    mu = jnp.mean(acc, axis=-1, keepdims=True)
    xc = acc - mu
    var = jnp.mean(xc * xc, axis=-1, keepdims=True)
    o_ref[...] = xc * lax.rsqrt(var + LN_EPS) * g_ref[...] + b_ref[...]


def _matmul_res_ln(a_list, w_list, res, g, b, tm):
    m, d = res.shape
    tm = min(tm, m)
    n_in = len(a_list)
    in_specs = ([pl.BlockSpec((tm, a.shape[1]), lambda i: (i, 0)) for a in a_list]
                + [pl.BlockSpec(w.shape, lambda i: (0, 0)) for w in w_list]
                + [pl.BlockSpec((tm, d), lambda i: (i, 0)),
                   pl.BlockSpec((1, d), lambda i: (0, 0)),
                   pl.BlockSpec((1, d), lambda i: (0, 0))])
    return pl.pallas_call(
        functools.partial(_mm_ln_kernel, n_in=n_in),
        grid=(m // tm,),
        in_specs=in_specs,
        out_specs=pl.BlockSpec((tm, d), lambda i: (i, 0)),
        out_shape=jax.ShapeDtypeStruct((m, d), F32),
        compiler_params=_params(1),
        name="proj_res_ln",
    )(*a_list, *w_list, res, g.reshape(1, d), b.reshape(1, d))


HG_COL = 0
GD_COL = 4 * MIX_HALF


def _ref_rows(gc_ref, b, col, m, c):
    blk = 2 * m
    if blk >= SUBLANES:
        parts = [jnp.broadcast_to(gc_ref[b, pl.ds(j * blk + m - 1, 1), col], (blk, HEAD_DIM))
                 for j in range(c // blk)]
    else:
        sub = lax.broadcasted_iota(jnp.int32, (SUBLANES, HEAD_DIM), 0)
        parts = []
        for i in range(c // SUBLANES):
            tile = None
            for j in range(SUBLANES // blk):
                row = jnp.broadcast_to(gc_ref[b, pl.ds(i * SUBLANES + j * blk + m - 1, 1), col],
                                       (SUBLANES, HEAD_DIM))
                tile = row if tile is None else jnp.where(sub >= j * blk, row, tile)
            parts.append(tile)
    return parts[0] if len(parts) == 1 else jnp.concatenate(parts, axis=0)


def _hgrn_chunk(p_ref, mix_ref, lb, ng, st_ref, gc_ref, *, r0, c, nb, row_lo):
    rows = lax.broadcasted_iota(jnp.int32, (c, 1), 0)
    ti = lax.broadcasted_iota(jnp.int32, (c, c), 0)
    si = lax.broadcasted_iota(jnp.int32, (c, c), 1)
    tri = (si <= ti).astype(F32).astype(BF16)
    eye = ti == si
    xr = ti ^ si
    pairs = [(b, h) for b in range(nb) for h in range(N_HEADS)]
    cols = [slice(h * HEAD_DIM, (h + 1) * HEAD_DIM) for h in range(N_HEADS)]
    rs = pl.ds(r0, c)

    def proj(b, j):
        return p_ref[b, rs, HG_COL + j * MIX_HALF:HG_COL + (j + 1) * MIX_HALF]

    q, k, gc = {}, {}, {}
    for b in range(nb):
        f = lb + (1.0 - lb) * jax.nn.sigmoid(proj(b, 1))
        g = jnp.log(f)
        kb = 1.0 - f
        if row_lo:
            g = jnp.where(rows >= row_lo, g, 0.0)
            kb = jnp.where(rows >= row_lo, kb, 0.0)
        qb = _silu(proj(b, 0))
        gcb = _cumsum_rows(tri, g)
        gc_ref[b] = gcb
        for h in range(N_HEADS):
            q[b, h], k[b, h], gc[b, h] = qb[:, cols[h]], kb[:, cols[h]], gcb[:, cols[h]]

    level_of = jnp.where(eye, 0, -1)
    m = 1
    while m < c:
        level_of = jnp.where((xr >= m) & (xr < 2 * m) & (si < ti), m, level_of)
        m *= 2
    rows_w = lax.broadcasted_iota(jnp.int32, (c, HEAD_DIM), 0)
    sc = _multi_dot(q, k, pairs, nt=True)
    sc = {bh: jnp.where(level_of == 0, sc[bh], 0.0) for bh in pairs}
    m = c // 2
    while m >= 1:
        sign = jnp.where((rows_w & m) != 0, LOG2_E, -LOG2_E)
        qw, kw = {}, {}
        for bh in pairs:
            w = jnp.exp2((gc[bh] - _ref_rows(gc_ref, bh[0], cols[bh[1]], m, c)) * sign)
            qw[bh], kw[bh] = q[bh] * w, k[bh] * w
        prod = _multi_dot(qw, kw, pairs, nt=True)
        sc = {bh: jnp.where(level_of == m, prod[bh], sc[bh]) for bh in pairs}
        m //= 2

    v = {(b, h): p_ref[b, rs, HG_COL + 2 * MIX_HALF + h * HEAD_DIM:HG_COL + 2 * MIX_HALF + (h + 1) * HEAD_DIM]
         for b, h in pairs}
    st = {bh: st_ref[bh] for bh in pairs}
    o_intra = _multi_dot(sc, v, pairs)
    o_inter = _multi_dot({bh: q[bh] * jnp.exp(gc[bh]) for bh in pairs}, st, pairs, nt=True)
    o = {bh: o_intra[bh] + o_inter[bh] for bh in pairs}
    for b, h in pairs:
        g_last = gc_ref[b, pl.ds(c - 1, 1), cols[h]]
        st_ref[b, h] = st[b, h] * jnp.exp(g_last) + _dot_tn(v[b, h], k[b, h] * jnp.exp(g_last - gc[b, h]))
    for b, h in pairs:
        ob = o[b, h]
        on = ob * lax.rsqrt(jnp.mean(ob * ob, axis=-1, keepdims=True) + RMS_EPS) * ng
        gate = p_ref[b, rs, HG_COL + 3 * MIX_HALF + h * HEAD_DIM:HG_COL + 3 * MIX_HALF + (h + 1) * HEAD_DIM]
        mix_ref[b, rs, cols[h]] = on * _silu(gate)


def _gdn_rows(p_ref, t_ref, hist_ref, mix_ref, cb_ref, cw_ref, neg_a, dt_bias, ng, s_ref, xp_ref,
              *, c, nc, nb, row_lo):
    n_hist = GD_CONV - 1
    rb = c * nc
    rows = lax.broadcasted_iota(jnp.int32, (rb, 1), 0) % c
    ti = lax.broadcasted_iota(jnp.int32, (c, c), 0)
    si = lax.broadcasted_iota(jnp.int32, (c, c), 1)
    tri = (si <= ti).astype(F32).astype(BF16)
    eye_f = (ti == si).astype(F32)
    incl = si <= ti

    chains = [(b, j, h) for b in range(nb) for j in range(nc) for h in range(N_HEADS)]
    q_all, k_all, v_all, beta_all, la_all = {}, {}, {}, {}, {}
    for b in range(nb):
        qkv = []
        for j in range(3):
            col = slice(j * MIX_HALF, (j + 1) * MIX_HALF)
            x = p_ref[b, :, GD_COL + j * MIX_HALF:GD_COL + (j + 1) * MIX_HALF]
            if hist_ref is not None:
                x = jnp.where((rows >= row_lo - n_hist) & (rows < row_lo), hist_ref[b, :, col], x)
            xp_ref[b, SUBLANES:SUBLANES + rb, col] = x
            y = x * cw_ref[n_hist:n_hist + 1, col]
            for tap in range(n_hist):
                off = SUBLANES - n_hist + tap
                y = y + xp_ref[b, off:off + rb, col] * cw_ref[tap:tap + 1, col]
            qkv.append(_silu(y))
        cb_ref[b] = xp_ref[b, SUBLANES + rb - n_hist:SUBLANES + rb, :]
        xp_ref[b, 0:SUBLANES, :] = xp_ref[b, rb:rb + SUBLANES, :]
        q_all[b], k_all[b], v_all[b] = qkv

        tail = t_ref[b]
        beta_all[b] = jax.nn.sigmoid(tail)
        la_all[b] = neg_a * jax.nn.softplus(tail + dt_bias)
        if row_lo:
            beta_all[b] = jnp.where(rows >= row_lo, beta_all[b], 0.0)
            la_all[b] = jnp.where(rows >= row_lo, la_all[b], 0.0)

    gc = {(b, j): _cumsum_rows(tri, la_all[b][j * c:(j + 1) * c]) for b in range(nb) for j in range(nc)}

    pre = {}
    for ch in chains:
        b, j, h = ch
        rs = slice(j * c, (j + 1) * c)
        col = slice(h * HEAD_DIM, (h + 1) * HEAD_DIM)
        qh, kh, vh = q_all[b][rs, col], k_all[b][rs, col], v_all[b][rs, col]
        qh = qh * lax.rsqrt(jnp.sum(qh * qh, axis=-1, keepdims=True) + RMS_EPS) * (HEAD_DIM ** -0.5)
        kh = kh * lax.rsqrt(jnp.sum(kh * kh, axis=-1, keepdims=True) + RMS_EPS)
        beta = beta_all[b][rs, h:h + 1]
        gcol = gc[b, j][:, N_HEADS + h:N_HEADS + h + 1]
        grow = jnp.sum(eye_f * gcol, axis=0, keepdims=True)
        decay = jnp.where(incl, jnp.exp(jnp.where(incl, gcol - grow, 0.0)), 0.0)
        gamma = jnp.exp(gcol)
        g_last = gcol[c - 1:c, :]
        pre[ch] = dict(
            q=qh, k=kh, beta=beta, decay=decay,
            rhs=jnp.concatenate([(beta * gamma) * kh, beta * vh], axis=1).astype(BF16),
            gq=(gamma * qh).astype(BF16),
            kd=(kh * jnp.exp(g_last - gcol)).astype(BF16),
            eg=jnp.exp(g_last))
    k_b = {ch: pre[ch]["k"].astype(BF16) for ch in chains}
    kk = _multi_dot(k_b, k_b, chains, nt=True)
    qk = _multi_dot({ch: pre[ch]["q"] for ch in chains}, k_b, chains, nt=True)
    a_mat = {ch: pre[ch]["beta"] * kk[ch] * pre[ch]["decay"] for ch in chains}
    aqk = {ch: (qk[ch] * pre[ch]["decay"]).astype(BF16) for ch in chains}

    xr = ti ^ si
    first = (xr < 2) & (si < ti)
    t_inv = {ch: eye_f - jnp.where(first, a_mat[ch], 0.0) for ch in chains}
    n = 2
    while n < c:
        lower = (xr < 2 * n) & ((ti & n) != 0) & ((si & n) == 0)
        x = _multi_dot({ch: jnp.where(lower, a_mat[ch], 0.0) for ch in chains}, t_inv, chains)
        tx = _multi_dot(t_inv, x, chains)
        t_inv = {ch: t_inv[ch] - tx[ch] for ch in chains}
        n *= 2
    wu = {ch: _dot(t_inv[ch], pre[ch]["rhs"]) for ch in chains}

    for j in range(nc):
        rs = slice(j * c, (j + 1) * c)
        group = [(b, j, h) for b in range(nb) for h in range(N_HEADS)]
        s_old = {ch: s_ref[ch[0], ch[2]] for ch in group}
        ws = _multi_dot({ch: jnp.concatenate([wu[ch][:, :HEAD_DIM].astype(BF16), pre[ch]["gq"]], axis=0)
                         for ch in group}, s_old, group)
        u = {ch: (wu[ch][:, HEAD_DIM:] - ws[ch][:c]).astype(BF16) for ch in group}
        au = _multi_dot(aqk, u, group)
        for ch in group:
            b, _, h = ch
            s_ref[b, h] = pre[ch]["eg"] * s_old[ch] + _dot_tn(pre[ch]["kd"], u[ch])
            o = ws[ch][c:] + au[ch]
            on = o * lax.rsqrt(jnp.mean(o * o, axis=-1, keepdims=True) + RMS_EPS) * ng
            gate = p_ref[b, rs, GD_COL + 3 * MIX_HALF + h * HEAD_DIM:GD_COL + 3 * MIX_HALF + (h + 1) * HEAD_DIM]
            mix_ref[b, rs, MIX_HALF + h * HEAD_DIM:MIX_HALF + (h + 1) * HEAD_DIM] = on * _silu(gate)


def _mixer_kernel(*refs, c, nc, nb, fuse_proj, has_state, row_lo, n_steps):
    refs = list(refs)
    if fuse_proj:
        x_ref, w_ref, wt_ref = refs[:3]
        del refs[:3]
    else:
        p_ref, t_ref = refs[:2]
        del refs[:2]
    lbl_ref, hng_ref, cw_ref, par_ref, gng_ref = refs[:5]
    del refs[:5]
    if has_state:
        hist_ref, shg0_ref, sgd0_ref = refs[:3]
        del refs[:3]
    else:
        hist_ref = shg0_ref = sgd0_ref = None
    mix_ref, shg_ref, sgd_ref, cb_ref, st_ref, s_ref, gc_ref, xp_ref = refs[:8]
    if fuse_proj:
        p_ref, t_ref = refs[8:]
    step = pl.program_id(1)
    pairs = [(b, h) for b in range(nb) for h in range(N_HEADS)]

    @pl.when(step == 0)
    def _():
        for b, h in pairs:
            if has_state:
                st_ref[b, h] = shg0_ref[b, h].T
                s_ref[b, h] = sgd0_ref[b, h]
            else:
                st_ref[b, h] = jnp.zeros((HEAD_DIM, HEAD_DIM), F32)
                s_ref[b, h] = jnp.zeros((HEAD_DIM, HEAD_DIM), F32)
        for b in range(nb):
            xp_ref[b, 0:SUBLANES, :] = jnp.zeros((SUBLANES, 3 * MIX_HALF), F32)

    if fuse_proj:
        for b in range(nb):
            xb = x_ref[b].astype(BF16)
            p_ref[b] = jnp.dot(xb, w_ref[...], preferred_element_type=F32)
            t_ref[b] = jnp.dot(xb, wt_ref[...], preferred_element_type=F32)

    lbl = lbl_ref[...]
    e = jnp.exp(lbl - jnp.max(lbl, axis=0, keepdims=True))
    lb = e[0:1] / jnp.sum(e, axis=0, keepdims=True)
    neg_a = -jnp.exp(par_ref[0:1, :])
    dt_bias = par_ref[1:2, :]

    for j in range(nc):
        _hgrn_chunk(p_ref, mix_ref, lb, hng_ref[...], st_ref, gc_ref, r0=j * c, c=c, nb=nb, row_lo=row_lo)
    _gdn_rows(p_ref, t_ref, hist_ref, mix_ref, cb_ref, cw_ref, neg_a, dt_bias, gng_ref[...], s_ref, xp_ref,
              c=c, nc=nc, nb=nb, row_lo=row_lo)

    @pl.when(step == n_steps - 1)
    def _():
        for b, h in pairs:
            shg_ref[b, h] = st_ref[b, h].T
            sgd_ref[b, h] = s_ref[b, h]


def _mixer(x3, proj, tail, weights, hist, s_hg, s_gd, *, c, nc, nb, row_lo):
    lb_logits, w_main, w_tail, conv_w, a_log, dt_bias, hg_norm_g, gd_norm_g = weights
    fuse_proj = proj is None
    bsz, t = (x3 if fuse_proj else proj).shape[:2]
    rb = c * nc
    n_steps = t // rb
    has_state = s_hg is not None
    n_hist = GD_CONV - 1
    n_proj = w_main.shape[1]
    const = lambda shape: pl.BlockSpec(shape, lambda i, si: (0,) * len(shape))
    rows3 = lambda width: pl.BlockSpec((nb, rb, width), lambda i, si: (i, si, 0))
    st_spec = pl.BlockSpec((nb, N_HEADS, HEAD_DIM, HEAD_DIM), lambda i, si: (i, 0, 0, 0))
    par = jnp.zeros((SUBLANES, HEAD_DIM), F32)
    par = par.at[0, N_HEADS:2 * N_HEADS].set(a_log).at[1, N_HEADS:2 * N_HEADS].set(dt_bias)
    if fuse_proj:
        in_specs = [rows3(D_MODEL), const(w_main.shape), const(w_tail.shape)]
        args = [x3, w_main, w_tail]
    else:
        in_specs = [rows3(n_proj), rows3(HEAD_DIM)]
        args = [proj, tail]
    in_specs += [const(lb_logits.shape), const((1, HEAD_DIM)), const(conv_w.shape), const(par.shape),
                 const((1, HEAD_DIM))]
    args += [lb_logits, hg_norm_g.reshape(1, HEAD_DIM), conv_w, par, gd_norm_g.reshape(1, HEAD_DIM)]
    if has_state:
        in_specs += [rows3(3 * MIX_HALF), st_spec, st_spec]
        args += [hist, s_hg, s_gd]
    scratch = [pltpu.VMEM((nb, N_HEADS, HEAD_DIM, HEAD_DIM), F32),
               pltpu.VMEM((nb, N_HEADS, HEAD_DIM, HEAD_DIM), F32),
               pltpu.VMEM((nb, c, MIX_HALF), F32),
               pltpu.VMEM((nb, SUBLANES + rb, 3 * MIX_HALF), F32)]
    if fuse_proj:
        scratch += [pltpu.VMEM((nb, rb, n_proj), F32), pltpu.VMEM((nb, rb, HEAD_DIM), F32)]
    return pl.pallas_call(
        functools.partial(_mixer_kernel, c=c, nc=nc, nb=nb, fuse_proj=fuse_proj, has_state=has_state,
                          row_lo=row_lo, n_steps=n_steps),
        grid=(bsz // nb, n_steps),
        in_specs=in_specs,
        out_specs=[rows3(D_MODEL), st_spec, st_spec,
                   pl.BlockSpec((nb, n_hist, 3 * MIX_HALF), lambda i, si: (i, 0, 0))],
        out_shape=[jax.ShapeDtypeStruct((bsz, t, D_MODEL), F32),
                   jax.ShapeDtypeStruct((bsz, N_HEADS, HEAD_DIM, HEAD_DIM), F32),
                   jax.ShapeDtypeStruct((bsz, N_HEADS, HEAD_DIM, HEAD_DIM), F32),
                   jax.ShapeDtypeStruct((bsz, n_hist, 3 * MIX_HALF), F32)],
        scratch_shapes=scratch,
        compiler_params=_params(2),
        name="mixer",
    )(*args)


def _attn_kernel(q_ref, k_ref, v_ref, o_ref, *, nb):
    scale = MEM_HEAD_DIM ** -0.5
    pairs = [(b, h) for b in range(nb) for h in range(MEM_HEADS)]

    def head_block(ref, b, h):
        if ref.shape[-1] == LANES:
            rows_per_mem = MEM_HEADS * MEM_HEAD_DIM // LANES
            return jnp.concatenate([ref[b, pl.ds(h + MEM_HEADS * j, N_MEM, stride=rows_per_mem), :]
                                    for j in range(MEM_HEAD_DIM // LANES)], axis=1)
        return ref[b, :, h * MEM_HEAD_DIM:(h + 1) * MEM_HEAD_DIM]

    s = {bh: _dot_nt(q_ref[bh[0], :, bh[1] * MEM_HEAD_DIM:(bh[1] + 1) * MEM_HEAD_DIM],
                     head_block(k_ref, *bh)) * scale for bh in pairs}
    p = {}
    for bh in pairs:
        e = jnp.exp(s[bh] - jnp.max(s[bh], axis=-1, keepdims=True))
        p[bh] = e / jnp.sum(e, axis=-1, keepdims=True)
    for b, h in pairs:
        o_ref[b, :, h * MEM_HEAD_DIM:(h + 1) * MEM_HEAD_DIM] = _dot(p[b, h], head_block(v_ref, b, h))


def _mem_attn_core(q3, mk, mv, *, tq, nb):
    bsz, t, d = q3.shape
    tq = min(tq, t)
    kv_spec = pl.BlockSpec((nb,) + mk.shape[1:], lambda i, r: (i, 0, 0))
    return pl.pallas_call(
        functools.partial(_attn_kernel, nb=nb),
        grid=(bsz // nb, t // tq),
        in_specs=[pl.BlockSpec((nb, tq, d), lambda i, r: (i, r, 0)), kv_spec, kv_spec],
        out_specs=pl.BlockSpec((nb, tq, d), lambda i, r: (i, r, 0)),
        out_shape=jax.ShapeDtypeStruct((bsz, t, d), F32),
        compiler_params=_params(2),
        name="mem_attn_core",
    )(q3, mk, mv)


def _ffn_up_kernel(*refs, tm, grp, row_lo, has_hist):
    if has_hist:
        h_ref, wg_ref, wv_ref, cw_ref, cb_ref, hist_ref, act_ref, buf_ref, xp_ref = refs
    else:
        h_ref, wg_ref, wv_ref, cw_ref, cb_ref, act_ref, buf_ref, xp_ref = refs
        hist_ref = None
    n_hist = FFN_CONV - 1
    r = pl.program_id(2)

    @pl.when(r == 0)
    def _():
        xp_ref[0:SUBLANES, :] = jnp.zeros((SUBLANES, xp_ref.shape[1]), F32)

    hb = h_ref[0].astype(BF16)
    gate = jnp.dot(hb, wg_ref[...], preferred_element_type=F32)
    val = jnp.dot(hb, wv_ref[...], preferred_element_type=F32)
    if hist_ref is not None:
        pos = lax.broadcasted_iota(jnp.int32, (tm, 1), 0) % grp
        gate = jnp.where((pos >= row_lo - n_hist) & (pos < row_lo), hist_ref[0], gate)
    xp_ref[SUBLANES:SUBLANES + tm, :] = gate
    y = gate * cw_ref[n_hist:n_hist + 1, :] + cb_ref[...]
    for tap in range(n_hist):
        off = SUBLANES - n_hist + tap
        y = y + xp_ref[off:off + tm, :] * cw_ref[tap:tap + 1, :]
    gelu = 0.5 * y * (1.0 + lax.erf(y * (2.0 ** -0.5)))
    act_ref[0] = (gelu * val).astype(act_ref.dtype)
    if buf_ref.shape[1] == n_hist:
        buf_ref[0] = xp_ref[SUBLANES + tm - n_hist:SUBLANES + tm, :]
    else:
        buf_ref[0] = gate
    xp_ref[0:SUBLANES, :] = xp_ref[tm:tm + SUBLANES, :]


def _ffn_up(h3, w_gate, w_val, conv_w, conv_b, hist, *, tm, tn, grp, row_lo):
    g, r, d = h3.shape
    f = w_gate.shape[1]
    tm = min(tm, r)
    n_hist = FFN_CONV - 1
    has_hist = hist is not None
    assert grp == r or grp <= tm
    if grp == r:
        buf_spec = pl.BlockSpec((1, n_hist, tn), lambda gi, fi, ri: (gi, 0, fi))
        buf_rows = n_hist
    else:
        buf_spec = pl.BlockSpec((1, tm, tn), lambda gi, fi, ri: (gi, ri, fi))
        buf_rows = r
    w_spec = pl.BlockSpec((d, tn), lambda gi, fi, ri: (0, fi))
    in_specs = [pl.BlockSpec((1, tm, d), lambda gi, fi, ri: (gi, ri, 0)), w_spec, w_spec,
                pl.BlockSpec((FFN_CONV, tn), lambda gi, fi, ri: (0, fi)),
                pl.BlockSpec((1, tn), lambda gi, fi, ri: (0, fi))]
    args = [h3, w_gate, w_val, conv_w, conv_b.reshape(1, f)]
    if has_hist:
        in_specs.append(pl.BlockSpec((1, tm, tn), lambda gi, fi, ri: (gi, ri, fi)))
        args.append(hist)
    return pl.pallas_call(
        functools.partial(_ffn_up_kernel, tm=tm, grp=grp, row_lo=row_lo, has_hist=has_hist),
        grid=(g, f // tn, r // tm),
        in_specs=in_specs,
        out_specs=[pl.BlockSpec((1, tm, tn), lambda gi, fi, ri: (gi, ri, fi)),
                   buf_spec],
        out_shape=[jax.ShapeDtypeStruct((g, r, f), BF16),
                   jax.ShapeDtypeStruct((g, buf_rows, f), F32)],
        scratch_shapes=[pltpu.VMEM((SUBLANES + tm, tn), F32)],
        compiler_params=_params(3),
        name="ffn_up",
    )(*args)


def _layer(x3, s_hg, s_gd, hist_gd, hist_ffn, mk, mv, weights, *, row_lo, grp, ffn_streams, c_mix, nc_mix,
           nb_mix, fuse_proj, nb_attn):
    (mixer_w, w_out, ln1_g, ln1_b, w_mq, w_mo, ln2_g, ln2_b,
     w_gate, w_val, w_ffn_conv, b_ffn_conv, w_down, ln3_g, ln3_b) = weights
    bsz, t, d = x3.shape
    rows = bsz * t
    x2 = x3.reshape(rows, d)

    if fuse_proj:
        proj = tail = None
    else:
        proj = _matmul(x2, mixer_w[1], 1024, 512).reshape(bsz, t, -1)
        tail = _matmul(x2, mixer_w[2], 1024, HEAD_DIM).reshape(bsz, t, HEAD_DIM)
    mix, new_hg, new_gd, new_buf_gd = _mixer(x3, proj, tail, mixer_w, hist_gd, s_hg, s_gd,
                                             c=c_mix, nc=nc_mix, nb=nb_mix, row_lo=row_lo)

    h1 = _matmul_res_ln([mix.reshape(rows, d)], [w_out], x2, ln1_g, ln1_b, 512)
    q = _matmul(h1, w_mq, 1024, 1024)
    att = _mem_attn_core(q.reshape(bsz, t, d), mk, mv, tq=512, nb=nb_attn)
    h2 = _matmul_res_ln([att.reshape(rows, d)], [w_mo], h1, ln2_g, ln2_b, 512)

    h2_3 = h2.reshape(ffn_streams, rows // ffn_streams, d)
    hist3 = None if hist_ffn is None else hist_ffn.reshape(ffn_streams, rows // ffn_streams, D_FF)
    act, new_buf_ffn = _ffn_up(h2_3, w_gate, w_val, w_ffn_conv, b_ffn_conv, hist3,
                               tm=512, tn=D_FF // 2, grp=grp, row_lo=row_lo)
    y = _matmul_res_ln([act.reshape(rows, D_FF)], [w_down], h2, ln3_g, ln3_b, 512)
    return y.reshape(bsz, t, d), new_hg, new_gd, new_buf_gd, new_buf_ffn


def _cache_rows(cache):
    b, m, h, d = cache.shape
    return cache.reshape(b, m, h, d // LANES, LANES).transpose(0, 1, 3, 2, 4).reshape(b, m * h * d // LANES, LANES)


def kernel(x_prompt, x_sample, state_hgrn, state_gdn, state_gdn_conv, state_ffn_conv, cache_mem_k, cache_mem_v, mem_prompt, hgrn_lb_logits, w_in, w_gd_conv, gd_a_log, gd_dt_bias, hg_norm_g, gd_norm_g, w_out, ln1_g, ln1_b, w_mq, w_mkv, w_mo, ln2_g, ln2_b, w_up, w_ffn_conv, b_ffn_conv, w_down, ln3_g, ln3_b):
    bp, tp, d = x_prompt.shape
    bs, ts, _ = x_sample.shape
    tpad = SUBLANES
    row_lo = tpad - ts
    l = 0

    main_cols = 8 * MIX_HALF
    w_in_tail = jnp.pad(w_in[l][:, main_cols:], ((0, 0), (0, HEAD_DIM - 2 * N_HEADS))).astype(BF16)
    mixer_w = (hgrn_lb_logits, w_in[l][:, :main_cols].astype(BF16), w_in_tail, w_gd_conv[l],
               gd_a_log[l], gd_dt_bias[l], hg_norm_g[l], gd_norm_g[l])
    weights = (mixer_w, w_out[l].astype(BF16), ln1_g[l], ln1_b[l],
               w_mq[l].astype(BF16), w_mo[l].astype(BF16), ln2_g[l], ln2_b[l],
               w_up[l][:, :D_FF].astype(BF16), w_up[l][:, D_FF:].astype(BF16),
               w_ffn_conv[l], b_ffn_conv[l], w_down[l].astype(BF16), ln3_g[l], ln3_b[l])

    mem2 = mem_prompt.reshape(bp * N_MEM, d)
    mk = _matmul(mem2, w_mkv[l][:, :d].astype(BF16), 1024, 1024).reshape(bp, N_MEM, d)
    mv = _matmul(mem2, w_mkv[l][:, d:].astype(BF16), 1024, 1024).reshape(bp, N_MEM, d)
    yp, p_hg, p_gd, p_bgd, p_bff = _layer(x_prompt, None, None, None, None, mk, mv, weights,
                                          row_lo=0, grp=tp, ffn_streams=bp, c_mix=128, nc_mix=4, nb_mix=1,
                                          fuse_proj=True, nb_attn=1)

    n_hg = GD_CONV - 1
    n_hf = FFN_CONV - 1
    xs = jnp.pad(x_sample, ((0, 0), (row_lo, 0), (0, 0)))
    hist_gd = jnp.pad(state_gdn_conv[l], ((0, 0), (row_lo - n_hg, ts), (0, 0)))
    hist_ffn = jnp.pad(state_ffn_conv[l], ((0, 0), (row_lo - n_hf, ts), (0, 0)))
    ys, s_hg, s_gd, s_bgd, s_bff = _layer(
        xs, state_hgrn[l], state_gdn[l], hist_gd, hist_ffn,
        _cache_rows(cache_mem_k[l]), _cache_rows(cache_mem_v[l]), weights,
        row_lo=row_lo, grp=tpad, ffn_streams=1, c_mix=tpad, nc_mix=1, nb_mix=8, fuse_proj=False,
        nb_attn=4)
    ys = ys[:, row_lo:]
    s_bff = s_bff.reshape(bs, tpad, D_FF)[:, tpad - n_hf:]

    mem_shape = (1, bp, N_MEM, MEM_HEADS, MEM_HEAD_DIM)
    return (yp, ys, p_hg[None], p_gd[None], p_bgd[None], p_bff[None],
            mk.reshape(mem_shape), mv.reshape(mem_shape),
            s_hg[None], s_gd[None], s_bgd[None], s_bff[None])
```

```python
import functools

import jax
import jax.numpy as jnp
from jax import lax
from jax.experimental import pallas as pl
from jax.experimental.pallas import tpu as pltpu

F32 = jnp.float32
BF16 = jnp.bfloat16

D_MODEL = 1024
HEAD_DIM = 128
N_HEADS = 4
MIX_HALF = N_HEADS * HEAD_DIM
GD_CONV = 4
FFN_CONV = 3
D_FF = 2816
N_MEM = 256
MEM_HEADS = 4
MEM_HEAD_DIM = D_MODEL // MEM_HEADS
LN_EPS = 1e-5
RMS_EPS = 1e-6
LOG2_E = 1.4426950408889634
DEPTH = 1
ALPHA = (2.0 * DEPTH) ** 0.25

SUBLANES = 8
LANES = 128
VMEM_LIMIT = 56 * 1024 * 1024


def _params(n_axes):
    return pltpu.CompilerParams(dimension_semantics=("arbitrary",) * n_axes,
                                vmem_limit_bytes=VMEM_LIMIT)


def _dot(a, b):
    return jnp.dot(a.astype(BF16), b.astype(BF16), preferred_element_type=F32)


def _dot_nt(a, b):
    return lax.dot_general(a.astype(BF16), b.astype(BF16), (((1,), (1,)), ((), ())),
                           preferred_element_type=F32)


def _dot_tn(a, b):
    return lax.dot_general(a.astype(BF16), b.astype(BF16), (((0,), (0,)), ((), ())),
                           preferred_element_type=F32)


def _split3(x):
    x1 = x.astype(BF16)
    r = x - x1.astype(F32)
    x2 = r.astype(BF16)
    x3 = (r - x2.astype(F32)).astype(BF16)
    return x1, x2, x3


def _cumsum_rows(tri_bf16, x):
    x1, x2, x3 = _split3(x)
    return (jnp.dot(tri_bf16, x1, preferred_element_type=F32)
            + jnp.dot(tri_bf16, x2, preferred_element_type=F32)
            + jnp.dot(tri_bf16, x3, preferred_element_type=F32))


def _multi_dot(a, b, keys, nt=False):
    dot = _dot_nt if nt else _dot
    return {k: dot(a[k], b[k]) for k in keys}


def _silu(x):
    return x * jax.nn.sigmoid(x)


def _mm_kernel(x_ref, w_ref, o_ref, xb_ref):
    @pl.when(pl.program_id(1) == 0)
    def _():
        xb_ref[...] = x_ref[...].astype(BF16)

    o_ref[...] = jnp.dot(xb_ref[...], w_ref[...], preferred_element_type=F32).astype(o_ref.dtype)


def _matmul(x, w, tm, tn):
    m, k = x.shape
    n = w.shape[1]
    tm = min(tm, m)
    tn = min(tn, n)
    return pl.pallas_call(
        _mm_kernel,
        grid=(m // tm, n // tn),
        in_specs=[pl.BlockSpec((tm, k), lambda i, j: (i, 0)),
                  pl.BlockSpec((k, tn), lambda i, j: (0, j))],
        out_specs=pl.BlockSpec((tm, tn), lambda i, j: (i, j)),
        out_shape=jax.ShapeDtypeStruct((m, n), F32),
        scratch_shapes=[pltpu.VMEM((tm, k), BF16)],
        compiler_params=_params(2),
        name="proj_matmul",
    )(x, w)


def _mm_ln_kernel(*refs, n_in):
    a_refs = refs[:n_in]
    w_refs = refs[n_in:2 * n_in]
    res_ref, g_ref, b_ref, o_ref = refs[2 * n_in:]
    acc = ALPHA * res_ref[...]
    for a_ref, w_ref in zip(a_refs, w_refs):
        acc = acc + jnp.dot(a_ref[...].astype(BF16), w_ref[...], preferred_element_type=F32)
    mu = jnp.mean(acc, axis=-1, keepdims=True)
    xc = acc - mu
    var = jnp.mean(xc * xc, axis=-1, keepdims=True)
    o_ref[...] = xc * lax.rsqrt(var + LN_EPS) * g_ref[...] + b_ref[...]


def _matmul_res_ln(a_list, w_list, res, g, b, tm):
    m, d = res.shape
    tm = min(tm, m)
    n_in = len(a_list)
    in_specs = ([pl.BlockSpec((tm, a.shape[1]), lambda i: (i, 0)) for a in a_list]
                + [pl.BlockSpec(w.shape, lambda i: (0, 0)) for w in w_list]
                + [pl.BlockSpec((tm, d), lambda i: (i, 0)),
                   pl.BlockSpec((1, d), lambda i: (0, 0)),
                   pl.BlockSpec((1, d), lambda i: (0, 0))])
    return pl.pallas_call(
        functools.partial(_mm_ln_kernel, n_in=n_in),
        grid=(m // tm,),
        in_specs=in_specs,
        out_specs=pl.BlockSpec((tm, d), lambda i: (i, 0)),
        out_shape=jax.ShapeDtypeStruct((m, d), F32),
        compiler_params=_params(1),
        name="proj_res_ln",
    )(*a_list, *w_list, res, g.reshape(1, d), b.reshape(1, d))


HG_COL = 0
GD_COL = 4 * MIX_HALF


def _ref_rows(gc_ref, b, col, m, c):
    blk = 2 * m
    if blk >= SUBLANES:
        parts = [jnp.broadcast_to(gc_ref[b, pl.ds(j * blk + m - 1, 1), col], (blk, HEAD_DIM))
                 for j in range(c // blk)]
    else:
        sub = lax.broadcasted_iota(jnp.int32, (SUBLANES, HEAD_DIM), 0)
        parts = []
        for i in range(c // SUBLANES):
            tile = None
            for j in range(SUBLANES // blk):
                row = jnp.broadcast_to(gc_ref[b, pl.ds(i * SUBLANES + j * blk + m - 1, 1), col],
                                       (SUBLANES, HEAD_DIM))
                tile = row if tile is None else jnp.where(sub >= j * blk, row, tile)
            parts.append(tile)
    return parts[0] if len(parts) == 1 else jnp.concatenate(parts, axis=0)


def _hgrn_chunk(p_ref, mix_ref, lb, ng, st_ref, gc_ref, *, r0, c, nb, row_lo):
    rows = lax.broadcasted_iota(jnp.int32, (c, 1), 0)
    ti = lax.broadcasted_iota(jnp.int32, (c, c), 0)
    si = lax.broadcasted_iota(jnp.int32, (c, c), 1)
    tri = (si <= ti).astype(F32).astype(BF16)
    eye = ti == si
    xr = ti ^ si
    pairs = [(b, h) for b in range(nb) for h in range(N_HEADS)]
    cols = [slice(h * HEAD_DIM, (h + 1) * HEAD_DIM) for h in range(N_HEADS)]
    rs = pl.ds(r0, c)

    def proj(b, j):
        return p_ref[b, rs, HG_COL + j * MIX_HALF:HG_COL + (j + 1) * MIX_HALF]

    q, k, gc = {}, {}, {}
    for b in range(nb):
        f = lb + (1.0 - lb) * jax.nn.sigmoid(proj(b, 1))
        g = jnp.log(f)
        kb = 1.0 - f
        if row_lo:
            g = jnp.where(rows >= row_lo, g, 0.0)
            kb = jnp.where(rows >= row_lo, kb, 0.0)
        qb = _silu(proj(b, 0))
        gcb = _cumsum_rows(tri, g)
        gc_ref[b] = gcb
        for h in range(N_HEADS):
            q[b, h], k[b, h], gc[b, h] = qb[:, cols[h]], kb[:, cols[h]], gcb[:, cols[h]]

    level_of = jnp.where(eye, 0, -1)
    m = 1
    while m < c:
        level_of = jnp.where((xr >= m) & (xr < 2 * m) & (si < ti), m, level_of)
        m *= 2
    rows_w = lax.broadcasted_iota(jnp.int32, (c, HEAD_DIM), 0)
    sc = _multi_dot(q, k, pairs, nt=True)
    sc = {bh: jnp.where(level_of == 0, sc[bh], 0.0) for bh in pairs}
    m = c // 2
    while m >= 1:
        sign = jnp.where((rows_w & m) != 0, LOG2_E, -LOG2_E)
        qw, kw = {}, {}
        for bh in pairs:
            w = jnp.exp2((gc[bh] - _ref_rows(gc_ref, bh[0], cols[bh[1]], m, c)) * sign)
            qw[bh], kw[bh] = q[bh] * w, k[bh] * w
        prod = _multi_dot(qw, kw, pairs, nt=True)
        sc = {bh: jnp.where(level_of == m, prod[bh], sc[bh]) for bh in pairs}
        m //= 2

    v = {(b, h): p_ref[b, rs, HG_COL + 2 * MIX_HALF + h * HEAD_DIM:HG_COL + 2 * MIX_HALF + (h + 1) * HEAD_DIM]
         for b, h in pairs}
    st = {bh: st_ref[bh] for bh in pairs}
    o_intra = _multi_dot(sc, v, pairs)
    o_inter = _multi_dot({bh: q[bh] * jnp.exp(gc[bh]) for bh in pairs}, st, pairs, nt=True)
    o = {bh: o_intra[bh] + o_inter[bh] for bh in pairs}
    for b, h in pairs:
        g_last = gc_ref[b, pl.ds(c - 1, 1), cols[h]]
        st_ref[b, h] = st[b, h] * jnp.exp(g_last) + _dot_tn(v[b, h], k[b, h] * jnp.exp(g_last - gc[b, h]))
    for b, h in pairs:
        ob = o[b, h]
        on = ob * lax.rsqrt(jnp.mean(ob * ob, axis=-1, keepdims=True) + RMS_EPS) * ng
        gate = p_ref[b, rs, HG_COL + 3 * MIX_HALF + h * HEAD_DIM:HG_COL + 3 * MIX_HALF + (h + 1) * HEAD_DIM]
        mix_ref[b, rs, cols[h]] = on * _silu(gate)


def _gdn_rows(p_ref, t_ref, hist_ref, mix_ref, cb_ref, cw_ref, neg_a, dt_bias, ng, s_ref, xp_ref,
              *, c, nc, nb, row_lo):
    n_hist = GD_CONV - 1
    rb = c * nc
    rows = lax.broadcasted_iota(jnp.int32, (rb, 1), 0) % c
    ti = lax.broadcasted_iota(jnp.int32, (c, c), 0)
    si = lax.broadcasted_iota(jnp.int32, (c, c), 1)
    tri = (si <= ti).astype(F32).astype(BF16)
    eye_f = (ti == si).astype(F32)
    incl = si <= ti

    chains = [(b, j, h) for b in range(nb) for j in range(nc) for h in range(N_HEADS)]
    q_all, k_all, v_all, beta_all, la_all = {}, {}, {}, {}, {}
    for b in range(nb):
        qkv = []
        for j in range(3):
            col = slice(j * MIX_HALF, (j + 1) * MIX_HALF)
            x = p_ref[b, :, GD_COL + j * MIX_HALF:GD_COL + (j + 1) * MIX_HALF]
            if hist_ref is not None:
                x = jnp.where((rows >= row_lo - n_hist) & (rows < row_lo), hist_ref[b, :, col], x)
            xp_ref[b, SUBLANES:SUBLANES + rb, col] = x
            y = x * cw_ref[n_hist:n_hist + 1, col]
            for tap in range(n_hist):
                off = SUBLANES - n_hist + tap
                y = y + xp_ref[b, off:off + rb, col] * cw_ref[tap:tap + 1, col]
            qkv.append(_silu(y))
        cb_ref[b] = xp_ref[b, SUBLANES + rb - n_hist:SUBLANES + rb, :]
        xp_ref[b, 0:SUBLANES, :] = xp_ref[b, rb:rb + SUBLANES, :]
        q_all[b], k_all[b], v_all[b] = qkv

        tail = t_ref[b]
        beta_all[b] = jax.nn.sigmoid(tail)
        la_all[b] = neg_a * jax.nn.softplus(tail + dt_bias)
        if row_lo:
            beta_all[b] = jnp.where(rows >= row_lo, beta_all[b], 0.0)
            la_all[b] = jnp.where(rows >= row_lo, la_all[b], 0.0)

    gc = {(b, j): _cumsum_rows(tri, la_all[b][j * c:(j + 1) * c]) for b in range(nb) for j in range(nc)}

    pre = {}
    for ch in chains:
        b, j, h = ch
        rs = slice(j * c, (j + 1) * c)
        col = slice(h * HEAD_DIM, (h + 1) * HEAD_DIM)
        qh, kh, vh = q_all[b][rs, col], k_all[b][rs, col], v_all[b][rs, col]
        qh = qh * lax.rsqrt(jnp.sum(qh * qh, axis=-1, keepdims=True) + RMS_EPS) * (HEAD_DIM ** -0.5)
        kh = kh * lax.rsqrt(jnp.sum(kh * kh, axis=-1, keepdims=True) + RMS_EPS)
        beta = beta_all[b][rs, h:h + 1]
        gcol = gc[b, j][:, N_HEADS + h:N_HEADS + h + 1]
        grow = jnp.sum(eye_f * gcol, axis=0, keepdims=True)
        decay = jnp.where(incl, jnp.exp(jnp.where(incl, gcol - grow, 0.0)), 0.0)
        gamma = jnp.exp(gcol)
        g_last = gcol[c - 1:c, :]
        pre[ch] = dict(
            q=qh, k=kh, beta=beta, decay=decay,
            rhs=jnp.concatenate([(beta * gamma) * kh, beta * vh], axis=1).astype(BF16),
            gq=(gamma * qh).astype(BF16),
            kd=(kh * jnp.exp(g_last - gcol)).astype(BF16),
            eg=jnp.exp(g_last))
    k_b = {ch: pre[ch]["k"].astype(BF16) for ch in chains}
    kk = _multi_dot(k_b, k_b, chains, nt=True)
    qk = _multi_dot({ch: pre[ch]["q"] for ch in chains}, k_b, chains, nt=True)
    a_mat = {ch: pre[ch]["beta"] * kk[ch] * pre[ch]["decay"] for ch in chains}
    aqk = {ch: (qk[ch] * pre[ch]["decay"]).astype(BF16) for ch in chains}

    xr = ti ^ si
    first = (xr < 2) & (si < ti)
    t_inv = {ch: eye_f - jnp.where(first, a_mat[ch], 0.0) for ch in chains}
    n = 2
    while n < c:
        lower = (xr < 2 * n) & ((ti & n) != 0) & ((si & n) == 0)
        x = _multi_dot({ch: jnp.where(lower, a_mat[ch], 0.0) for ch in chains}, t_inv, chains)
        tx = _multi_dot(t_inv, x, chains)
        t_inv = {ch: t_inv[ch] - tx[ch] for ch in chains}
        n *= 2
    wu = {ch: _dot(t_inv[ch], pre[ch]["rhs"]) for ch in chains}

    for j in range(nc):
        rs = slice(j * c, (j + 1) * c)
        group = [(b, j, h) for b in range(nb) for h in range(N_HEADS)]
        s_old = {ch: s_ref[ch[0], ch[2]] for ch in group}
        ws = _multi_dot({ch: jnp.concatenate([wu[ch][:, :HEAD_DIM].astype(BF16), pre[ch]["gq"]], axis=0)
                         for ch in group}, s_old, group)
        u = {ch: (wu[ch][:, HEAD_DIM:] - ws[ch][:c]).astype(BF16) for ch in group}
        au = _multi_dot(aqk, u, group)
        for ch in group:
            b, _, h = ch
            s_ref[b, h] = pre[ch]["eg"] * s_old[ch] + _dot_tn(pre[ch]["kd"], u[ch])
            o = ws[ch][c:] + au[ch]
            on = o * lax.rsqrt(jnp.mean(o * o, axis=-1, keepdims=True) + RMS_EPS) * ng
            gate = p_ref[b, rs, GD_COL + 3 * MIX_HALF + h * HEAD_DIM:GD_COL + 3 * MIX_HALF + (h + 1) * HEAD_DIM]
            mix_ref[b, rs, MIX_HALF + h * HEAD_DIM:MIX_HALF + (h + 1) * HEAD_DIM] = on * _silu(gate)


def _mixer_kernel(*refs, c, nc, nb, fuse_proj, has_state, row_lo, n_steps):
    refs = list(refs)
    if fuse_proj:
        x_ref, w_ref, wt_ref = refs[:3]
        del refs[:3]
    else:
        p_ref, t_ref = refs[:2]
        del refs[:2]
    lbl_ref, hng_ref, cw_ref, par_ref, gng_ref = refs[:5]
    del refs[:5]
    if has_state:
        hist_ref, shg0_ref, sgd0_ref = refs[:3]
        del refs[:3]
    else:
        hist_ref = shg0_ref = sgd0_ref = None
    mix_ref, shg_ref, sgd_ref, cb_ref, st_ref, s_ref, gc_ref, xp_ref = refs[:8]
    if fuse_proj:
        p_ref, t_ref = refs[8:]
    step = pl.program_id(1)
    pairs = [(b, h) for b in range(nb) for h in range(N_HEADS)]

    @pl.when(step == 0)
    def _():
        for b, h in pairs:
            if has_state:
                st_ref[b, h] = shg0_ref[b, h].T
                s_ref[b, h] = sgd0_ref[b, h]
            else:
                st_ref[b, h] = jnp.zeros((HEAD_DIM, HEAD_DIM), F32)
                s_ref[b, h] = jnp.zeros((HEAD_DIM, HEAD_DIM), F32)
        for b in range(nb):
            xp_ref[b, 0:SUBLANES, :] = jnp.zeros((SUBLANES, 3 * MIX_HALF), F32)

    if fuse_proj:
        for b in range(nb):
            xb = x_ref[b].astype(BF16)
            p_ref[b] = jnp.dot(xb, w_ref[...], preferred_element_type=F32)
            t_ref[b] = jnp.dot(xb, wt_ref[...], preferred_element_type=F32)

    lbl = lbl_ref[...]
    e = jnp.exp(lbl - jnp.max(lbl, axis=0, keepdims=True))
    lb = e[0:1] / jnp.sum(e, axis=0, keepdims=True)
    neg_a = -jnp.exp(par_ref[0:1, :])
    dt_bias = par_ref[1:2, :]

    for j in range(nc):
        _hgrn_chunk(p_ref, mix_ref, lb, hng_ref[...], st_ref, gc_ref, r0=j * c, c=c, nb=nb, row_lo=row_lo)
    _gdn_rows(p_ref, t_ref, hist_ref, mix_ref, cb_ref, cw_ref, neg_a, dt_bias, gng_ref[...], s_ref, xp_ref,
              c=c, nc=nc, nb=nb, row_lo=row_lo)

    @pl.when(step == n_steps - 1)
    def _():
        for b, h in pairs:
            shg_ref[b, h] = st_ref[b, h].T
            sgd_ref[b, h] = s_ref[b, h]


def _mixer(x3, proj, tail, weights, hist, s_hg, s_gd, *, c, nc, nb, row_lo):
    lb_logits, w_main, w_tail, conv_w, a_log, dt_bias, hg_norm_g, gd_norm_g = weights
    fuse_proj = proj is None
    bsz, t = (x3 if fuse_proj else proj).shape[:2]
    rb = c * nc
    n_steps = t // rb
    has_state = s_hg is not None
    n_hist = GD_CONV - 1
    n_proj = w_main.shape[1]
    const = lambda shape: pl.BlockSpec(shape, lambda i, si: (0,) * len(shape))
    rows3 = lambda width: pl.BlockSpec((nb, rb, width), lambda i, si: (i, si, 0))
    st_spec = pl.BlockSpec((nb, N_HEADS, HEAD_DIM, HEAD_DIM), lambda i, si: (i, 0, 0, 0))
    par = jnp.zeros((SUBLANES, HEAD_DIM), F32)
    par = par.at[0, N_HEADS:2 * N_HEADS].set(a_log).at[1, N_HEADS:2 * N_HEADS].set(dt_bias)
    if fuse_proj:
        in_specs = [rows3(D_MODEL), const(w_main.shape), const(w_tail.shape)]
        args = [x3, w_main, w_tail]
    else:
        in_specs = [rows3(n_proj), rows3(HEAD_DIM)]
        args = [proj, tail]
    in_specs += [const(lb_logits.shape), const((1, HEAD_DIM)), const(conv_w.shape), const(par.shape),
                 const((1, HEAD_DIM))]
    args += [lb_logits, hg_norm_g.reshape(1, HEAD_DIM), conv_w, par, gd_norm_g.reshape(1, HEAD_DIM)]
    if has_state:
        in_specs += [rows3(3 * MIX_HALF), st_spec, st_spec]
        args += [hist, s_hg, s_gd]
    scratch = [pltpu.VMEM((nb, N_HEADS, HEAD_DIM, HEAD_DIM), F32),
               pltpu.VMEM((nb, N_HEADS, HEAD_DIM, HEAD_DIM), F32),
               pltpu.VMEM((nb, c, MIX_HALF), F32),
               pltpu.VMEM((nb, SUBLANES + rb, 3 * MIX_HALF), F32)]
    if fuse_proj:
        scratch += [pltpu.VMEM((nb, rb, n_proj), F32), pltpu.VMEM((nb, rb, HEAD_DIM), F32)]
    return pl.pallas_call(
        functools.partial(_mixer_kernel, c=c, nc=nc, nb=nb, fuse_proj=fuse_proj, has_state=has_state,
                          row_lo=row_lo, n_steps=n_steps),
        grid=(bsz // nb, n_steps),
        in_specs=in_specs,
        out_specs=[rows3(D_MODEL), st_spec, st_spec,
                   pl.BlockSpec((nb, n_hist, 3 * MIX_HALF), lambda i, si: (i, 0, 0))],
        out_shape=[jax.ShapeDtypeStruct((bsz, t, D_MODEL), F32),
                   jax.ShapeDtypeStruct((bsz, N_HEADS, HEAD_DIM, HEAD_DIM), F32),
                   jax.ShapeDtypeStruct((bsz, N_HEADS, HEAD_DIM, HEAD_DIM), F32),
                   jax.ShapeDtypeStruct((bsz, n_hist, 3 * MIX_HALF), F32)],
        scratch_shapes=scratch,
        compiler_params=_params(2),
        name="mixer",
    )(*args)


def _attn_kernel(q_ref, k_ref, v_ref, o_ref, *, nb):
    scale = MEM_HEAD_DIM ** -0.5
    pairs = [(b, h) for b in range(nb) for h in range(MEM_HEADS)]

    def head_block(ref, b, h):
        if ref.shape[-1] == LANES:
            rows_per_mem = MEM_HEADS * MEM_HEAD_DIM // LANES
            return jnp.concatenate([ref[b, pl.ds(h + MEM_HEADS * j, N_MEM, stride=rows_per_mem), :]
                                    for j in range(MEM_HEAD_DIM // LANES)], axis=1)
        return ref[b, :, h * MEM_HEAD_DIM:(h + 1) * MEM_HEAD_DIM]

    s = {bh: _dot_nt(q_ref[bh[0], :, bh[1] * MEM_HEAD_DIM:(bh[1] + 1) * MEM_HEAD_DIM],
                     head_block(k_ref, *bh)) * scale for bh in pairs}
    p = {}
    for bh in pairs:
        e = jnp.exp(s[bh] - jnp.max(s[bh], axis=-1, keepdims=True))
        p[bh] = e / jnp.sum(e, axis=-1, keepdims=True)
    for b, h in pairs:
        o_ref[b, :, h * MEM_HEAD_DIM:(h + 1) * MEM_HEAD_DIM] = _dot(p[b, h], head_block(v_ref, b, h))


def _mem_attn_core(q3, mk, mv, *, tq, nb):
    bsz, t, d = q3.shape
    tq = min(tq, t)
    kv_spec = pl.BlockSpec((nb,) + mk.shape[1:], lambda i, r: (i, 0, 0))
    return pl.pallas_call(
        functools.partial(_attn_kernel, nb=nb),
        grid=(bsz // nb, t // tq),
        in_specs=[pl.BlockSpec((nb, tq, d), lambda i, r: (i, r, 0)), kv_spec, kv_spec],
        out_specs=pl.BlockSpec((nb, tq, d), lambda i, r: (i, r, 0)),
        out_shape=jax.ShapeDtypeStruct((bsz, t, d), F32),
        compiler_params=_params(2),
        name="mem_attn_core",
    )(q3, mk, mv)


def _attn_block_kernel(mix_ref, x_ref, k_ref, v_ref, wo_ref, wq_ref, wm_ref, g1_ref, b1_ref, g2_ref, b2_ref,
                       o_ref):
    def layer_norm(acc, g_ref, b_ref):
        mu = jnp.mean(acc, axis=-1, keepdims=True)
        xc = acc - mu
        var = jnp.mean(xc * xc, axis=-1, keepdims=True)
        return xc * lax.rsqrt(var + LN_EPS) * g_ref[...] + b_ref[...]

    h1 = layer_norm(ALPHA * x_ref[0] + jnp.dot(mix_ref[0].astype(BF16), wo_ref[...], preferred_element_type=F32),
                    g1_ref, b1_ref)
    q = jnp.dot(h1.astype(BF16), wq_ref[...], preferred_element_type=F32).astype(BF16)
    scale = MEM_HEAD_DIM ** -0.5
    cols = [slice(h * MEM_HEAD_DIM, (h + 1) * MEM_HEAD_DIM) for h in range(MEM_HEADS)]
    s = [_dot_nt(q[:, c], k_ref[0, :, c]) * scale for c in cols]
    p = []
    for sh in s:
        e = jnp.exp(sh - jnp.max(sh, axis=-1, keepdims=True))
        p.append(e / jnp.sum(e, axis=-1, keepdims=True))
    att = jnp.concatenate([_dot(ph, v_ref[0, :, c]).astype(BF16) for ph, c in zip(p, cols)], axis=1)
    o_ref[0] = layer_norm(ALPHA * h1 + jnp.dot(att, wm_ref[...], preferred_element_type=F32), g2_ref, b2_ref)


def _attn_block(mix3, x3, mk, mv, w_out, w_mq, w_mo, ln1_g, ln1_b, ln2_g, ln2_b, *, tm):
    bsz, t, d = x3.shape
    tm = min(tm, t)
    rows = pl.BlockSpec((1, tm, d), lambda i, r: (i, r, 0))
    kv = pl.BlockSpec((1, N_MEM, d), lambda i, r: (i, 0, 0))
    wsp = pl.BlockSpec((d, d), lambda i, r: (0, 0))
    vec = pl.BlockSpec((1, d), lambda i, r: (0, 0))
    return pl.pallas_call(
        _attn_block_kernel,
        grid=(bsz, t // tm),
        in_specs=[rows, rows, kv, kv, wsp, wsp, wsp, vec, vec, vec, vec],
        out_specs=rows,
        out_shape=jax.ShapeDtypeStruct((bsz, t, d), F32),
        compiler_params=_params(2),
        name="attn_block",
    )(mix3, x3, mk, mv, w_out, w_mq, w_mo, ln1_g.reshape(1, d), ln1_b.reshape(1, d),
      ln2_g.reshape(1, d), ln2_b.reshape(1, d))


def _ffn_up_kernel(*refs, tm, grp, row_lo, has_hist):
    if has_hist:
        h_ref, wg_ref, wv_ref, cw_ref, cb_ref, hist_ref, act_ref, buf_ref, xp_ref = refs
    else:
        h_ref, wg_ref, wv_ref, cw_ref, cb_ref, act_ref, buf_ref, xp_ref = refs
        hist_ref = None
    n_hist = FFN_CONV - 1
    r = pl.program_id(2)

    @pl.when(r == 0)
    def _():
        xp_ref[0:SUBLANES, :] = jnp.zeros((SUBLANES, xp_ref.shape[1]), F32)

    hb = h_ref[0].astype(BF16)
    gate = jnp.dot(hb, wg_ref[...], preferred_element_type=F32)
    val = jnp.dot(hb, wv_ref[...], preferred_element_type=F32)
    if hist_ref is not None:
        pos = lax.broadcasted_iota(jnp.int32, (tm, 1), 0) % grp
        gate = jnp.where((pos >= row_lo - n_hist) & (pos < row_lo), hist_ref[0], gate)
    xp_ref[SUBLANES:SUBLANES + tm, :] = gate
    y = gate * cw_ref[n_hist:n_hist + 1, :] + cb_ref[...]
    for tap in range(n_hist):
        off = SUBLANES - n_hist + tap
        y = y + xp_ref[off:off + tm, :] * cw_ref[tap:tap + 1, :]
    gelu = 0.5 * y * (1.0 + lax.erf(y * (2.0 ** -0.5)))
    act_ref[0] = (gelu * val).astype(act_ref.dtype)
    if buf_ref.shape[1] == n_hist:
        buf_ref[0] = xp_ref[SUBLANES + tm - n_hist:SUBLANES + tm, :]
    else:
        buf_ref[0] = gate
    xp_ref[0:SUBLANES, :] = xp_ref[tm:tm + SUBLANES, :]


def _ffn_up(h3, w_gate, w_val, conv_w, conv_b, hist, *, tm, tn, grp, row_lo):
    g, r, d = h3.shape
    f = w_gate.shape[1]
    tm = min(tm, r)
    n_hist = FFN_CONV - 1
    has_hist = hist is not None
    assert grp == r or grp <= tm
    if grp == r:
        buf_spec = pl.BlockSpec((1, n_hist, tn), lambda gi, fi, ri: (gi, 0, fi))
        buf_rows = n_hist
    else:
        buf_spec = pl.BlockSpec((1, tm, tn), lambda gi, fi, ri: (gi, ri, fi))
        buf_rows = r
    w_spec = pl.BlockSpec((d, tn), lambda gi, fi, ri: (0, fi))
    in_specs = [pl.BlockSpec((1, tm, d), lambda gi, fi, ri: (gi, ri, 0)), w_spec, w_spec,
                pl.BlockSpec((FFN_CONV, tn), lambda gi, fi, ri: (0, fi)),
                pl.BlockSpec((1, tn), lambda gi, fi, ri: (0, fi))]
    args = [h3, w_gate, w_val, conv_w, conv_b.reshape(1, f)]
    if has_hist:
        in_specs.append(pl.BlockSpec((1, tm, tn), lambda gi, fi, ri: (gi, ri, fi)))
        args.append(hist)
    return pl.pallas_call(
        functools.partial(_ffn_up_kernel, tm=tm, grp=grp, row_lo=row_lo, has_hist=has_hist),
        grid=(g, f // tn, r // tm),
        in_specs=in_specs,
        out_specs=[pl.BlockSpec((1, tm, tn), lambda gi, fi, ri: (gi, ri, fi)),
                   buf_spec],
        out_shape=[jax.ShapeDtypeStruct((g, r, f), BF16),
                   jax.ShapeDtypeStruct((g, buf_rows, f), F32)],
        scratch_shapes=[pltpu.VMEM((SUBLANES + tm, tn), F32)],
        compiler_params=_params(3),
        name="ffn_up",
    )(*args)


def _layer(x3, s_hg, s_gd, hist_gd, hist_ffn, mk, mv, weights, *, row_lo, grp, ffn_streams, c_mix, nc_mix,
           nb_mix, fuse_proj, nb_attn):
    (mixer_w, w_out, ln1_g, ln1_b, w_mq, w_mo, ln2_g, ln2_b,
     w_gate, w_val, w_ffn_conv, b_ffn_conv, w_down, ln3_g, ln3_b) = weights
    bsz, t, d = x3.shape
    rows = bsz * t
    x2 = x3.reshape(rows, d)

    if fuse_proj:
        proj = tail = None
    else:
        proj = _matmul(x2, mixer_w[1], 1024, 512).reshape(bsz, t, -1)
        tail = _matmul(x2, mixer_w[2], 1024, HEAD_DIM).reshape(bsz, t, HEAD_DIM)
    mix, new_hg, new_gd, new_buf_gd = _mixer(x3, proj, tail, mixer_w, hist_gd, s_hg, s_gd,
                                             c=c_mix, nc=nc_mix, nb=nb_mix, row_lo=row_lo)

    if nb_attn is None:
        h2 = _attn_block(mix, x3, mk, mv, w_out, w_mq, w_mo, ln1_g, ln1_b, ln2_g, ln2_b, tm=512).reshape(rows, d)
    else:
        h1 = _matmul_res_ln([mix.reshape(rows, d)], [w_out], x2, ln1_g, ln1_b, 512)
        q = _matmul(h1, w_mq, 1024, 1024)
        att = _mem_attn_core(q.reshape(bsz, t, d), mk, mv, tq=512, nb=nb_attn)
        h2 = _matmul_res_ln([att.reshape(rows, d)], [w_mo], h1, ln2_g, ln2_b, 512)

    h2_3 = h2.reshape(ffn_streams, rows // ffn_streams, d)
    hist3 = None if hist_ffn is None else hist_ffn.reshape(ffn_streams, rows // ffn_streams, D_FF)
    act, new_buf_ffn = _ffn_up(h2_3, w_gate, w_val, w_ffn_conv, b_ffn_conv, hist3,
                               tm=512, tn=D_FF // 2, grp=grp, row_lo=row_lo)
    y = _matmul_res_ln([act.reshape(rows, D_FF)], [w_down], h2, ln3_g, ln3_b, 512)
    return y.reshape(bsz, t, d), new_hg, new_gd, new_buf_gd, new_buf_ffn


def _cache_rows(cache):
    b, m, h, d = cache.shape
    return cache.reshape(b, m, h, d // LANES, LANES).transpose(0, 1, 3, 2, 4).reshape(b, m * h * d // LANES, LANES)


def kernel(x_prompt, x_sample, state_hgrn, state_gdn, state_gdn_conv, state_ffn_conv, cache_mem_k, cache_mem_v, mem_prompt, hgrn_lb_logits, w_in, w_gd_conv, gd_a_log, gd_dt_bias, hg_norm_g, gd_norm_g, w_out, ln1_g, ln1_b, w_mq, w_mkv, w_mo, ln2_g, ln2_b, w_up, w_ffn_conv, b_ffn_conv, w_down, ln3_g, ln3_b):
    bp, tp, d = x_prompt.shape
    bs, ts, _ = x_sample.shape
    tpad = SUBLANES
    row_lo = tpad - ts
    l = 0

    main_cols = 8 * MIX_HALF
    w_in_tail = jnp.pad(w_in[l][:, main_cols:], ((0, 0), (0, HEAD_DIM - 2 * N_HEADS))).astype(BF16)
    mixer_w = (hgrn_lb_logits, w_in[l][:, :main_cols].astype(BF16), w_in_tail, w_gd_conv[l],
               gd_a_log[l], gd_dt_bias[l], hg_norm_g[l], gd_norm_g[l])
    weights = (mixer_w, w_out[l].astype(BF16), ln1_g[l], ln1_b[l],
               w_mq[l].astype(BF16), w_mo[l].astype(BF16), ln2_g[l], ln2_b[l],
               w_up[l][:, :D_FF].astype(BF16), w_up[l][:, D_FF:].astype(BF16),
               w_ffn_conv[l], b_ffn_conv[l], w_down[l].astype(BF16), ln3_g[l], ln3_b[l])

    mem2 = mem_prompt.reshape(bp * N_MEM, d)
    mk = _matmul(mem2, w_mkv[l][:, :d].astype(BF16), 1024, 1024).reshape(bp, N_MEM, d)
    mv = _matmul(mem2, w_mkv[l][:, d:].astype(BF16), 1024, 1024).reshape(bp, N_MEM, d)
    yp, p_hg, p_gd, p_bgd, p_bff = _layer(x_prompt, None, None, None, None, mk, mv, weights,
                                          row_lo=0, grp=tp, ffn_streams=bp, c_mix=128, nc_mix=4, nb_mix=1,
                                          fuse_proj=True, nb_attn=None)

    n_hg = GD_CONV - 1
    n_hf = FFN_CONV - 1
    xs = jnp.pad(x_sample, ((0, 0), (row_lo, 0), (0, 0)))
    hist_gd = jnp.pad(state_gdn_conv[l], ((0, 0), (row_lo - n_hg, ts), (0, 0)))
    hist_ffn = jnp.pad(state_ffn_conv[l], ((0, 0), (row_lo - n_hf, ts), (0, 0)))
    ys, s_hg, s_gd, s_bgd, s_bff = _layer(
        xs, state_hgrn[l], state_gdn[l], hist_gd, hist_ffn,
        _cache_rows(cache_mem_k[l]), _cache_rows(cache_mem_v[l]), weights,
        row_lo=row_lo, grp=tpad, ffn_streams=1, c_mix=tpad, nc_mix=1, nb_mix=8, fuse_proj=False,
        nb_attn=4)
    ys = ys[:, row_lo:]
    s_bff = s_bff.reshape(bs, tpad, D_FF)[:, tpad - n_hf:]

    mem_shape = (1, bp, N_MEM, MEM_HEADS, MEM_HEAD_DIM)
    return (yp, ys, p_hg[None], p_gd[None], p_bgd[None], p_bff[None],
            mk.reshape(mem_shape), mv.reshape(mem_shape),
            s_hg[None], s_gd[None], s_bgd[None], s_bff[None])
```

```python
import functools

import jax
import jax.numpy as jnp
from jax import lax
from jax.experimental import pallas as pl
from jax.experimental.pallas import tpu as pltpu

F32 = jnp.float32
BF16 = jnp.bfloat16

D_MODEL = 1024
HEAD_DIM = 128
N_HEADS = 4
MIX_HALF = N_HEADS * HEAD_DIM
GD_CONV = 4
FFN_CONV = 3
D_FF = 2816
N_MEM = 256
MEM_HEADS = 4
MEM_HEAD_DIM = D_MODEL // MEM_HEADS
LN_EPS = 1e-5
RMS_EPS = 1e-6
LOG2_E = 1.4426950408889634
DEPTH = 1
ALPHA = (2.0 * DEPTH) ** 0.25

SUBLANES = 8
LANES = 128
VMEM_LIMIT = 56 * 1024 * 1024


def _params(n_axes):
    return pltpu.CompilerParams(dimension_semantics=("arbitrary",) * n_axes,
                                vmem_limit_bytes=VMEM_LIMIT)


def _dot(a, b):
    return jnp.dot(a.astype(BF16), b.astype(BF16), preferred_element_type=F32)


def _dot_nt(a, b):
    return lax.dot_general(a.astype(BF16), b.astype(BF16), (((1,), (1,)), ((), ())),
                           preferred_element_type=F32)


def _dot_tn(a, b):
    return lax.dot_general(a.astype(BF16), b.astype(BF16), (((0,), (0,)), ((), ())),
                           preferred_element_type=F32)


def _split3(x):
    x1 = x.astype(BF16)
    r = x - x1.astype(F32)
    x2 = r.astype(BF16)
    x3 = (r - x2.astype(F32)).astype(BF16)
    return x1, x2, x3


def _cumsum_rows(tri_bf16, x):
    x1, x2, x3 = _split3(x)
    return (jnp.dot(tri_bf16, x1, preferred_element_type=F32)
            + jnp.dot(tri_bf16, x2, preferred_element_type=F32)
            + jnp.dot(tri_bf16, x3, preferred_element_type=F32))


def _multi_dot(a, b, keys, nt=False):
    dot = _dot_nt if nt else _dot
    return {k: dot(a[k], b[k]) for k in keys}


def _silu(x):
    return x * jax.nn.sigmoid(x)


def _mm_kernel(x_ref, w_ref, o_ref, xb_ref):
    @pl.when(pl.program_id(1) == 0)
    def _():
        xb_ref[...] = x_ref[...].astype(BF16)

    o_ref[...] = jnp.dot(xb_ref[...], w_ref[...], preferred_element_type=F32).astype(o_ref.dtype)


def _matmul(x, w, tm, tn):
    m, k = x.shape
    n = w.shape[1]
    tm = min(tm, m)
    tn = min(tn, n)
    return pl.pallas_call(
        _mm_kernel,
        grid=(m // tm, n // tn),
        in_specs=[pl.BlockSpec((tm, k), lambda i, j: (i, 0)),
                  pl.BlockSpec((k, tn), lambda i, j: (0, j))],
        out_specs=pl.BlockSpec((tm, tn), lambda i, j: (i, j)),
        out_shape=jax.ShapeDtypeStruct((m, n), F32),
        scratch_shapes=[pltpu.VMEM((tm, k), BF16)],
        compiler_params=_params(2),
        name="proj_matmul",
    )(x, w)


def _mm_ln_kernel(*refs, n_in):
    a_refs = refs[:n_in]
    w_refs = refs[n_in:2 * n_in]
    res_ref, g_ref, b_ref, o_ref = refs[2 * n_in:]
    acc = ALPHA * res_ref[...]
    for a_ref, w_ref in zip(a_refs, w_refs):
        acc = acc + jnp.dot(a_ref[...].astype(BF16), w_ref[...], preferred_element_type=F32)
    mu = jnp.mean(acc, axis=-1, keepdims=True)
    xc = acc - mu
    var = jnp.mean(xc * xc, axis=-1, keepdims=True)
    o_ref[...] = xc * lax.rsqrt(var + LN_EPS) * g_ref[...] + b_ref[...]


def _matmul_res_ln(a_list, w_list, res, g, b, tm):
    m, d = res.shape
    tm = min(tm, m)
    n_in = len(a_list)
    in_specs = ([pl.BlockSpec((tm, a.shape[1]), lambda i: (i, 0)) for a in a_list]
                + [pl.BlockSpec(w.shape, lambda i: (0, 0)) for w in w_list]
                + [pl.BlockSpec((tm, d), lambda i: (i, 0)),
                   pl.BlockSpec((1, d), lambda i: (0, 0)),
                   pl.BlockSpec((1, d), lambda i: (0, 0))])
    return pl.pallas_call(
        functools.partial(_mm_ln_kernel, n_in=n_in),
        grid=(m // tm,),
        in_specs=in_specs,
        out_specs=pl.BlockSpec((tm, d), lambda i: (i, 0)),
        out_shape=jax.ShapeDtypeStruct((m, d), F32),
        compiler_params=_params(1),
        name="proj_res_ln",
    )(*a_list, *w_list, res, g.reshape(1, d), b.reshape(1, d))


HG_COL = 0
GD_COL = 4 * MIX_HALF


def _ref_rows(gc_ref, b, col, m, c):
    blk = 2 * m
    if blk >= SUBLANES:
        parts = [jnp.broadcast_to(gc_ref[b, pl.ds(j * blk + m - 1, 1), col], (blk, HEAD_DIM))
                 for j in range(c // blk)]
    else:
        sub = lax.broadcasted_iota(jnp.int32, (SUBLANES, HEAD_DIM), 0)
        parts = []
        for i in range(c // SUBLANES):
            tile = None
            for j in range(SUBLANES // blk):
                row = jnp.broadcast_to(gc_ref[b, pl.ds(i * SUBLANES + j * blk + m - 1, 1), col],
                                       (SUBLANES, HEAD_DIM))
                tile = row if tile is None else jnp.where(sub >= j * blk, row, tile)
            parts.append(tile)
    return parts[0] if len(parts) == 1 else jnp.concatenate(parts, axis=0)


def _hgrn_chunk(p_ref, mix_ref, lb, ng, st_ref, gc_ref, *, r0, c, nb, row_lo):
    rows = lax.broadcasted_iota(jnp.int32, (c, 1), 0)
    ti = lax.broadcasted_iota(jnp.int32, (c, c), 0)
    si = lax.broadcasted_iota(jnp.int32, (c, c), 1)
    tri = (si <= ti).astype(F32).astype(BF16)
    eye = ti == si
    xr = ti ^ si
    pairs = [(b, h) for b in range(nb) for h in range(N_HEADS)]
    cols = [slice(h * HEAD_DIM, (h + 1) * HEAD_DIM) for h in range(N_HEADS)]
    rs = pl.ds(r0, c)

    def proj(b, j):
        return p_ref[b, rs, HG_COL + j * MIX_HALF:HG_COL + (j + 1) * MIX_HALF]

    q, k, gc = {}, {}, {}
    for b in range(nb):
        f = lb + (1.0 - lb) * jax.nn.sigmoid(proj(b, 1))
        g = jnp.log(f)
        kb = 1.0 - f
        if row_lo:
            g = jnp.where(rows >= row_lo, g, 0.0)
            kb = jnp.where(rows >= row_lo, kb, 0.0)
        qb = _silu(proj(b, 0))
        gcb = _cumsum_rows(tri, g)
        gc_ref[b] = gcb
        for h in range(N_HEADS):
            q[b, h], k[b, h], gc[b, h] = qb[:, cols[h]], kb[:, cols[h]], gcb[:, cols[h]]

    level_of = jnp.where(eye, 0, -1)
    m = 1
    while m < c:
        level_of = jnp.where((xr >= m) & (xr < 2 * m) & (si < ti), m, level_of)
        m *= 2
    rows_w = lax.broadcasted_iota(jnp.int32, (c, HEAD_DIM), 0)
    sc = _multi_dot(q, k, pairs, nt=True)
    sc = {bh: jnp.where(level_of == 0, sc[bh], 0.0) for bh in pairs}
    m = c // 2
    while m >= 1:
        sign = jnp.where((rows_w & m) != 0, LOG2_E, -LOG2_E)
        qw, kw = {}, {}
        for bh in pairs:
            w = jnp.exp2((gc[bh] - _ref_rows(gc_ref, bh[0], cols[bh[1]], m, c)) * sign)
            qw[bh], kw[bh] = q[bh] * w, k[bh] * w
        prod = _multi_dot(qw, kw, pairs, nt=True)
        sc = {bh: jnp.where(level_of == m, prod[bh], sc[bh]) for bh in pairs}
        m //= 2

    v = {(b, h): p_ref[b, rs, HG_COL + 2 * MIX_HALF + h * HEAD_DIM:HG_COL + 2 * MIX_HALF + (h + 1) * HEAD_DIM]
         for b, h in pairs}
    st = {bh: st_ref[bh] for bh in pairs}
    o_intra = _multi_dot(sc, v, pairs)
    o_inter = _multi_dot({bh: q[bh] * jnp.exp(gc[bh]) for bh in pairs}, st, pairs, nt=True)
    o = {bh: o_intra[bh] + o_inter[bh] for bh in pairs}
    for b, h in pairs:
        g_last = gc_ref[b, pl.ds(c - 1, 1), cols[h]]
        st_ref[b, h] = st[b, h] * jnp.exp(g_last) + _dot_tn(v[b, h], k[b, h] * jnp.exp(g_last - gc[b, h]))
    for b, h in pairs:
        ob = o[b, h]
        on = ob * lax.rsqrt(jnp.mean(ob * ob, axis=-1, keepdims=True) + RMS_EPS) * ng
        gate = p_ref[b, rs, HG_COL + 3 * MIX_HALF + h * HEAD_DIM:HG_COL + 3 * MIX_HALF + (h + 1) * HEAD_DIM]
        mix_ref[b, rs, cols[h]] = on * _silu(gate)


def _gdn_rows(p_ref, t_ref, hist_ref, mix_ref, cb_ref, cw_ref, neg_a, dt_bias, ng, s_ref, xp_ref,
              *, c, nc, nb, row_lo):
    n_hist = GD_CONV - 1
    rb = c * nc
    rows = lax.broadcasted_iota(jnp.int32, (rb, 1), 0) % c
    ti = lax.broadcasted_iota(jnp.int32, (c, c), 0)
    si = lax.broadcasted_iota(jnp.int32, (c, c), 1)
    tri = (si <= ti).astype(F32).astype(BF16)
    eye_f = (ti == si).astype(F32)
    incl = si <= ti

    chains = [(b, j, h) for b in range(nb) for j in range(nc) for h in range(N_HEADS)]
    q_all, k_all, v_all, beta_all, la_all = {}, {}, {}, {}, {}
    for b in range(nb):
        qkv = []
        for j in range(3):
            col = slice(j * MIX_HALF, (j + 1) * MIX_HALF)
            x = p_ref[b, :, GD_COL + j * MIX_HALF:GD_COL + (j + 1) * MIX_HALF]
            if hist_ref is not None:
                x = jnp.where((rows >= row_lo - n_hist) & (rows < row_lo), hist_ref[b, :, col], x)
            xp_ref[b, SUBLANES:SUBLANES + rb, col] = x
            y = x * cw_ref[n_hist:n_hist + 1, col]
            for tap in range(n_hist):
                off = SUBLANES - n_hist + tap
                y = y + xp_ref[b, off:off + rb, col] * cw_ref[tap:tap + 1, col]
            qkv.append(_silu(y))
        cb_ref[b] = xp_ref[b, SUBLANES + rb - n_hist:SUBLANES + rb, :]
        xp_ref[b, 0:SUBLANES, :] = xp_ref[b, rb:rb + SUBLANES, :]
        q_all[b], k_all[b], v_all[b] = qkv

        tail = t_ref[b]
        beta_all[b] = jax.nn.sigmoid(tail)
        la_all[b] = neg_a * jax.nn.softplus(tail + dt_bias)
        if row_lo:
            beta_all[b] = jnp.where(rows >= row_lo, beta_all[b], 0.0)
            la_all[b] = jnp.where(rows >= row_lo, la_all[b], 0.0)

    gc = {(b, j): _cumsum_rows(tri, la_all[b][j * c:(j + 1) * c]) for b in range(nb) for j in range(nc)}

    pre = {}
    for ch in chains:
        b, j, h = ch
        rs = slice(j * c, (j + 1) * c)
        col = slice(h * HEAD_DIM, (h + 1) * HEAD_DIM)
        qh, kh, vh = q_all[b][rs, col], k_all[b][rs, col], v_all[b][rs, col]
        qh = qh * lax.rsqrt(jnp.sum(qh * qh, axis=-1, keepdims=True) + RMS_EPS) * (HEAD_DIM ** -0.5)
        kh = kh * lax.rsqrt(jnp.sum(kh * kh, axis=-1, keepdims=True) + RMS_EPS)
        beta = beta_all[b][rs, h:h + 1]
        gcol = gc[b, j][:, N_HEADS + h:N_HEADS + h + 1]
        grow = jnp.sum(eye_f * gcol, axis=0, keepdims=True)
        decay = jnp.where(incl, jnp.exp(jnp.where(incl, gcol - grow, 0.0)), 0.0)
        gamma = jnp.exp(gcol)
        g_last = gcol[c - 1:c, :]
        pre[ch] = dict(
            q=qh, k=kh, beta=beta, decay=decay,
            rhs=jnp.concatenate([(beta * gamma) * kh, beta * vh], axis=1).astype(BF16),
            gq=(gamma * qh).astype(BF16),
            kd=(kh * jnp.exp(g_last - gcol)).astype(BF16),
            eg=jnp.exp(g_last))
    k_b = {ch: pre[ch]["k"].astype(BF16) for ch in chains}
    kk = _multi_dot(k_b, k_b, chains, nt=True)
    qk = _multi_dot({ch: pre[ch]["q"] for ch in chains}, k_b, chains, nt=True)
    a_mat = {ch: pre[ch]["beta"] * kk[ch] * pre[ch]["decay"] for ch in chains}
    aqk = {ch: (qk[ch] * pre[ch]["decay"]).astype(BF16) for ch in chains}

    xr = ti ^ si
    first = (xr < 2) & (si < ti)
    t_inv = {ch: eye_f - jnp.where(first, a_mat[ch], 0.0) for ch in chains}
    n = 2
    while n < c:
        lower = (xr < 2 * n) & ((ti & n) != 0) & ((si & n) == 0)
        x = _multi_dot({ch: jnp.where(lower, a_mat[ch], 0.0) for ch in chains}, t_inv, chains)
        tx = _multi_dot(t_inv, x, chains)
        t_inv = {ch: t_inv[ch] - tx[ch] for ch in chains}
        n *= 2
    wu = {ch: _dot(t_inv[ch], pre[ch]["rhs"]) for ch in chains}

    for j in range(nc):
        rs = slice(j * c, (j + 1) * c)
        group = [(b, j, h) for b in range(nb) for h in range(N_HEADS)]
        s_old = {ch: s_ref[ch[0], ch[2]] for ch in group}
        ws = _multi_dot({ch: jnp.concatenate([wu[ch][:, :HEAD_DIM].astype(BF16), pre[ch]["gq"]], axis=0)
                         for ch in group}, s_old, group)
        u = {ch: (wu[ch][:, HEAD_DIM:] - ws[ch][:c]).astype(BF16) for ch in group}
        au = _multi_dot(aqk, u, group)
        for ch in group:
            b, _, h = ch
            s_ref[b, h] = pre[ch]["eg"] * s_old[ch] + _dot_tn(pre[ch]["kd"], u[ch])
            o = ws[ch][c:] + au[ch]
            on = o * lax.rsqrt(jnp.mean(o * o, axis=-1, keepdims=True) + RMS_EPS) * ng
            gate = p_ref[b, rs, GD_COL + 3 * MIX_HALF + h * HEAD_DIM:GD_COL + 3 * MIX_HALF + (h + 1) * HEAD_DIM]
            mix_ref[b, rs, MIX_HALF + h * HEAD_DIM:MIX_HALF + (h + 1) * HEAD_DIM] = on * _silu(gate)


def _mixer_kernel(*refs, c, nc, nb, fuse_proj, has_state, row_lo, n_steps):
    refs = list(refs)
    if fuse_proj:
        x_ref, w_ref, wt_ref = refs[:3]
        del refs[:3]
    else:
        p_ref, t_ref = refs[:2]
        del refs[:2]
    lbl_ref, hng_ref, cw_ref, par_ref, gng_ref = refs[:5]
    del refs[:5]
    if has_state:
        hist_ref, shg0_ref, sgd0_ref = refs[:3]
        del refs[:3]
    else:
        hist_ref = shg0_ref = sgd0_ref = None
    mix_ref, shg_ref, sgd_ref, cb_ref, st_ref, s_ref, gc_ref, xp_ref = refs[:8]
    if fuse_proj:
        p_ref, t_ref = refs[8:]
    step = pl.program_id(1)
    pairs = [(b, h) for b in range(nb) for h in range(N_HEADS)]

    @pl.when(step == 0)
    def _():
        for b, h in pairs:
            if has_state:
                st_ref[b, h] = shg0_ref[b, h].T
                s_ref[b, h] = sgd0_ref[b, h]
            else:
                st_ref[b, h] = jnp.zeros((HEAD_DIM, HEAD_DIM), F32)
                s_ref[b, h] = jnp.zeros((HEAD_DIM, HEAD_DIM), F32)
        for b in range(nb):
            xp_ref[b, 0:SUBLANES, :] = jnp.zeros((SUBLANES, 3 * MIX_HALF), F32)

    if fuse_proj:
        for b in range(nb):
            xb = x_ref[b].astype(BF16)
            p_ref[b] = jnp.dot(xb, w_ref[...], preferred_element_type=F32)
            t_ref[b] = jnp.dot(xb, wt_ref[...], preferred_element_type=F32)

    lbl = lbl_ref[...]
    e = jnp.exp(lbl - jnp.max(lbl, axis=0, keepdims=True))
    lb = e[0:1] / jnp.sum(e, axis=0, keepdims=True)
    neg_a = -jnp.exp(par_ref[0:1, :])
    dt_bias = par_ref[1:2, :]

    for j in range(nc):
        _hgrn_chunk(p_ref, mix_ref, lb, hng_ref[...], st_ref, gc_ref, r0=j * c, c=c, nb=nb, row_lo=row_lo)
    _gdn_rows(p_ref, t_ref, hist_ref, mix_ref, cb_ref, cw_ref, neg_a, dt_bias, gng_ref[...], s_ref, xp_ref,
              c=c, nc=nc, nb=nb, row_lo=row_lo)

    @pl.when(step == n_steps - 1)
    def _():
        for b, h in pairs:
            shg_ref[b, h] = st_ref[b, h].T
            sgd_ref[b, h] = s_ref[b, h]


def _mixer(x3, proj, tail, weights, hist, s_hg, s_gd, *, c, nc, nb, row_lo):
    lb_logits, w_main, w_tail, conv_w, a_log, dt_bias, hg_norm_g, gd_norm_g = weights
    fuse_proj = proj is None
    bsz, t = (x3 if fuse_proj else proj).shape[:2]
    rb = c * nc
    n_steps = t // rb
    has_state = s_hg is not None
    n_hist = GD_CONV - 1
    n_proj = w_main.shape[1]
    const = lambda shape: pl.BlockSpec(shape, lambda i, si: (0,) * len(shape))
    rows3 = lambda width: pl.BlockSpec((nb, rb, width), lambda i, si: (i, si, 0))
    st_spec = pl.BlockSpec((nb, N_HEADS, HEAD_DIM, HEAD_DIM), lambda i, si: (i, 0, 0, 0))
    par = jnp.zeros((SUBLANES, HEAD_DIM), F32)
    par = par.at[0, N_HEADS:2 * N_HEADS].set(a_log).at[1, N_HEADS:2 * N_HEADS].set(dt_bias)
    if fuse_proj:
        in_specs = [rows3(D_MODEL), const(w_main.shape), const(w_tail.shape)]
        args = [x3, w_main, w_tail]
    else:
        in_specs = [rows3(n_proj), rows3(HEAD_DIM)]
        args = [proj, tail]
    in_specs += [const(lb_logits.shape), const((1, HEAD_DIM)), const(conv_w.shape), const(par.shape),
                 const((1, HEAD_DIM))]
    args += [lb_logits, hg_norm_g.reshape(1, HEAD_DIM), conv_w, par, gd_norm_g.reshape(1, HEAD_DIM)]
    if has_state:
        in_specs += [rows3(3 * MIX_HALF), st_spec, st_spec]
        args += [hist, s_hg, s_gd]
    scratch = [pltpu.VMEM((nb, N_HEADS, HEAD_DIM, HEAD_DIM), F32),
               pltpu.VMEM((nb, N_HEADS, HEAD_DIM, HEAD_DIM), F32),
               pltpu.VMEM((nb, c, MIX_HALF), F32),
               pltpu.VMEM((nb, SUBLANES + rb, 3 * MIX_HALF), F32)]
    if fuse_proj:
        scratch += [pltpu.VMEM((nb, rb, n_proj), F32), pltpu.VMEM((nb, rb, HEAD_DIM), F32)]
    return pl.pallas_call(
        functools.partial(_mixer_kernel, c=c, nc=nc, nb=nb, fuse_proj=fuse_proj, has_state=has_state,
                          row_lo=row_lo, n_steps=n_steps),
        grid=(bsz // nb, n_steps),
        in_specs=in_specs,
        out_specs=[rows3(D_MODEL), st_spec, st_spec,
                   pl.BlockSpec((nb, n_hist, 3 * MIX_HALF), lambda i, si: (i, 0, 0))],
        out_shape=[jax.ShapeDtypeStruct((bsz, t, D_MODEL), F32),
                   jax.ShapeDtypeStruct((bsz, N_HEADS, HEAD_DIM, HEAD_DIM), F32),
                   jax.ShapeDtypeStruct((bsz, N_HEADS, HEAD_DIM, HEAD_DIM), F32),
                   jax.ShapeDtypeStruct((bsz, n_hist, 3 * MIX_HALF), F32)],
        scratch_shapes=scratch,
        compiler_params=_params(2),
        name="mixer",
    )(*args)


def _attn_kernel(q_ref, k_ref, v_ref, o_ref, *, nb):
    scale = MEM_HEAD_DIM ** -0.5
    pairs = [(b, h) for b in range(nb) for h in range(MEM_HEADS)]

    def head_block(ref, b, h):
        if ref.shape[-1] == LANES:
            rows_per_mem = MEM_HEADS * MEM_HEAD_DIM // LANES
            return jnp.concatenate([ref[b, pl.ds(h + MEM_HEADS * j, N_MEM, stride=rows_per_mem), :]
                                    for j in range(MEM_HEAD_DIM // LANES)], axis=1)
        return ref[b, :, h * MEM_HEAD_DIM:(h + 1) * MEM_HEAD_DIM]

    s = {bh: _dot_nt(q_ref[bh[0], :, bh[1] * MEM_HEAD_DIM:(bh[1] + 1) * MEM_HEAD_DIM],
                     head_block(k_ref, *bh)) * scale for bh in pairs}
    p = {}
    for bh in pairs:
        e = jnp.exp(s[bh] - jnp.max(s[bh], axis=-1, keepdims=True))
        p[bh] = e / jnp.sum(e, axis=-1, keepdims=True)
    for b, h in pairs:
        o_ref[b, :, h * MEM_HEAD_DIM:(h + 1) * MEM_HEAD_DIM] = _dot(p[b, h], head_block(v_ref, b, h))


def _mem_attn_core(q3, mk, mv, *, tq, nb):
    bsz, t, d = q3.shape
    tq = min(tq, t)
    kv_spec = pl.BlockSpec((nb,) + mk.shape[1:], lambda i, r: (i, 0, 0))
    return pl.pallas_call(
        functools.partial(_attn_kernel, nb=nb),
        grid=(bsz // nb, t // tq),
        in_specs=[pl.BlockSpec((nb, tq, d), lambda i, r: (i, r, 0)), kv_spec, kv_spec],
        out_specs=pl.BlockSpec((nb, tq, d), lambda i, r: (i, r, 0)),
        out_shape=jax.ShapeDtypeStruct((bsz, t, d), F32),
        compiler_params=_params(2),
        name="mem_attn_core",
    )(q3, mk, mv)


def _attn_block_kernel(mix_ref, x_ref, k_ref, v_ref, wo_ref, wq_ref, wm_ref, g1_ref, b1_ref, g2_ref, b2_ref,
                       o_ref):
    def layer_norm(acc, g_ref, b_ref):
        mu = jnp.mean(acc, axis=-1, keepdims=True)
        xc = acc - mu
        var = jnp.mean(xc * xc, axis=-1, keepdims=True)
        return xc * lax.rsqrt(var + LN_EPS) * g_ref[...] + b_ref[...]

    h1 = layer_norm(ALPHA * x_ref[0] + jnp.dot(mix_ref[0].astype(BF16), wo_ref[...], preferred_element_type=F32),
                    g1_ref, b1_ref)
    q = jnp.dot(h1.astype(BF16), wq_ref[...], preferred_element_type=F32).astype(BF16)
    scale = MEM_HEAD_DIM ** -0.5
    cols = [slice(h * MEM_HEAD_DIM, (h + 1) * MEM_HEAD_DIM) for h in range(MEM_HEADS)]
    s = [_dot_nt(q[:, c], k_ref[0, :, c]) * scale for c in cols]
    p = []
    for sh in s:
        e = jnp.exp(sh - jnp.max(sh, axis=-1, keepdims=True))
        p.append(e / jnp.sum(e, axis=-1, keepdims=True))
    att = jnp.concatenate([_dot(ph, v_ref[0, :, c]).astype(BF16) for ph, c in zip(p, cols)], axis=1)
    o_ref[0] = layer_norm(ALPHA * h1 + jnp.dot(att, wm_ref[...], preferred_element_type=F32), g2_ref, b2_ref)


def _attn_block(mix3, x3, mk, mv, w_out, w_mq, w_mo, ln1_g, ln1_b, ln2_g, ln2_b, *, tm):
    bsz, t, d = x3.shape
    tm = min(tm, t)
    rows = pl.BlockSpec((1, tm, d), lambda i, r: (i, r, 0))
    kv = pl.BlockSpec((1, N_MEM, d), lambda i, r: (i, 0, 0))
    wsp = pl.BlockSpec((d, d), lambda i, r: (0, 0))
    vec = pl.BlockSpec((1, d), lambda i, r: (0, 0))
    return pl.pallas_call(
        _attn_block_kernel,
        grid=(bsz, t // tm),
        in_specs=[rows, rows, kv, kv, wsp, wsp, wsp, vec, vec, vec, vec],
        out_specs=rows,
        out_shape=jax.ShapeDtypeStruct((bsz, t, d), F32),
        compiler_params=_params(2),
        name="attn_block",
    )(mix3, x3, mk, mv, w_out, w_mq, w_mo, ln1_g.reshape(1, d), ln1_b.reshape(1, d),
      ln2_g.reshape(1, d), ln2_b.reshape(1, d))


def _ffn_up_kernel(*refs, tm, grp, row_lo, has_hist):
    if has_hist:
        h_ref, wg_ref, wv_ref, cw_ref, cb_ref, hist_ref, act_ref, buf_ref, xp_ref = refs
    else:
        h_ref, wg_ref, wv_ref, cw_ref, cb_ref, act_ref, buf_ref, xp_ref = refs
        hist_ref = None
    n_hist = FFN_CONV - 1
    r = pl.program_id(2)

    @pl.when(r == 0)
    def _():
        xp_ref[0:SUBLANES, :] = jnp.zeros((SUBLANES, xp_ref.shape[1]), F32)

    hb = h_ref[0].astype(BF16)
    gate = jnp.dot(hb, wg_ref[...], preferred_element_type=F32)
    val = jnp.dot(hb, wv_ref[...], preferred_element_type=F32)
    if hist_ref is not None:
        pos = lax.broadcasted_iota(jnp.int32, (tm, 1), 0) % grp
        gate = jnp.where((pos >= row_lo - n_hist) & (pos < row_lo), hist_ref[0], gate)
    xp_ref[SUBLANES:SUBLANES + tm, :] = gate
    y = gate * cw_ref[n_hist:n_hist + 1, :] + cb_ref[...]
    for tap in range(n_hist):
        off = SUBLANES - n_hist + tap
        y = y + xp_ref[off:off + tm, :] * cw_ref[tap:tap + 1, :]
    gelu = 0.5 * y * (1.0 + lax.erf(y * (2.0 ** -0.5)))
    act_ref[0] = (gelu * val).astype(act_ref.dtype)
    if buf_ref.shape[1] == n_hist:
        buf_ref[0] = xp_ref[SUBLANES + tm - n_hist:SUBLANES + tm, :]
    else:
        buf_ref[0] = gate
    xp_ref[0:SUBLANES, :] = xp_ref[tm:tm + SUBLANES, :]


def _ffn_up(h3, w_gate, w_val, conv_w, conv_b, hist, *, tm, tn, grp, row_lo):
    g, r, d = h3.shape
    f = w_gate.shape[1]
    tm = min(tm, r)
    n_hist = FFN_CONV - 1
    has_hist = hist is not None
    assert grp == r or grp <= tm
    if grp == r:
        buf_spec = pl.BlockSpec((1, n_hist, tn), lambda gi, fi, ri: (gi, 0, fi))
        buf_rows = n_hist
    else:
        buf_spec = pl.BlockSpec((1, tm, tn), lambda gi, fi, ri: (gi, ri, fi))
        buf_rows = r
    w_spec = pl.BlockSpec((d, tn), lambda gi, fi, ri: (0, fi))
    in_specs = [pl.BlockSpec((1, tm, d), lambda gi, fi, ri: (gi, ri, 0)), w_spec, w_spec,
                pl.BlockSpec((FFN_CONV, tn), lambda gi, fi, ri: (0, fi)),
                pl.BlockSpec((1, tn), lambda gi, fi, ri: (0, fi))]
    args = [h3, w_gate, w_val, conv_w, conv_b.reshape(1, f)]
    if has_hist:
        in_specs.append(pl.BlockSpec((1, tm, tn), lambda gi, fi, ri: (gi, ri, fi)))
        args.append(hist)
    return pl.pallas_call(
        functools.partial(_ffn_up_kernel, tm=tm, grp=grp, row_lo=row_lo, has_hist=has_hist),
        grid=(g, f // tn, r // tm),
        in_specs=in_specs,
        out_specs=[pl.BlockSpec((1, tm, tn), lambda gi, fi, ri: (gi, ri, fi)),
                   buf_spec],
        out_shape=[jax.ShapeDtypeStruct((g, r, f), BF16),
                   jax.ShapeDtypeStruct((g, buf_rows, f), F32)],
        scratch_shapes=[pltpu.VMEM((SUBLANES + tm, tn), F32)],
        compiler_params=_params(3),
        name="ffn_up",
    )(*args)


def _ffn_block_kernel(h_ref, wg_ref, wv_ref, wd_ref, cw_ref, cb_ref, g_ref, b_ref, y_ref, buf_ref, xp_ref,
                      *, tm, n_split):
    n_hist = FFN_CONV - 1
    f = xp_ref.shape[1]
    fs = f // n_split

    @pl.when(pl.program_id(1) == 0)
    def _():
        xp_ref[0:SUBLANES, :] = jnp.zeros((SUBLANES, f), F32)

    h = h_ref[0]
    hb = h.astype(BF16)
    cols = [slice(j * fs, (j + 1) * fs) for j in range(n_split)]
    gate = [jnp.dot(hb, wg_ref[:, c], preferred_element_type=F32) for c in cols]
    val = [jnp.dot(hb, wv_ref[:, c], preferred_element_type=F32) for c in cols]
    act = []
    for c, gj, vj in zip(cols, gate, val):
        xp_ref[SUBLANES:SUBLANES + tm, c] = gj
        y = gj * cw_ref[n_hist:n_hist + 1, c] + cb_ref[:, c]
        for tap in range(n_hist):
            off = SUBLANES - n_hist + tap
            y = y + xp_ref[off:off + tm, c] * cw_ref[tap:tap + 1, c]
        gelu = 0.5 * y * (1.0 + lax.erf(y * (2.0 ** -0.5)))
        act.append((gelu * vj).astype(BF16))
    acc = ALPHA * h
    for c, aj in zip(cols, act):
        acc = acc + jnp.dot(aj, wd_ref[c, :], preferred_element_type=F32)
    mu = jnp.mean(acc, axis=-1, keepdims=True)
    xc = acc - mu
    var = jnp.mean(xc * xc, axis=-1, keepdims=True)
    y_ref[0] = xc * lax.rsqrt(var + LN_EPS) * g_ref[...] + b_ref[...]
    buf_ref[0] = xp_ref[SUBLANES + tm - n_hist:SUBLANES + tm, :]
    xp_ref[0:SUBLANES, :] = xp_ref[tm:tm + SUBLANES, :]


def _ffn_block(h3, w_gate, w_val, w_down, conv_w, conv_b, ln_g, ln_b, *, tm, n_split):
    bsz, t, d = h3.shape
    f = w_gate.shape[1]
    tm = min(tm, t)
    n_hist = FFN_CONV - 1
    once = pl.Buffered(1)
    const = lambda shape: pl.BlockSpec(shape, lambda i, r: (0,) * len(shape), pipeline_mode=once)
    rows = pl.BlockSpec((1, tm, d), lambda i, r: (i, r, 0))
    return pl.pallas_call(
        functools.partial(_ffn_block_kernel, tm=tm, n_split=n_split),
        grid=(bsz, t // tm),
        in_specs=[rows, const((d, f)), const((d, f)), const((f, d)), const((FFN_CONV, f)), const((1, f)),
                  const((1, d)), const((1, d))],
        out_specs=[rows, pl.BlockSpec((1, n_hist, f), lambda i, r: (i, 0, 0))],
        out_shape=[jax.ShapeDtypeStruct((bsz, t, d), F32), jax.ShapeDtypeStruct((bsz, n_hist, f), F32)],
        scratch_shapes=[pltpu.VMEM((SUBLANES + tm, f), F32)],
        compiler_params=_params(2),
        name="ffn_block",
    )(h3, w_gate, w_val, w_down, conv_w, conv_b.reshape(1, f), ln_g.reshape(1, d), ln_b.reshape(1, d))


def _layer(x3, s_hg, s_gd, hist_gd, hist_ffn, mk, mv, weights, *, row_lo, grp, ffn_streams, c_mix, nc_mix,
           nb_mix, fuse_proj, nb_attn):
    (mixer_w, w_out, ln1_g, ln1_b, w_mq, w_mo, ln2_g, ln2_b,
     w_gate, w_val, w_ffn_conv, b_ffn_conv, w_down, ln3_g, ln3_b) = weights
    bsz, t, d = x3.shape
    rows = bsz * t
    x2 = x3.reshape(rows, d)

    if fuse_proj:
        proj = tail = None
    else:
        proj = _matmul(x2, mixer_w[1], 1024, 512).reshape(bsz, t, -1)
        tail = _matmul(x2, mixer_w[2], 1024, HEAD_DIM).reshape(bsz, t, HEAD_DIM)
    mix, new_hg, new_gd, new_buf_gd = _mixer(x3, proj, tail, mixer_w, hist_gd, s_hg, s_gd,
                                             c=c_mix, nc=nc_mix, nb=nb_mix, row_lo=row_lo)

    if nb_attn is None:
        h2 = _attn_block(mix, x3, mk, mv, w_out, w_mq, w_mo, ln1_g, ln1_b, ln2_g, ln2_b, tm=512).reshape(rows, d)
    else:
        h1 = _matmul_res_ln([mix.reshape(rows, d)], [w_out], x2, ln1_g, ln1_b, 512)
        q = _matmul(h1, w_mq, 1024, 1024)
        att = _mem_attn_core(q.reshape(bsz, t, d), mk, mv, tq=512, nb=nb_attn)
        h2 = _matmul_res_ln([att.reshape(rows, d)], [w_mo], h1, ln2_g, ln2_b, 512)

    if hist_ffn is None and grp == t:
        y, new_buf_ffn = _ffn_block(h2.reshape(bsz, t, d), w_gate, w_val, w_down, w_ffn_conv, b_ffn_conv,
                                    ln3_g, ln3_b, tm=512, n_split=2)
        return y, new_hg, new_gd, new_buf_gd, new_buf_ffn
    h2_3 = h2.reshape(ffn_streams, rows // ffn_streams, d)
    hist3 = None if hist_ffn is None else hist_ffn.reshape(ffn_streams, rows // ffn_streams, D_FF)
    act, new_buf_ffn = _ffn_up(h2_3, w_gate, w_val, w_ffn_conv, b_ffn_conv, hist3,
                               tm=512, tn=D_FF // 2, grp=grp, row_lo=row_lo)
    y = _matmul_res_ln([act.reshape(rows, D_FF)], [w_down], h2, ln3_g, ln3_b, 512)
    return y.reshape(bsz, t, d), new_hg, new_gd, new_buf_gd, new_buf_ffn


def _cache_rows(cache):
    b, m, h, d = cache.shape
    return cache.reshape(b, m, h, d // LANES, LANES).transpose(0, 1, 3, 2, 4).reshape(b, m * h * d // LANES, LANES)


def kernel(x_prompt, x_sample, state_hgrn, state_gdn, state_gdn_conv, state_ffn_conv, cache_mem_k, cache_mem_v, mem_prompt, hgrn_lb_logits, w_in, w_gd_conv, gd_a_log, gd_dt_bias, hg_norm_g, gd_norm_g, w_out, ln1_g, ln1_b, w_mq, w_mkv, w_mo, ln2_g, ln2_b, w_up, w_ffn_conv, b_ffn_conv, w_down, ln3_g, ln3_b):
    bp, tp, d = x_prompt.shape
    bs, ts, _ = x_sample.shape
    tpad = SUBLANES
    row_lo = tpad - ts
    l = 0

    main_cols = 8 * MIX_HALF
    w_in_tail = jnp.pad(w_in[l][:, main_cols:], ((0, 0), (0, HEAD_DIM - 2 * N_HEADS))).astype(BF16)
    mixer_w = (hgrn_lb_logits, w_in[l][:, :main_cols].astype(BF16), w_in_tail, w_gd_conv[l],
               gd_a_log[l], gd_dt_bias[l], hg_norm_g[l], gd_norm_g[l])
    weights = (mixer_w, w_out[l].astype(BF16), ln1_g[l], ln1_b[l],
               w_mq[l].astype(BF16), w_mo[l].astype(BF16), ln2_g[l], ln2_b[l],
               w_up[l][:, :D_FF].astype(BF16), w_up[l][:, D_FF:].astype(BF16),
               w_ffn_conv[l], b_ffn_conv[l], w_down[l].astype(BF16), ln3_g[l], ln3_b[l])

    mem2 = mem_prompt.reshape(bp * N_MEM, d)
    mk = _matmul(mem2, w_mkv[l][:, :d].astype(BF16), 1024, 1024).reshape(bp, N_MEM, d)
    mv = _matmul(mem2, w_mkv[l][:, d:].astype(BF16), 1024, 1024).reshape(bp, N_MEM, d)
    yp, p_hg, p_gd, p_bgd, p_bff = _layer(x_prompt, None, None, None, None, mk, mv, weights,
                                          row_lo=0, grp=tp, ffn_streams=bp, c_mix=128, nc_mix=4, nb_mix=1,
                                          fuse_proj=True, nb_attn=None)

    n_hg = GD_CONV - 1
    n_hf = FFN_CONV - 1
    xs = jnp.pad(x_sample, ((0, 0), (row_lo, 0), (0, 0)))
    hist_gd = jnp.pad(state_gdn_conv[l], ((0, 0), (row_lo - n_hg, ts), (0, 0)))
    hist_ffn = jnp.pad(state_ffn_conv[l], ((0, 0), (row_lo - n_hf, ts), (0, 0)))
    ys, s_hg, s_gd, s_bgd, s_bff = _layer(
        xs, state_hgrn[l], state_gdn[l], hist_gd, hist_ffn,
        _cache_rows(cache_mem_k[l]), _cache_rows(cache_mem_v[l]), weights,
        row_lo=row_lo, grp=tpad, ffn_streams=1, c_mix=tpad, nc_mix=1, nb_mix=8, fuse_proj=False,
        nb_attn=4)
    ys = ys[:, row_lo:]
    s_bff = s_bff.reshape(bs, tpad, D_FF)[:, tpad - n_hf:]

    mem_shape = (1, bp, N_MEM, MEM_HEADS, MEM_HEAD_DIM)
    return (yp, ys, p_hg[None], p_gd[None], p_bgd[None], p_bff[None],
            mk.reshape(mem_shape), mv.reshape(mem_shape),
            s_hg[None], s_gd[None], s_bgd[None], s_bff[None])
```

```python
import functools

import jax
import jax.numpy as jnp
from jax import lax
from jax.experimental import pallas as pl
from jax.experimental.pallas import tpu as pltpu

F32 = jnp.float32
BF16 = jnp.bfloat16

D_MODEL = 1024
HEAD_DIM = 128
N_HEADS = 4
MIX_HALF = N_HEADS * HEAD_DIM
GD_CONV = 4
FFN_CONV = 3
D_FF = 2816
N_MEM = 256
MEM_HEADS = 4
MEM_HEAD_DIM = D_MODEL // MEM_HEADS
LN_EPS = 1e-5
RMS_EPS = 1e-6
LOG2_E = 1.4426950408889634
DEPTH = 1
ALPHA = (2.0 * DEPTH) ** 0.25

SUBLANES = 8
LANES = 128
VMEM_LIMIT = 56 * 1024 * 1024


def _params(n_axes):
    return pltpu.CompilerParams(dimension_semantics=("arbitrary",) * n_axes,
                                vmem_limit_bytes=VMEM_LIMIT)


def _dot(a, b):
    return jnp.dot(a.astype(BF16), b.astype(BF16), preferred_element_type=F32)


def _dot_nt(a, b):
    return lax.dot_general(a.astype(BF16), b.astype(BF16), (((1,), (1,)), ((), ())),
                           preferred_element_type=F32)


def _dot_tn(a, b):
    return lax.dot_general(a.astype(BF16), b.astype(BF16), (((0,), (0,)), ((), ())),
                           preferred_element_type=F32)


def _split3(x):
    x1 = x.astype(BF16)
    r = x - x1.astype(F32)
    x2 = r.astype(BF16)
    x3 = (r - x2.astype(F32)).astype(BF16)
    return x1, x2, x3


def _cumsum_rows(tri_bf16, x):
    x1, x2, x3 = _split3(x)
    return (jnp.dot(tri_bf16, x1, preferred_element_type=F32)
            + jnp.dot(tri_bf16, x2, preferred_element_type=F32)
            + jnp.dot(tri_bf16, x3, preferred_element_type=F32))


def _multi_dot(a, b, keys, nt=False):
    dot = _dot_nt if nt else _dot
    return {k: dot(a[k], b[k]) for k in keys}


def _silu(x):
    return x * jax.nn.sigmoid(x)


def _mm_kernel(x_ref, w_ref, o_ref, xb_ref):
    @pl.when(pl.program_id(1) == 0)
    def _():
        xb_ref[...] = x_ref[...].astype(BF16)

    o_ref[...] = jnp.dot(xb_ref[...], w_ref[...], preferred_element_type=F32).astype(o_ref.dtype)


def _matmul(x, w, tm, tn):
    m, k = x.shape
    n = w.shape[1]
    tm = min(tm, m)
    tn = min(tn, n)
    return pl.pallas_call(
        _mm_kernel,
        grid=(m // tm, n // tn),
        in_specs=[pl.BlockSpec((tm, k), lambda i, j: (i, 0)),
                  pl.BlockSpec((k, tn), lambda i, j: (0, j))],
        out_specs=pl.BlockSpec((tm, tn), lambda i, j: (i, j)),
        out_shape=jax.ShapeDtypeStruct((m, n), F32),
        scratch_shapes=[pltpu.VMEM((tm, k), BF16)],
        compiler_params=_params(2),
        name="proj_matmul",
    )(x, w)


ROWS_PER_MEM = MEM_HEADS * MEM_HEAD_DIM // LANES


def _head_rows(h, j):
    return pl.ds(h + MEM_HEADS * j, N_MEM, stride=ROWS_PER_MEM)


def _head_block(ref, b, h):
    return jnp.concatenate([ref[b, _head_rows(h, j), :] for j in range(MEM_HEAD_DIM // LANES)], axis=1)


def _kv_proj_kernel(x_ref, w_ref, o_ref):
    res = jnp.dot(x_ref[0].astype(BF16), w_ref[...], preferred_element_type=F32)
    for h in range(MEM_HEADS):
        for j in range(MEM_HEAD_DIM // LANES):
            c0 = h * MEM_HEAD_DIM + j * LANES
            o_ref[0, _head_rows(h, j), :] = res[:, c0:c0 + LANES]


def _kv_proj(mem3, w):
    bsz, n_mem, d = mem3.shape
    return pl.pallas_call(
        _kv_proj_kernel,
        grid=(bsz,),
        in_specs=[pl.BlockSpec((1, n_mem, d), lambda i: (i, 0, 0)), pl.BlockSpec(w.shape, lambda i: (0, 0))],
        out_specs=pl.BlockSpec((1, n_mem * ROWS_PER_MEM, LANES), lambda i: (i, 0, 0)),
        out_shape=jax.ShapeDtypeStruct((bsz, n_mem * ROWS_PER_MEM, LANES), F32),
        compiler_params=_params(1),
        name="kv_proj",
    )(mem3, w)


def _mm_ln_kernel(*refs, n_in):
    a_refs = refs[:n_in]
    w_refs = refs[n_in:2 * n_in]
    res_ref, g_ref, b_ref, o_ref = refs[2 * n_in:]
    acc = ALPHA * res_ref[...]
    for a_ref, w_ref in zip(a_refs, w_refs):
        acc = acc + jnp.dot(a_ref[...].astype(BF16), w_ref[...], preferred_element_type=F32)
    mu = jnp.mean(acc, axis=-1, keepdims=True)
    xc = acc - mu
    var = jnp.mean(xc * xc, axis=-1, keepdims=True)
    o_ref[...] = xc * lax.rsqrt(var + LN_EPS) * g_ref[...] + b_ref[...]


def _matmul_res_ln(a_list, w_list, res, g, b, tm):
    m, d = res.shape
    tm = min(tm, m)
    n_in = len(a_list)
    in_specs = ([pl.BlockSpec((tm, a.shape[1]), lambda i: (i, 0)) for a in a_list]
                + [pl.BlockSpec(w.shape, lambda i: (0, 0)) for w in w_list]
                + [pl.BlockSpec((tm, d), lambda i: (i, 0)),
                   pl.BlockSpec((1, d), lambda i: (0, 0)),
                   pl.BlockSpec((1, d), lambda i: (0, 0))])
    return pl.pallas_call(
        functools.partial(_mm_ln_kernel, n_in=n_in),
        grid=(m // tm,),
        in_specs=in_specs,
        out_specs=pl.BlockSpec((tm, d), lambda i: (i, 0)),
        out_shape=jax.ShapeDtypeStruct((m, d), F32),
        compiler_params=_params(1),
        name="proj_res_ln",
    )(*a_list, *w_list, res, g.reshape(1, d), b.reshape(1, d))


HG_COL = 0
GD_COL = 4 * MIX_HALF
HGRN_GROUP = 1
GDN_GROUP = 4


def _ref_rows(gc_ref, b, r0, col, m, c):
    blk = 2 * m
    if blk >= SUBLANES:
        parts = [jnp.broadcast_to(gc_ref[b, pl.ds(r0 + j * blk + m - 1, 1), col], (blk, HEAD_DIM))
                 for j in range(c // blk)]
    else:
        sub = lax.broadcasted_iota(jnp.int32, (SUBLANES, HEAD_DIM), 0)
        parts = []
        for i in range(c // SUBLANES):
            tile = None
            for j in range(SUBLANES // blk):
                row = jnp.broadcast_to(gc_ref[b, pl.ds(r0 + i * SUBLANES + j * blk + m - 1, 1), col],
                                       (SUBLANES, HEAD_DIM))
                tile = row if tile is None else jnp.where(sub >= j * blk, row, tile)
            parts.append(tile)
    return parts[0] if len(parts) == 1 else jnp.concatenate(parts, axis=0)


def _hgrn_rows(p_ref, mix_ref, lb, ng, st_ref, gc_ref, *, chunk_ids, c, nb, row_lo):
    rows = lax.broadcasted_iota(jnp.int32, (c, 1), 0)
    ti = lax.broadcasted_iota(jnp.int32, (c, c), 0)
    si = lax.broadcasted_iota(jnp.int32, (c, c), 1)
    tri = (si <= ti).astype(F32).astype(BF16)
    eye = ti == si
    xr = ti ^ si
    cols = [slice(h * HEAD_DIM, (h + 1) * HEAD_DIM) for h in range(N_HEADS)]
    chains = [(b, j, h) for b in range(nb) for j in chunk_ids for h in range(N_HEADS)]

    def proj(b, j, part):
        return p_ref[b, pl.ds(j * c, c), HG_COL + part * MIX_HALF:HG_COL + (part + 1) * MIX_HALF]

    q, k, gc = {}, {}, {}
    for b in range(nb):
        for j in chunk_ids:
            f = lb + (1.0 - lb) * jax.nn.sigmoid(proj(b, j, 1))
            g = jnp.log(f)
            kb = 1.0 - f
            if row_lo:
                g = jnp.where(rows >= row_lo, g, 0.0)
                kb = jnp.where(rows >= row_lo, kb, 0.0)
            qb = _silu(proj(b, j, 0))
            gcb = _cumsum_rows(tri, g)
            gc_ref[b, pl.ds(j * c, c), :] = gcb
            for h in range(N_HEADS):
                q[b, j, h], k[b, j, h], gc[b, j, h] = qb[:, cols[h]], kb[:, cols[h]], gcb[:, cols[h]]

    level_of = jnp.where(eye, 0, -1)
    m = 1
    while m < c:
        level_of = jnp.where((xr >= m) & (xr < 2 * m) & (si < ti), m, level_of)
        m *= 2
    rows_w = lax.broadcasted_iota(jnp.int32, (c, HEAD_DIM), 0)
    sc = _multi_dot(q, k, chains, nt=True)
    sc = {ch: jnp.where(level_of == 0, sc[ch], 0.0) for ch in chains}
    m = c // 2
    while m >= 1:
        sign = jnp.where((rows_w & m) != 0, LOG2_E, -LOG2_E)
        qw, kw = {}, {}
        for ch in chains:
            b, j, h = ch
            w = jnp.exp2((gc[ch] - _ref_rows(gc_ref, b, j * c, cols[h], m, c)) * sign)
            qw[ch], kw[ch] = q[ch] * w, k[ch] * w
        prod = _multi_dot(qw, kw, chains, nt=True)
        sc = {ch: jnp.where(level_of == m, prod[ch], sc[ch]) for ch in chains}
        m //= 2

    v = {(b, j, h): p_ref[b, pl.ds(j * c, c),
                          HG_COL + 2 * MIX_HALF + h * HEAD_DIM:HG_COL + 2 * MIX_HALF + (h + 1) * HEAD_DIM]
         for b, j, h in chains}
    o_intra = _multi_dot(sc, v, chains)

    for j in chunk_ids:
        group = [(b, j, h) for b in range(nb) for h in range(N_HEADS)]
        st = {ch: st_ref[ch[0], ch[2]] for ch in group}
        o_inter = _multi_dot({ch: q[ch] * jnp.exp(gc[ch]) for ch in group}, st, group, nt=True)
        for ch in group:
            b, _, h = ch
            g_last = gc_ref[b, pl.ds(j * c + c - 1, 1), cols[h]]
            st_ref[b, h] = st[ch] * jnp.exp(g_last) + _dot_tn(v[ch], k[ch] * jnp.exp(g_last - gc[ch]))
        for ch in group:
            b, _, h = ch
            ob = o_intra[ch] + o_inter[ch]
            on = ob * lax.rsqrt(jnp.mean(ob * ob, axis=-1, keepdims=True) + RMS_EPS) * ng
            gate = p_ref[b, pl.ds(j * c, c),
                         HG_COL + 3 * MIX_HALF + h * HEAD_DIM:HG_COL + 3 * MIX_HALF + (h + 1) * HEAD_DIM]
            mix_ref[b, pl.ds(j * c, c), cols[h]] = on * _silu(gate)


def _gdn_rows(p_ref, t_ref, hist_ref, mix_ref, cb_ref, cw_ref, neg_a, dt_bias, ng, s_ref, xp_ref,
              *, c, nc, nb, row_lo):
    n_hist = GD_CONV - 1
    rb = c * nc
    rows = lax.broadcasted_iota(jnp.int32, (rb, 1), 0) % c
    ti = lax.broadcasted_iota(jnp.int32, (c, c), 0)
    si = lax.broadcasted_iota(jnp.int32, (c, c), 1)
    tri = (si <= ti).astype(F32).astype(BF16)
    eye_f = (ti == si).astype(F32)
    incl = si <= ti

    q_all, k_all, v_all, beta_all, la_all = {}, {}, {}, {}, {}
    for b in range(nb):
        qkv = []
        for j in range(3):
            col = slice(j * MIX_HALF, (j + 1) * MIX_HALF)
            x = p_ref[b, :, GD_COL + j * MIX_HALF:GD_COL + (j + 1) * MIX_HALF]
            if hist_ref is not None:
                x = jnp.where((rows >= row_lo - n_hist) & (rows < row_lo), hist_ref[b, :, col], x)
            xp_ref[b, SUBLANES:SUBLANES + rb, col] = x
            y = x * cw_ref[n_hist:n_hist + 1, col]
            for tap in range(n_hist):
                off = SUBLANES - n_hist + tap
                y = y + xp_ref[b, off:off + rb, col] * cw_ref[tap:tap + 1, col]
            qkv.append(_silu(y))
        cb_ref[b] = xp_ref[b, SUBLANES + rb - n_hist:SUBLANES + rb, :]
        xp_ref[b, 0:SUBLANES, :] = xp_ref[b, rb:rb + SUBLANES, :]
        q_all[b], k_all[b], v_all[b] = qkv

        tail = t_ref[b]
        beta_all[b] = jax.nn.sigmoid(tail)
        la_all[b] = neg_a * jax.nn.softplus(tail + dt_bias)
        if row_lo:
            beta_all[b] = jnp.where(rows >= row_lo, beta_all[b], 0.0)
            la_all[b] = jnp.where(rows >= row_lo, la_all[b], 0.0)

    for j0 in range(0, nc, GDN_GROUP):
        _gdn_chunks(list(range(j0, min(nc, j0 + GDN_GROUP))), q_all, k_all, v_all, beta_all, la_all,
                    p_ref, mix_ref, ng, s_ref, c=c, nb=nb)


def _gdn_chunks(chunk_ids, q_all, k_all, v_all, beta_all, la_all, p_ref, mix_ref, ng, s_ref, *, c, nb):
    ti = lax.broadcasted_iota(jnp.int32, (c, c), 0)
    si = lax.broadcasted_iota(jnp.int32, (c, c), 1)
    tri = (si <= ti).astype(F32).astype(BF16)
    eye_f = (ti == si).astype(F32)
    incl = si <= ti
    chains = [(b, j, h) for b in range(nb) for j in chunk_ids for h in range(N_HEADS)]

    gc = {(b, j): _cumsum_rows(tri, la_all[b][j * c:(j + 1) * c]) for b in range(nb) for j in chunk_ids}

    pre = {}
    for ch in chains:
        b, j, h = ch
        rs = slice(j * c, (j + 1) * c)
        col = slice(h * HEAD_DIM, (h + 1) * HEAD_DIM)
        qh, kh, vh = q_all[b][rs, col], k_all[b][rs, col], v_all[b][rs, col]
        qh = qh * lax.rsqrt(jnp.sum(qh * qh, axis=-1, keepdims=True) + RMS_EPS) * (HEAD_DIM ** -0.5)
        kh = kh * lax.rsqrt(jnp.sum(kh * kh, axis=-1, keepdims=True) + RMS_EPS)
        beta = beta_all[b][rs, h:h + 1]
        gcol = gc[b, j][:, N_HEADS + h:N_HEADS + h + 1]
        grow = jnp.sum(eye_f * gcol, axis=0, keepdims=True)
        decay = jnp.where(incl, jnp.exp(jnp.where(incl, gcol - grow, 0.0)), 0.0)
        gamma = jnp.exp(gcol)
        g_last = gcol[c - 1:c, :]
        pre[ch] = dict(
            q=qh, k=kh, beta=beta, decay=decay,
            rhs=jnp.concatenate([(beta * gamma) * kh, beta * vh], axis=1).astype(BF16),
            gq=(gamma * qh).astype(BF16),
            kd=(kh * jnp.exp(g_last - gcol)).astype(BF16),
            eg=jnp.exp(g_last))
    k_b = {ch: pre[ch]["k"].astype(BF16) for ch in chains}
    kk = _multi_dot(k_b, k_b, chains, nt=True)
    qk = _multi_dot({ch: pre[ch]["q"] for ch in chains}, k_b, chains, nt=True)
    a_mat = {ch: pre[ch]["beta"] * kk[ch] * pre[ch]["decay"] for ch in chains}
    aqk = {ch: (qk[ch] * pre[ch]["decay"]).astype(BF16) for ch in chains}

    xr = ti ^ si
    first = (xr < 2) & (si < ti)
    t_inv = {ch: eye_f - jnp.where(first, a_mat[ch], 0.0) for ch in chains}
    n = 2
    while n < c:
        lower = (xr < 2 * n) & ((ti & n) != 0) & ((si & n) == 0)
        x = _multi_dot({ch: jnp.where(lower, a_mat[ch], 0.0) for ch in chains}, t_inv, chains)
        tx = _multi_dot(t_inv, x, chains)
        t_inv = {ch: t_inv[ch] - tx[ch] for ch in chains}
        n *= 2
    wu = {ch: _dot(t_inv[ch], pre[ch]["rhs"]) for ch in chains}

    for j in chunk_ids:
        rs = slice(j * c, (j + 1) * c)
        group = [(b, j, h) for b in range(nb) for h in range(N_HEADS)]
        s_old = {ch: s_ref[ch[0], ch[2]] for ch in group}
        ws = _multi_dot({ch: jnp.concatenate([wu[ch][:, :HEAD_DIM].astype(BF16), pre[ch]["gq"]], axis=0)
                         for ch in group}, s_old, group)
        u = {ch: (wu[ch][:, HEAD_DIM:] - ws[ch][:c]).astype(BF16) for ch in group}
        au = _multi_dot(aqk, u, group)
        for ch in group:
            b, _, h = ch
            s_ref[b, h] = pre[ch]["eg"] * s_old[ch] + _dot_tn(pre[ch]["kd"], u[ch])
            o = ws[ch][c:] + au[ch]
            on = o * lax.rsqrt(jnp.mean(o * o, axis=-1, keepdims=True) + RMS_EPS) * ng
            gate = p_ref[b, rs, GD_COL + 3 * MIX_HALF + h * HEAD_DIM:GD_COL + 3 * MIX_HALF + (h + 1) * HEAD_DIM]
            mix_ref[b, rs, MIX_HALF + h * HEAD_DIM:MIX_HALF + (h + 1) * HEAD_DIM] = on * _silu(gate)


def _mixer_kernel(*refs, c, nc, nb, fuse_proj, has_state, row_lo, n_steps):
    refs = list(refs)
    if fuse_proj:
        x_ref, w_ref, wt_ref = refs[:3]
        del refs[:3]
    else:
        p_ref, t_ref = refs[:2]
        del refs[:2]
    lbl_ref, hng_ref, cw_ref, par_ref, gng_ref = refs[:5]
    del refs[:5]
    if has_state:
        hist_ref, shg0_ref, sgd0_ref = refs[:3]
        del refs[:3]
    else:
        hist_ref = shg0_ref = sgd0_ref = None
    mix_ref, shg_ref, sgd_ref, cb_ref, st_ref, s_ref, gc_ref, xp_ref = refs[:8]
    if fuse_proj:
        p_ref, t_ref = refs[8:]
    step = pl.program_id(1)
    pairs = [(b, h) for b in range(nb) for h in range(N_HEADS)]

    @pl.when(step == 0)
    def _():
        for b, h in pairs:
            if has_state:
                st_ref[b, h] = shg0_ref[b, h].T
                s_ref[b, h] = sgd0_ref[b, h]
            else:
                st_ref[b, h] = jnp.zeros((HEAD_DIM, HEAD_DIM), F32)
                s_ref[b, h] = jnp.zeros((HEAD_DIM, HEAD_DIM), F32)
        for b in range(nb):
            xp_ref[b, 0:SUBLANES, :] = jnp.zeros((SUBLANES, 3 * MIX_HALF), F32)

    if fuse_proj:
        for b in range(nb):
            xb = x_ref[b].astype(BF16)
            p_ref[b] = jnp.dot(xb, w_ref[...], preferred_element_type=F32)
            t_ref[b] = jnp.dot(xb, wt_ref[...], preferred_element_type=F32)

    lbl = lbl_ref[...]
    e = jnp.exp(lbl - jnp.max(lbl, axis=0, keepdims=True))
    lb = e[0:1] / jnp.sum(e, axis=0, keepdims=True)
    neg_a = -jnp.exp(par_ref[0:1, :])
    dt_bias = par_ref[1:2, :]

    for j0 in range(0, nc, HGRN_GROUP):
        _hgrn_rows(p_ref, mix_ref, lb, hng_ref[...], st_ref, gc_ref,
                   chunk_ids=list(range(j0, min(nc, j0 + HGRN_GROUP))), c=c, nb=nb, row_lo=row_lo)
    _gdn_rows(p_ref, t_ref, hist_ref, mix_ref, cb_ref, cw_ref, neg_a, dt_bias, gng_ref[...], s_ref, xp_ref,
              c=c, nc=nc, nb=nb, row_lo=row_lo)

    @pl.when(step == n_steps - 1)
    def _():
        for b, h in pairs:
            shg_ref[b, h] = st_ref[b, h].T
            sgd_ref[b, h] = s_ref[b, h]


def _mixer(x3, proj, tail, weights, hist, s_hg, s_gd, *, c, nc, nb, row_lo):
    lb_logits, w_main, w_tail, conv_w, a_log, dt_bias, hg_norm_g, gd_norm_g = weights
    fuse_proj = proj is None
    bsz, t = (x3 if fuse_proj else proj).shape[:2]
    rb = c * nc
    n_steps = t // rb
    has_state = s_hg is not None
    n_hist = GD_CONV - 1
    n_proj = w_main.shape[1]
    const = lambda shape: pl.BlockSpec(shape, lambda i, si: (0,) * len(shape))
    rows3 = lambda width: pl.BlockSpec((nb, rb, width), lambda i, si: (i, si, 0))
    st_spec = pl.BlockSpec((nb, N_HEADS, HEAD_DIM, HEAD_DIM), lambda i, si: (i, 0, 0, 0))
    par = jnp.zeros((SUBLANES, HEAD_DIM), F32)
    par = par.at[0, N_HEADS:2 * N_HEADS].set(a_log).at[1, N_HEADS:2 * N_HEADS].set(dt_bias)
    if fuse_proj:
        in_specs = [rows3(D_MODEL), const(w_main.shape), const(w_tail.shape)]
        args = [x3, w_main, w_tail]
    else:
        in_specs = [rows3(n_proj), rows3(HEAD_DIM)]
        args = [proj, tail]
    in_specs += [const(lb_logits.shape), const((1, HEAD_DIM)), const(conv_w.shape), const(par.shape),
                 const((1, HEAD_DIM))]
    args += [lb_logits, hg_norm_g.reshape(1, HEAD_DIM), conv_w, par, gd_norm_g.reshape(1, HEAD_DIM)]
    if has_state:
        in_specs += [rows3(3 * MIX_HALF), st_spec, st_spec]
        args += [hist, s_hg, s_gd]
    scratch = [pltpu.VMEM((nb, N_HEADS, HEAD_DIM, HEAD_DIM), F32),
               pltpu.VMEM((nb, N_HEADS, HEAD_DIM, HEAD_DIM), F32),
               pltpu.VMEM((nb, rb, MIX_HALF), F32),
               pltpu.VMEM((nb, SUBLANES + rb, 3 * MIX_HALF), F32)]
    if fuse_proj:
        scratch += [pltpu.VMEM((nb, rb, n_proj), F32), pltpu.VMEM((nb, rb, HEAD_DIM), F32)]
    return pl.pallas_call(
        functools.partial(_mixer_kernel, c=c, nc=nc, nb=nb, fuse_proj=fuse_proj, has_state=has_state,
                          row_lo=row_lo, n_steps=n_steps),
        grid=(bsz // nb, n_steps),
        in_specs=in_specs,
        out_specs=[rows3(D_MODEL), st_spec, st_spec,
                   pl.BlockSpec((nb, n_hist, 3 * MIX_HALF), lambda i, si: (i, 0, 0))],
        out_shape=[jax.ShapeDtypeStruct((bsz, t, D_MODEL), F32),
                   jax.ShapeDtypeStruct((bsz, N_HEADS, HEAD_DIM, HEAD_DIM), F32),
                   jax.ShapeDtypeStruct((bsz, N_HEADS, HEAD_DIM, HEAD_DIM), F32),
                   jax.ShapeDtypeStruct((bsz, n_hist, 3 * MIX_HALF), F32)],
        scratch_shapes=scratch,
        compiler_params=_params(2),
        name="mixer",
    )(*args)


def _attn_kernel(q_ref, k_ref, v_ref, o_ref, *, nb):
    scale = MEM_HEAD_DIM ** -0.5
    pairs = [(b, h) for b in range(nb) for h in range(MEM_HEADS)]
    s = {bh: _dot_nt(q_ref[bh[0], :, bh[1] * MEM_HEAD_DIM:(bh[1] + 1) * MEM_HEAD_DIM],
                     _head_block(k_ref, *bh)) * scale for bh in pairs}
    p = {}
    for bh in pairs:
        e = jnp.exp(s[bh] - jnp.max(s[bh], axis=-1, keepdims=True))
        p[bh] = e / jnp.sum(e, axis=-1, keepdims=True)
    for b, h in pairs:
        o_ref[b, :, h * MEM_HEAD_DIM:(h + 1) * MEM_HEAD_DIM] = _dot(p[b, h], _head_block(v_ref, b, h))


def _mem_attn_core(q3, mk, mv, *, tq, nb):
    bsz, t, d = q3.shape
    tq = min(tq, t)
    kv_spec = pl.BlockSpec((nb,) + mk.shape[1:], lambda i, r: (i, 0, 0))
    return pl.pallas_call(
        functools.partial(_attn_kernel, nb=nb),
        grid=(bsz // nb, t // tq),
        in_specs=[pl.BlockSpec((nb, tq, d), lambda i, r: (i, r, 0)), kv_spec, kv_spec],
        out_specs=pl.BlockSpec((nb, tq, d), lambda i, r: (i, r, 0)),
        out_shape=jax.ShapeDtypeStruct((bsz, t, d), F32),
        compiler_params=_params(2),
        name="mem_attn_core",
    )(q3, mk, mv)


def _attn_block_kernel(mix_ref, x_ref, k_ref, v_ref, wo_ref, wq_ref, wm_ref, g1_ref, b1_ref, g2_ref, b2_ref,
                       o_ref, *, n_sub):
    def layer_norm(acc, g_ref, b_ref):
        mu = jnp.mean(acc, axis=-1, keepdims=True)
        xc = acc - mu
        var = jnp.mean(xc * xc, axis=-1, keepdims=True)
        return xc * lax.rsqrt(var + LN_EPS) * g_ref[...] + b_ref[...]

    ts = x_ref.shape[1] // n_sub
    subs = [pl.ds(i * ts, ts) for i in range(n_sub)]
    scale = MEM_HEAD_DIM ** -0.5
    cols = [slice(h * MEM_HEAD_DIM, (h + 1) * MEM_HEAD_DIM) for h in range(MEM_HEADS)]
    heads = range(MEM_HEADS)
    kh = [_head_block(k_ref, 0, h).astype(BF16) for h in heads]
    vh = [_head_block(v_ref, 0, h).astype(BF16) for h in heads]
    h1 = [layer_norm(ALPHA * x_ref[0, r, :]
                     + jnp.dot(mix_ref[0, r, :].astype(BF16), wo_ref[...], preferred_element_type=F32),
                     g1_ref, b1_ref) for r in subs]
    q = [jnp.dot(hi.astype(BF16), wq_ref[...], preferred_element_type=F32).astype(BF16) for hi in h1]
    s = [[_dot_nt(qi[:, cols[h]], kh[h]) * scale for h in heads] for qi in q]
    p = []
    for si in s:
        pi = []
        for sh in si:
            e = jnp.exp(sh - jnp.max(sh, axis=-1, keepdims=True))
            pi.append(e / jnp.sum(e, axis=-1, keepdims=True))
        p.append(pi)
    att = [jnp.concatenate([_dot(pi[h], vh[h]).astype(BF16) for h in heads], axis=1) for pi in p]
    for r, hi, ai in zip(subs, h1, att):
        o_ref[0, r, :] = layer_norm(ALPHA * hi + jnp.dot(ai, wm_ref[...], preferred_element_type=F32),
                                    g2_ref, b2_ref)


def _attn_block(mix3, x3, mk, mv, w_out, w_mq, w_mo, ln1_g, ln1_b, ln2_g, ln2_b, *, tm, n_sub):
    bsz, t, d = x3.shape
    tm = min(tm, t)
    rows = pl.BlockSpec((1, tm, d), lambda i, r: (i, r, 0))
    kv = pl.BlockSpec((1,) + mk.shape[1:], lambda i, r: (i, 0, 0))
    wsp = pl.BlockSpec((d, d), lambda i, r: (0, 0))
    vec = pl.BlockSpec((1, d), lambda i, r: (0, 0))
    return pl.pallas_call(
        functools.partial(_attn_block_kernel, n_sub=n_sub),
        grid=(bsz, t // tm),
        in_specs=[rows, rows, kv, kv, wsp, wsp, wsp, vec, vec, vec, vec],
        out_specs=rows,
        out_shape=jax.ShapeDtypeStruct((bsz, t, d), F32),
        compiler_params=_params(2),
        name="attn_block",
    )(mix3, x3, mk, mv, w_out, w_mq, w_mo, ln1_g.reshape(1, d), ln1_b.reshape(1, d),
      ln2_g.reshape(1, d), ln2_b.reshape(1, d))


def _ffn_up_kernel(*refs, tm, grp, row_lo, has_hist):
    if has_hist:
        h_ref, wg_ref, wv_ref, cw_ref, cb_ref, hist_ref, act_ref, buf_ref, xp_ref = refs
    else:
        h_ref, wg_ref, wv_ref, cw_ref, cb_ref, act_ref, buf_ref, xp_ref = refs
        hist_ref = None
    n_hist = FFN_CONV - 1
    r = pl.program_id(2)

    @pl.when(r == 0)
    def _():
        xp_ref[0:SUBLANES, :] = jnp.zeros((SUBLANES, xp_ref.shape[1]), F32)

    hb = h_ref[0].astype(BF16)
    gate = jnp.dot(hb, wg_ref[...], preferred_element_type=F32)
    val = jnp.dot(hb, wv_ref[...], preferred_element_type=F32)
    if hist_ref is not None:
        pos = lax.broadcasted_iota(jnp.int32, (tm, 1), 0) % grp
        gate = jnp.where((pos >= row_lo - n_hist) & (pos < row_lo), hist_ref[0], gate)
    xp_ref[SUBLANES:SUBLANES + tm, :] = gate
    y = gate * cw_ref[n_hist:n_hist + 1, :] + cb_ref[...]
    for tap in range(n_hist):
        off = SUBLANES - n_hist + tap
        y = y + xp_ref[off:off + tm, :] * cw_ref[tap:tap + 1, :]
    gelu = 0.5 * y * (1.0 + lax.erf(y * (2.0 ** -0.5)))
    act_ref[0] = (gelu * val).astype(act_ref.dtype)
    if buf_ref.shape[1] == n_hist:
        buf_ref[0] = xp_ref[SUBLANES + tm - n_hist:SUBLANES + tm, :]
    else:
        buf_ref[0] = gate
    xp_ref[0:SUBLANES, :] = xp_ref[tm:tm + SUBLANES, :]


def _ffn_up(h3, w_gate, w_val, conv_w, conv_b, hist, *, tm, tn, grp, row_lo):
    g, r, d = h3.shape
    f = w_gate.shape[1]
    tm = min(tm, r)
    n_hist = FFN_CONV - 1
    has_hist = hist is not None
    assert grp == r or grp <= tm
    if grp == r:
        buf_spec = pl.BlockSpec((1, n_hist, tn), lambda gi, fi, ri: (gi, 0, fi))
        buf_rows = n_hist
    else:
        buf_spec = pl.BlockSpec((1, tm, tn), lambda gi, fi, ri: (gi, ri, fi))
        buf_rows = r
    w_spec = pl.BlockSpec((d, tn), lambda gi, fi, ri: (0, fi))
    in_specs = [pl.BlockSpec((1, tm, d), lambda gi, fi, ri: (gi, ri, 0)), w_spec, w_spec,
                pl.BlockSpec((FFN_CONV, tn), lambda gi, fi, ri: (0, fi)),
                pl.BlockSpec((1, tn), lambda gi, fi, ri: (0, fi))]
    args = [h3, w_gate, w_val, conv_w, conv_b.reshape(1, f)]
    if has_hist:
        in_specs.append(pl.BlockSpec((1, tm, tn), lambda gi, fi, ri: (gi, ri, fi)))
        args.append(hist)
    return pl.pallas_call(
        functools.partial(_ffn_up_kernel, tm=tm, grp=grp, row_lo=row_lo, has_hist=has_hist),
        grid=(g, f // tn, r // tm),
        in_specs=in_specs,
        out_specs=[pl.BlockSpec((1, tm, tn), lambda gi, fi, ri: (gi, ri, fi)),
                   buf_spec],
        out_shape=[jax.ShapeDtypeStruct((g, r, f), BF16),
                   jax.ShapeDtypeStruct((g, buf_rows, f), F32)],
        scratch_shapes=[pltpu.VMEM((SUBLANES + tm, tn), F32)],
        compiler_params=_params(3),
        name="ffn_up",
    )(*args)


def _ffn_block_kernel(h_ref, wg_ref, wv_ref, wd_ref, cw_ref, cb_ref, g_ref, b_ref, y_ref, buf_ref, xp_ref,
                      *, tm, n_split):
    n_hist = FFN_CONV - 1
    f = xp_ref.shape[1]
    fs = f // n_split

    @pl.when(pl.program_id(1) == 0)
    def _():
        xp_ref[0:SUBLANES, :] = jnp.zeros((SUBLANES, f), F32)

    h = h_ref[0]
    hb = h.astype(BF16)
    cols = [slice(j * fs, (j + 1) * fs) for j in range(n_split)]
    gate = [jnp.dot(hb, wg_ref[:, c], preferred_element_type=F32) for c in cols]
    val = [jnp.dot(hb, wv_ref[:, c], preferred_element_type=F32) for c in cols]
    act = []
    for c, gj, vj in zip(cols, gate, val):
        xp_ref[SUBLANES:SUBLANES + tm, c] = gj
        y = gj * cw_ref[n_hist:n_hist + 1, c] + cb_ref[:, c]
        for tap in range(n_hist):
            off = SUBLANES - n_hist + tap
            y = y + xp_ref[off:off + tm, c] * cw_ref[tap:tap + 1, c]
        gelu = 0.5 * y * (1.0 + lax.erf(y * (2.0 ** -0.5)))
        act.append((gelu * vj).astype(BF16))
    acc = ALPHA * h
    for c, aj in zip(cols, act):
        acc = acc + jnp.dot(aj, wd_ref[c, :], preferred_element_type=F32)
    mu = jnp.mean(acc, axis=-1, keepdims=True)
    xc = acc - mu
    var = jnp.mean(xc * xc, axis=-1, keepdims=True)
    y_ref[0] = xc * lax.rsqrt(var + LN_EPS) * g_ref[...] + b_ref[...]
    buf_ref[0] = xp_ref[SUBLANES + tm - n_hist:SUBLANES + tm, :]
    xp_ref[0:SUBLANES, :] = xp_ref[tm:tm + SUBLANES, :]


def _ffn_block(h3, w_gate, w_val, w_down, conv_w, conv_b, ln_g, ln_b, *, tm, n_split):
    bsz, t, d = h3.shape
    f = w_gate.shape[1]
    tm = min(tm, t)
    n_hist = FFN_CONV - 1
    once = pl.Buffered(1)
    const = lambda shape: pl.BlockSpec(shape, lambda i, r: (0,) * len(shape), pipeline_mode=once)
    rows = pl.BlockSpec((1, tm, d), lambda i, r: (i, r, 0))
    return pl.pallas_call(
        functools.partial(_ffn_block_kernel, tm=tm, n_split=n_split),
        grid=(bsz, t // tm),
        in_specs=[rows, const((d, f)), const((d, f)), const((f, d)), const((FFN_CONV, f)), const((1, f)),
                  const((1, d)), const((1, d))],
        out_specs=[rows, pl.BlockSpec((1, n_hist, f), lambda i, r: (i, 0, 0))],
        out_shape=[jax.ShapeDtypeStruct((bsz, t, d), F32), jax.ShapeDtypeStruct((bsz, n_hist, f), F32)],
        scratch_shapes=[pltpu.VMEM((SUBLANES + tm, f), F32)],
        compiler_params=_params(2),
        name="ffn_block",
    )(h3, w_gate, w_val, w_down, conv_w, conv_b.reshape(1, f), ln_g.reshape(1, d), ln_b.reshape(1, d))


def _layer(x3, s_hg, s_gd, hist_gd, hist_ffn, mk, mv, weights, *, row_lo, grp, ffn_streams, c_mix, nc_mix,
           nb_mix, fuse_proj, nb_attn):
    (mixer_w, w_out, ln1_g, ln1_b, w_mq, w_mo, ln2_g, ln2_b,
     w_gate, w_val, w_ffn_conv, b_ffn_conv, w_down, ln3_g, ln3_b) = weights
    bsz, t, d = x3.shape
    rows = bsz * t
    x2 = x3.reshape(rows, d)

    if fuse_proj:
        proj = tail = None
    else:
        proj = _matmul(x2, mixer_w[1], 1024, 512).reshape(bsz, t, -1)
        tail = _matmul(x2, mixer_w[2], 1024, HEAD_DIM).reshape(bsz, t, HEAD_DIM)
    mix, new_hg, new_gd, new_buf_gd = _mixer(x3, proj, tail, mixer_w, hist_gd, s_hg, s_gd,
                                             c=c_mix, nc=nc_mix, nb=nb_mix, row_lo=row_lo)

    if nb_attn is None:
        h2 = _attn_block(mix, x3, mk, mv, w_out, w_mq, w_mo, ln1_g, ln1_b, ln2_g, ln2_b, tm=512, n_sub=2).reshape(rows, d)
    else:
        h1 = _matmul_res_ln([mix.reshape(rows, d)], [w_out], x2, ln1_g, ln1_b, 512)
        q = _matmul(h1, w_mq, 1024, 1024)
        att = _mem_attn_core(q.reshape(bsz, t, d), mk, mv, tq=512, nb=nb_attn)
        h2 = _matmul_res_ln([att.reshape(rows, d)], [w_mo], h1, ln2_g, ln2_b, 512)

    if hist_ffn is None and grp == t:
        y, new_buf_ffn = _ffn_block(h2.reshape(bsz, t, d), w_gate, w_val, w_down, w_ffn_conv, b_ffn_conv,
                                    ln3_g, ln3_b, tm=512, n_split=2)
        return y, new_hg, new_gd, new_buf_gd, new_buf_ffn
    h2_3 = h2.reshape(ffn_streams, rows // ffn_streams, d)
    hist3 = None if hist_ffn is None else hist_ffn.reshape(ffn_streams, rows // ffn_streams, D_FF)
    act, new_buf_ffn = _ffn_up(h2_3, w_gate, w_val, w_ffn_conv, b_ffn_conv, hist3,
                               tm=512, tn=D_FF // 2, grp=grp, row_lo=row_lo)
    y = _matmul_res_ln([act.reshape(rows, D_FF)], [w_down], h2, ln3_g, ln3_b, 512)
    return y.reshape(bsz, t, d), new_hg, new_gd, new_buf_gd, new_buf_ffn


def _cache_rows(cache):
    b, m, h, d = cache.shape
    return cache.reshape(b, m, h, d // LANES, LANES).transpose(0, 1, 3, 2, 4).reshape(b, m * h * d // LANES, LANES)


def _cache_from_rows(rows):
    b = rows.shape[0]
    halves = MEM_HEAD_DIM // LANES
    return (rows.reshape(b, N_MEM, halves, MEM_HEADS, LANES).transpose(0, 1, 3, 2, 4)
            .reshape(b, N_MEM, MEM_HEADS, MEM_HEAD_DIM))


def kernel(x_prompt, x_sample, state_hgrn, state_gdn, state_gdn_conv, state_ffn_conv, cache_mem_k, cache_mem_v, mem_prompt, hgrn_lb_logits, w_in, w_gd_conv, gd_a_log, gd_dt_bias, hg_norm_g, gd_norm_g, w_out, ln1_g, ln1_b, w_mq, w_mkv, w_mo, ln2_g, ln2_b, w_up, w_ffn_conv, b_ffn_conv, w_down, ln3_g, ln3_b):
    bp, tp, d = x_prompt.shape
    bs, ts, _ = x_sample.shape
    tpad = SUBLANES
    row_lo = tpad - ts
    l = 0

    main_cols = 8 * MIX_HALF
    w_in_tail = jnp.pad(w_in[l][:, main_cols:], ((0, 0), (0, HEAD_DIM - 2 * N_HEADS))).astype(BF16)
    mixer_w = (hgrn_lb_logits, w_in[l][:, :main_cols].astype(BF16), w_in_tail, w_gd_conv[l],
               gd_a_log[l], gd_dt_bias[l], hg_norm_g[l], gd_norm_g[l])
    weights = (mixer_w, w_out[l].astype(BF16), ln1_g[l], ln1_b[l],
               w_mq[l].astype(BF16), w_mo[l].astype(BF16), ln2_g[l], ln2_b[l],
               w_up[l][:, :D_FF].astype(BF16), w_up[l][:, D_FF:].astype(BF16),
               w_ffn_conv[l], b_ffn_conv[l], w_down[l].astype(BF16), ln3_g[l], ln3_b[l])

    mk = _kv_proj(mem_prompt, w_mkv[l][:, :d].astype(BF16))
    mv = _kv_proj(mem_prompt, w_mkv[l][:, d:].astype(BF16))
    yp, p_hg, p_gd, p_bgd, p_bff = _layer(x_prompt, None, None, None, None, mk, mv, weights,
                                          row_lo=0, grp=tp, ffn_streams=bp, c_mix=128, nc_mix=4, nb_mix=1,
                                          fuse_proj=True, nb_attn=None)

    n_hg = GD_CONV - 1
    n_hf = FFN_CONV - 1
    xs = jnp.pad(x_sample, ((0, 0), (row_lo, 0), (0, 0)))
    hist_gd = jnp.pad(state_gdn_conv[l], ((0, 0), (row_lo - n_hg, ts), (0, 0)))
    hist_ffn = jnp.pad(state_ffn_conv[l], ((0, 0), (row_lo - n_hf, ts), (0, 0)))
    ys, s_hg, s_gd, s_bgd, s_bff = _layer(
        xs, state_hgrn[l], state_gdn[l], hist_gd, hist_ffn,
        _cache_rows(cache_mem_k[l]), _cache_rows(cache_mem_v[l]), weights,
        row_lo=row_lo, grp=tpad, ffn_streams=1, c_mix=tpad, nc_mix=1, nb_mix=8, fuse_proj=False,
        nb_attn=4)
    ys = ys[:, row_lo:]
    s_bff = s_bff.reshape(bs, tpad, D_FF)[:, tpad - n_hf:]

    return (yp, ys, p_hg[None], p_gd[None], p_bgd[None], p_bff[None],
            _cache_from_rows(mk)[None], _cache_from_rows(mv)[None],
            s_hg[None], s_gd[None], s_bgd[None], s_bff[None])
```

```python
import functools

import jax
import jax.numpy as jnp
from jax import lax
from jax.experimental import pallas as pl
from jax.experimental.pallas import tpu as pltpu

F32 = jnp.float32
BF16 = jnp.bfloat16

D_MODEL = 1024
HEAD_DIM = 128
N_HEADS = 4
MIX_HALF = N_HEADS * HEAD_DIM
GD_CONV = 4
FFN_CONV = 3
D_FF = 2816
N_MEM = 256
MEM_HEADS = 4
MEM_HEAD_DIM = D_MODEL // MEM_HEADS
LN_EPS = 1e-5
RMS_EPS = 1e-6
LOG2_E = 1.4426950408889634
DEPTH = 1
ALPHA = (2.0 * DEPTH) ** 0.25

SUBLANES = 8
LANES = 128
VMEM_LIMIT = 56 * 1024 * 1024


def _params(n_axes):
    return pltpu.CompilerParams(dimension_semantics=("arbitrary",) * n_axes,
                                vmem_limit_bytes=VMEM_LIMIT)


def _dot(a, b):
    return jnp.dot(a.astype(BF16), b.astype(BF16), preferred_element_type=F32)


def _dot_nt(a, b):
    return lax.dot_general(a.astype(BF16), b.astype(BF16), (((1,), (1,)), ((), ())),
                           preferred_element_type=F32)


def _dot_tn(a, b):
    return lax.dot_general(a.astype(BF16), b.astype(BF16), (((0,), (0,)), ((), ())),
                           preferred_element_type=F32)


def _split3(x):
    x1 = x.astype(BF16)
    r = x - x1.astype(F32)
    x2 = r.astype(BF16)
    x3 = (r - x2.astype(F32)).astype(BF16)
    return x1, x2, x3


def _cumsum_rows(tri_bf16, x):
    x1, x2, x3 = _split3(x)
    return (jnp.dot(tri_bf16, x1, preferred_element_type=F32)
            + jnp.dot(tri_bf16, x2, preferred_element_type=F32)
            + jnp.dot(tri_bf16, x3, preferred_element_type=F32))


def _multi_dot(a, b, keys, nt=False):
    dot = _dot_nt if nt else _dot
    return {k: dot(a[k], b[k]) for k in keys}


def _silu(x):
    return x * jax.nn.sigmoid(x)


def _mm_kernel(x_ref, w_ref, o_ref, xb_ref):
    @pl.when(pl.program_id(1) == 0)
    def _():
        xb_ref[...] = x_ref[...].astype(BF16)

    o_ref[...] = jnp.dot(xb_ref[...], w_ref[...], preferred_element_type=F32).astype(o_ref.dtype)


def _matmul(x, w, tm, tn, n=None):
    m, k = x.shape
    n = w.shape[1] if n is None else n
    tm = min(tm, m)
    tn = min(tn, n)
    return pl.pallas_call(
        _mm_kernel,
        grid=(m // tm, n // tn),
        in_specs=[pl.BlockSpec((tm, k), lambda i, j: (i, 0)),
                  pl.BlockSpec((k, tn), lambda i, j: (0, j))],
        out_specs=pl.BlockSpec((tm, tn), lambda i, j: (i, j)),
        out_shape=jax.ShapeDtypeStruct((m, n), F32),
        scratch_shapes=[pltpu.VMEM((tm, k), BF16)],
        compiler_params=_params(2),
        name="proj_matmul",
    )(x, w)


ROWS_PER_MEM = MEM_HEADS * MEM_HEAD_DIM // LANES


def _head_rows(h, j):
    return pl.ds(h + MEM_HEADS * j, N_MEM, stride=ROWS_PER_MEM)


def _head_block(ref, b, h):
    return jnp.concatenate([ref[b, _head_rows(h, j), :] for j in range(MEM_HEAD_DIM // LANES)], axis=1)


def _kv_proj_kernel(x_ref, w_ref, o_ref):
    res = jnp.dot(x_ref[0].astype(BF16), w_ref[...], preferred_element_type=F32)
    for h in range(MEM_HEADS):
        for j in range(MEM_HEAD_DIM // LANES):
            c0 = h * MEM_HEAD_DIM + j * LANES
            o_ref[0, _head_rows(h, j), :] = res[:, c0:c0 + LANES]


def _kv_proj(mem3, w):
    bsz, n_mem, d = mem3.shape
    return pl.pallas_call(
        _kv_proj_kernel,
        grid=(bsz,),
        in_specs=[pl.BlockSpec((1, n_mem, d), lambda i: (i, 0, 0)), pl.BlockSpec(w.shape, lambda i: (0, 0))],
        out_specs=pl.BlockSpec((1, n_mem * ROWS_PER_MEM, LANES), lambda i: (i, 0, 0)),
        out_shape=jax.ShapeDtypeStruct((bsz, n_mem * ROWS_PER_MEM, LANES), F32),
        compiler_params=_params(1),
        name="kv_proj",
    )(mem3, w)


def _mm_ln_kernel(*refs, n_in):
    a_refs = refs[:n_in]
    w_refs = refs[n_in:2 * n_in]
    res_ref, g_ref, b_ref, o_ref = refs[2 * n_in:]
    acc = ALPHA * res_ref[...]
    for a_ref, w_ref in zip(a_refs, w_refs):
        acc = acc + jnp.dot(a_ref[...].astype(BF16), w_ref[...], preferred_element_type=F32)
    mu = jnp.mean(acc, axis=-1, keepdims=True)
    xc = acc - mu
    var = jnp.mean(xc * xc, axis=-1, keepdims=True)
    o_ref[...] = xc * lax.rsqrt(var + LN_EPS) * g_ref[...] + b_ref[...]


def _matmul_res_ln(a_list, w_list, res, g, b, tm):
    m, d = res.shape
    tm = min(tm, m)
    n_in = len(a_list)
    in_specs = ([pl.BlockSpec((tm, a.shape[1]), lambda i: (i, 0)) for a in a_list]
                + [pl.BlockSpec(w.shape, lambda i: (0, 0)) for w in w_list]
                + [pl.BlockSpec((tm, d), lambda i: (i, 0)),
                   pl.BlockSpec((1, d), lambda i: (0, 0)),
                   pl.BlockSpec((1, d), lambda i: (0, 0))])
    return pl.pallas_call(
        functools.partial(_mm_ln_kernel, n_in=n_in),
        grid=(m // tm,),
        in_specs=in_specs,
        out_specs=pl.BlockSpec((tm, d), lambda i: (i, 0)),
        out_shape=jax.ShapeDtypeStruct((m, d), F32),
        compiler_params=_params(1),
        name="proj_res_ln",
    )(*a_list, *w_list, res, g.reshape(1, d), b.reshape(1, d))


PROJ_COLS = 8 * MIX_HALF
HG_COL = 0
GD_COL = 4 * MIX_HALF
HGRN_GROUP = 1
GDN_GROUP = 4


def _ref_rows(gc_ref, b, r0, col, m, c):
    blk = 2 * m
    if blk >= SUBLANES:
        parts = [jnp.broadcast_to(gc_ref[b, pl.ds(r0 + j * blk + m - 1, 1), col], (blk, HEAD_DIM))
                 for j in range(c // blk)]
    else:
        sub = lax.broadcasted_iota(jnp.int32, (SUBLANES, HEAD_DIM), 0)
        parts = []
        for i in range(c // SUBLANES):
            tile = None
            for j in range(SUBLANES // blk):
                row = jnp.broadcast_to(gc_ref[b, pl.ds(r0 + i * SUBLANES + j * blk + m - 1, 1), col],
                                       (SUBLANES, HEAD_DIM))
                tile = row if tile is None else jnp.where(sub >= j * blk, row, tile)
            parts.append(tile)
    return parts[0] if len(parts) == 1 else jnp.concatenate(parts, axis=0)


def _hgrn_rows(p_ref, mix_ref, lb, ng, st_ref, gc_ref, *, chunk_ids, c, nb, row_lo):
    rows = lax.broadcasted_iota(jnp.int32, (c, 1), 0)
    ti = lax.broadcasted_iota(jnp.int32, (c, c), 0)
    si = lax.broadcasted_iota(jnp.int32, (c, c), 1)
    tri = (si <= ti).astype(F32).astype(BF16)
    eye = ti == si
    xr = ti ^ si
    cols = [slice(h * HEAD_DIM, (h + 1) * HEAD_DIM) for h in range(N_HEADS)]
    chains = [(b, j, h) for b in range(nb) for j in chunk_ids for h in range(N_HEADS)]

    def proj(b, j, part):
        return p_ref[b, pl.ds(j * c, c), HG_COL + part * MIX_HALF:HG_COL + (part + 1) * MIX_HALF]

    q, k, gc = {}, {}, {}
    for b in range(nb):
        for j in chunk_ids:
            f = lb + (1.0 - lb) * jax.nn.sigmoid(proj(b, j, 1))
            g = jnp.log(f)
            kb = 1.0 - f
            if row_lo:
                g = jnp.where(rows >= row_lo, g, 0.0)
                kb = jnp.where(rows >= row_lo, kb, 0.0)
            qb = _silu(proj(b, j, 0))
            gcb = _cumsum_rows(tri, g)
            gc_ref[b, pl.ds(j * c, c), :] = gcb
            for h in range(N_HEADS):
                q[b, j, h], k[b, j, h], gc[b, j, h] = qb[:, cols[h]], kb[:, cols[h]], gcb[:, cols[h]]

    level_of = jnp.where(eye, 0, -1)
    m = 1
    while m < c:
        level_of = jnp.where((xr >= m) & (xr < 2 * m) & (si < ti), m, level_of)
        m *= 2
    rows_w = lax.broadcasted_iota(jnp.int32, (c, HEAD_DIM), 0)
    q_b = {ch: q[ch].astype(BF16) for ch in chains}
    k_b = {ch: k[ch].astype(BF16) for ch in chains}
    sc = _multi_dot(q_b, k_b, chains, nt=True)
    sc = {ch: jnp.where(level_of == 0, sc[ch], 0.0) for ch in chains}
    m = c // 2
    while m >= 1:
        sign = jnp.where((rows_w & m) != 0, LOG2_E, -LOG2_E)
        qw, kw = {}, {}
        for ch in chains:
            b, j, h = ch
            w = jnp.exp2((gc[ch] - _ref_rows(gc_ref, b, j * c, cols[h], m, c)) * sign).astype(BF16)
            qw[ch], kw[ch] = q_b[ch] * w, k_b[ch] * w
        prod = _multi_dot(qw, kw, chains, nt=True)
        sc = {ch: jnp.where(level_of == m, prod[ch], sc[ch]) for ch in chains}
        m //= 2

    v = {(b, j, h): p_ref[b, pl.ds(j * c, c),
                          HG_COL + 2 * MIX_HALF + h * HEAD_DIM:HG_COL + 2 * MIX_HALF + (h + 1) * HEAD_DIM]
         for b, j, h in chains}
    o_intra = _multi_dot(sc, v, chains)

    for j in chunk_ids:
        group = [(b, j, h) for b in range(nb) for h in range(N_HEADS)]
        st = {ch: st_ref[ch[0], ch[2]] for ch in group}
        o_inter = _multi_dot({ch: q[ch] * jnp.exp(gc[ch]) for ch in group}, st, group, nt=True)
        for ch in group:
            b, _, h = ch
            g_last = gc_ref[b, pl.ds(j * c + c - 1, 1), cols[h]]
            st_ref[b, h] = st[ch] * jnp.exp(g_last) + _dot_tn(v[ch], k[ch] * jnp.exp(g_last - gc[ch]))
        for ch in group:
            b, _, h = ch
            ob = o_intra[ch] + o_inter[ch]
            on = ob * lax.rsqrt(jnp.mean(ob * ob, axis=-1, keepdims=True) + RMS_EPS) * ng
            gate = p_ref[b, pl.ds(j * c, c),
                         HG_COL + 3 * MIX_HALF + h * HEAD_DIM:HG_COL + 3 * MIX_HALF + (h + 1) * HEAD_DIM]
            mix_ref[b, pl.ds(j * c, c), cols[h]] = on * _silu(gate)


def _gdn_rows(p_ref, t_ref, hist_ref, mix_ref, cb_ref, cw_ref, neg_a, dt_bias, ng, s_ref, xp_ref,
              *, c, nc, nb, row_lo):
    n_hist = GD_CONV - 1
    rb = c * nc
    rows = lax.broadcasted_iota(jnp.int32, (rb, 1), 0) % c
    ti = lax.broadcasted_iota(jnp.int32, (c, c), 0)
    si = lax.broadcasted_iota(jnp.int32, (c, c), 1)
    tri = (si <= ti).astype(F32).astype(BF16)
    eye_f = (ti == si).astype(F32)
    incl = si <= ti

    q_all, k_all, v_all, beta_all, la_all = {}, {}, {}, {}, {}
    for b in range(nb):
        qkv = []
        for j in range(3):
            col = slice(j * MIX_HALF, (j + 1) * MIX_HALF)
            x = p_ref[b, :, GD_COL + j * MIX_HALF:GD_COL + (j + 1) * MIX_HALF]
            if hist_ref is not None:
                x = jnp.where((rows >= row_lo - n_hist) & (rows < row_lo), hist_ref[b, :, col], x)
            xp_ref[b, SUBLANES:SUBLANES + rb, col] = x
            y = x * cw_ref[n_hist:n_hist + 1, col]
            for tap in range(n_hist):
                off = SUBLANES - n_hist + tap
                y = y + xp_ref[b, off:off + rb, col] * cw_ref[tap:tap + 1, col]
            qkv.append(_silu(y))
        cb_ref[b] = xp_ref[b, SUBLANES + rb - n_hist:SUBLANES + rb, :]
        xp_ref[b, 0:SUBLANES, :] = xp_ref[b, rb:rb + SUBLANES, :]
        q_all[b], k_all[b], v_all[b] = qkv

        tail = t_ref[b]
        beta_all[b] = jax.nn.sigmoid(tail)
        la_all[b] = neg_a * jax.nn.softplus(tail + dt_bias)
        if row_lo:
            beta_all[b] = jnp.where(rows >= row_lo, beta_all[b], 0.0)
            la_all[b] = jnp.where(rows >= row_lo, la_all[b], 0.0)

    for j0 in range(0, nc, GDN_GROUP):
        _gdn_chunks(list(range(j0, min(nc, j0 + GDN_GROUP))), q_all, k_all, v_all, beta_all, la_all,
                    p_ref, mix_ref, ng, s_ref, c=c, nb=nb)


def _gdn_chunks(chunk_ids, q_all, k_all, v_all, beta_all, la_all, p_ref, mix_ref, ng, s_ref, *, c, nb):
    ti = lax.broadcasted_iota(jnp.int32, (c, c), 0)
    si = lax.broadcasted_iota(jnp.int32, (c, c), 1)
    tri = (si <= ti).astype(F32).astype(BF16)
    eye_f = (ti == si).astype(F32)
    incl = si <= ti
    chains = [(b, j, h) for b in range(nb) for j in chunk_ids for h in range(N_HEADS)]

    gc = {(b, j): _cumsum_rows(tri, la_all[b][j * c:(j + 1) * c]) for b in range(nb) for j in chunk_ids}

    pre = {}
    for ch in chains:
        b, j, h = ch
        rs = slice(j * c, (j + 1) * c)
        col = slice(h * HEAD_DIM, (h + 1) * HEAD_DIM)
        qh, kh, vh = q_all[b][rs, col], k_all[b][rs, col], v_all[b][rs, col]
        qh = qh * lax.rsqrt(jnp.sum(qh * qh, axis=-1, keepdims=True) + RMS_EPS) * (HEAD_DIM ** -0.5)
        kh = kh * lax.rsqrt(jnp.sum(kh * kh, axis=-1, keepdims=True) + RMS_EPS)
        beta = beta_all[b][rs, h:h + 1]
        gcol = gc[b, j][:, N_HEADS + h:N_HEADS + h + 1]
        grow = jnp.sum(eye_f * gcol, axis=0, keepdims=True)
        decay = jnp.where(incl, jnp.exp(jnp.where(incl, gcol - grow, 0.0)), 0.0)
        gamma = jnp.exp(gcol)
        g_last = gcol[c - 1:c, :]
        pre[ch] = dict(
            q=qh, k=kh, beta=beta, decay=decay,
            rhs=jnp.concatenate([(beta * gamma) * kh, beta * vh], axis=1).astype(BF16),
            gq=(gamma * qh).astype(BF16),
            kd=(kh * jnp.exp(g_last - gcol)).astype(BF16),
            eg=jnp.exp(g_last))
    k_b = {ch: pre[ch]["k"].astype(BF16) for ch in chains}
    kk = _multi_dot(k_b, k_b, chains, nt=True)
    qk = _multi_dot({ch: pre[ch]["q"] for ch in chains}, k_b, chains, nt=True)
    a_mat = {ch: (pre[ch]["beta"] * kk[ch] * pre[ch]["decay"]).astype(BF16) for ch in chains}
    aqk = {ch: (qk[ch] * pre[ch]["decay"]).astype(BF16) for ch in chains}

    xr = ti ^ si
    first = (xr < 2) & (si < ti)
    zero_b = jnp.zeros((c, c), BF16)
    t_inv = {ch: (eye_f - jnp.where(first, a_mat[ch], zero_b).astype(F32)).astype(BF16) for ch in chains}
    n = 2
    while n < c:
        lower = (xr < 2 * n) & ((ti & n) != 0) & ((si & n) == 0)
        x = _multi_dot({ch: jnp.where(lower, a_mat[ch], zero_b) for ch in chains}, t_inv, chains)
        tx = _multi_dot(t_inv, x, chains)
        t_inv = {ch: (t_inv[ch].astype(F32) - tx[ch]).astype(BF16) for ch in chains}
        n *= 2
    wu = {ch: _dot(t_inv[ch], pre[ch]["rhs"]) for ch in chains}

    for j in chunk_ids:
        rs = slice(j * c, (j + 1) * c)
        group = [(b, j, h) for b in range(nb) for h in range(N_HEADS)]
        s_old = {ch: s_ref[ch[0], ch[2]] for ch in group}
        ws = _multi_dot({ch: jnp.concatenate([wu[ch][:, :HEAD_DIM].astype(BF16), pre[ch]["gq"]], axis=0)
                         for ch in group}, s_old, group)
        u = {ch: (wu[ch][:, HEAD_DIM:] - ws[ch][:c]).astype(BF16) for ch in group}
        au = _multi_dot(aqk, u, group)
        for ch in group:
            b, _, h = ch
            s_ref[b, h] = pre[ch]["eg"] * s_old[ch] + _dot_tn(pre[ch]["kd"], u[ch])
            o = ws[ch][c:] + au[ch]
            on = o * lax.rsqrt(jnp.mean(o * o, axis=-1, keepdims=True) + RMS_EPS) * ng
            gate = p_ref[b, rs, GD_COL + 3 * MIX_HALF + h * HEAD_DIM:GD_COL + 3 * MIX_HALF + (h + 1) * HEAD_DIM]
            mix_ref[b, rs, MIX_HALF + h * HEAD_DIM:MIX_HALF + (h + 1) * HEAD_DIM] = on * _silu(gate)


def _mixer_kernel(*refs, c, nc, nb, fuse_proj, has_state, row_lo, n_steps):
    refs = list(refs)
    if fuse_proj:
        x_ref, w_ref, wt_ref = refs[:3]
        del refs[:3]
    else:
        p_ref, t_ref = refs[:2]
        del refs[:2]
    lbl_ref, hng_ref, cw_ref, par_ref, gng_ref = refs[:5]
    del refs[:5]
    if has_state:
        hist_ref, shg0_ref, sgd0_ref = refs[:3]
        del refs[:3]
    else:
        hist_ref = shg0_ref = sgd0_ref = None
    mix_ref, shg_ref, sgd_ref, cb_ref, st_ref, s_ref, gc_ref, xp_ref = refs[:8]
    if fuse_proj:
        p_ref, t_ref = refs[8:]
    step = pl.program_id(1)
    pairs = [(b, h) for b in range(nb) for h in range(N_HEADS)]

    @pl.when(step == 0)
    def _():
        for b, h in pairs:
            if has_state:
                st_ref[b, h] = shg0_ref[b, h].T
                s_ref[b, h] = sgd0_ref[b, h]
            else:
                st_ref[b, h] = jnp.zeros((HEAD_DIM, HEAD_DIM), F32)
                s_ref[b, h] = jnp.zeros((HEAD_DIM, HEAD_DIM), F32)
        for b in range(nb):
            xp_ref[b, 0:SUBLANES, :] = jnp.zeros((SUBLANES, 3 * MIX_HALF), F32)

    if fuse_proj:
        for b in range(nb):
            xb = x_ref[b].astype(BF16)
            p_ref[b] = jnp.dot(xb, w_ref[...], preferred_element_type=F32)
            t_ref[b] = jnp.dot(xb, wt_ref[...], preferred_element_type=F32)

    lbl = lbl_ref[...]
    e = jnp.exp(lbl - jnp.max(lbl, axis=0, keepdims=True))
    lb = e[0:1] / jnp.sum(e, axis=0, keepdims=True)
    neg_a = -jnp.exp(par_ref[0:1, :])
    dt_bias = par_ref[1:2, :]

    for j0 in range(0, nc, HGRN_GROUP):
        _hgrn_rows(p_ref, mix_ref, lb, hng_ref[...], st_ref, gc_ref,
                   chunk_ids=list(range(j0, min(nc, j0 + HGRN_GROUP))), c=c, nb=nb, row_lo=row_lo)
    _gdn_rows(p_ref, t_ref, hist_ref, mix_ref, cb_ref, cw_ref, neg_a, dt_bias, gng_ref[...], s_ref, xp_ref,
              c=c, nc=nc, nb=nb, row_lo=row_lo)

    @pl.when(step == n_steps - 1)
    def _():
        for b, h in pairs:
            shg_ref[b, h] = st_ref[b, h].T
            sgd_ref[b, h] = s_ref[b, h]


def _mixer(x3, proj, tail, weights, hist, s_hg, s_gd, *, c, nc, nb, row_lo):
    lb_logits, w_main, w_tail, conv_w, a_log, dt_bias, hg_norm_g, gd_norm_g = weights
    fuse_proj = proj is None
    bsz, t = (x3 if fuse_proj else proj).shape[:2]
    rb = c * nc
    n_steps = t // rb
    has_state = s_hg is not None
    n_hist = GD_CONV - 1
    const = lambda shape: pl.BlockSpec(shape, lambda i, si: (0,) * len(shape))
    rows3 = lambda width: pl.BlockSpec((nb, rb, width), lambda i, si: (i, si, 0))
    st_spec = pl.BlockSpec((nb, N_HEADS, HEAD_DIM, HEAD_DIM), lambda i, si: (i, 0, 0, 0))
    par = jnp.zeros((SUBLANES, HEAD_DIM), F32)
    par = par.at[0, N_HEADS:2 * N_HEADS].set(a_log).at[1, N_HEADS:2 * N_HEADS].set(dt_bias)
    if fuse_proj:
        in_specs = [rows3(D_MODEL), const((D_MODEL, PROJ_COLS)), const(w_tail.shape)]
        args = [x3, w_main, w_tail]
    else:
        in_specs = [rows3(PROJ_COLS), rows3(HEAD_DIM)]
        args = [proj, tail]
    in_specs += [const(lb_logits.shape), const((1, HEAD_DIM)), const(conv_w.shape), const(par.shape),
                 const((1, HEAD_DIM))]
    args += [lb_logits, hg_norm_g.reshape(1, HEAD_DIM), conv_w, par, gd_norm_g.reshape(1, HEAD_DIM)]
    if has_state:
        in_specs += [rows3(3 * MIX_HALF), st_spec, st_spec]
        args += [hist, s_hg, s_gd]
    scratch = [pltpu.VMEM((nb, N_HEADS, HEAD_DIM, HEAD_DIM), F32),
               pltpu.VMEM((nb, N_HEADS, HEAD_DIM, HEAD_DIM), F32),
               pltpu.VMEM((nb, rb, MIX_HALF), F32),
               pltpu.VMEM((nb, SUBLANES + rb, 3 * MIX_HALF), F32)]
    if fuse_proj:
        scratch += [pltpu.VMEM((nb, rb, PROJ_COLS), F32), pltpu.VMEM((nb, rb, HEAD_DIM), F32)]
    return pl.pallas_call(
        functools.partial(_mixer_kernel, c=c, nc=nc, nb=nb, fuse_proj=fuse_proj, has_state=has_state,
                          row_lo=row_lo, n_steps=n_steps),
        grid=(bsz // nb, n_steps),
        in_specs=in_specs,
        out_specs=[rows3(D_MODEL), st_spec, st_spec,
                   pl.BlockSpec((nb, n_hist, 3 * MIX_HALF), lambda i, si: (i, 0, 0))],
        out_shape=[jax.ShapeDtypeStruct((bsz, t, D_MODEL), F32),
                   jax.ShapeDtypeStruct((bsz, N_HEADS, HEAD_DIM, HEAD_DIM), F32),
                   jax.ShapeDtypeStruct((bsz, N_HEADS, HEAD_DIM, HEAD_DIM), F32),
                   jax.ShapeDtypeStruct((bsz, n_hist, 3 * MIX_HALF), F32)],
        scratch_shapes=scratch,
        compiler_params=_params(2),
        name="mixer",
    )(*args)


def _attn_kernel(q_ref, k_ref, v_ref, o_ref, *, nb):
    scale = MEM_HEAD_DIM ** -0.5
    pairs = [(b, h) for b in range(nb) for h in range(MEM_HEADS)]
    s = {bh: _dot_nt(q_ref[bh[0], :, bh[1] * MEM_HEAD_DIM:(bh[1] + 1) * MEM_HEAD_DIM],
                     _head_block(k_ref, *bh)) * scale for bh in pairs}
    p = {}
    for bh in pairs:
        e = jnp.exp(s[bh] - jnp.max(s[bh], axis=-1, keepdims=True))
        p[bh] = e / jnp.sum(e, axis=-1, keepdims=True)
    for b, h in pairs:
        o_ref[b, :, h * MEM_HEAD_DIM:(h + 1) * MEM_HEAD_DIM] = _dot(p[b, h], _head_block(v_ref, b, h))


def _mem_attn_core(q3, mk, mv, *, tq, nb):
    bsz, t, d = q3.shape
    tq = min(tq, t)
    kv_spec = pl.BlockSpec((nb,) + mk.shape[1:], lambda i, r: (i, 0, 0))
    return pl.pallas_call(
        functools.partial(_attn_kernel, nb=nb),
        grid=(bsz // nb, t // tq),
        in_specs=[pl.BlockSpec((nb, tq, d), lambda i, r: (i, r, 0)), kv_spec, kv_spec],
        out_specs=pl.BlockSpec((nb, tq, d), lambda i, r: (i, r, 0)),
        out_shape=jax.ShapeDtypeStruct((bsz, t, d), F32),
        compiler_params=_params(2),
        name="mem_attn_core",
    )(q3, mk, mv)


def _attn_block_kernel(mix_ref, x_ref, k_ref, v_ref, wo_ref, wq_ref, wm_ref, g1_ref, b1_ref, g2_ref, b2_ref,
                       o_ref, *, n_sub):
    def layer_norm(acc, g_ref, b_ref):
        mu = jnp.mean(acc, axis=-1, keepdims=True)
        xc = acc - mu
        var = jnp.mean(xc * xc, axis=-1, keepdims=True)
        return xc * lax.rsqrt(var + LN_EPS) * g_ref[...] + b_ref[...]

    ts = x_ref.shape[1] // n_sub
    subs = [pl.ds(i * ts, ts) for i in range(n_sub)]
    scale = MEM_HEAD_DIM ** -0.5
    cols = [slice(h * MEM_HEAD_DIM, (h + 1) * MEM_HEAD_DIM) for h in range(MEM_HEADS)]
    heads = range(MEM_HEADS)
    kh = [_head_block(k_ref, 0, h).astype(BF16) for h in heads]
    vh = [_head_block(v_ref, 0, h).astype(BF16) for h in heads]
    h1 = [layer_norm(ALPHA * x_ref[0, r, :]
                     + jnp.dot(mix_ref[0, r, :].astype(BF16), wo_ref[...], preferred_element_type=F32),
                     g1_ref, b1_ref) for r in subs]
    q = [jnp.dot(hi.astype(BF16), wq_ref[...], preferred_element_type=F32).astype(BF16) for hi in h1]
    s = [[_dot_nt(qi[:, cols[h]], kh[h]) * scale for h in heads] for qi in q]
    p = []
    for si in s:
        pi = []
        for sh in si:
            e = jnp.exp(sh - jnp.max(sh, axis=-1, keepdims=True))
            pi.append(e / jnp.sum(e, axis=-1, keepdims=True))
        p.append(pi)
    att = [jnp.concatenate([_dot(pi[h], vh[h]).astype(BF16) for h in heads], axis=1) for pi in p]
    for r, hi, ai in zip(subs, h1, att):
        o_ref[0, r, :] = layer_norm(ALPHA * hi + jnp.dot(ai, wm_ref[...], preferred_element_type=F32),
                                    g2_ref, b2_ref)


def _attn_block(mix3, x3, mk, mv, w_out, w_mq, w_mo, ln1_g, ln1_b, ln2_g, ln2_b, *, tm, n_sub):
    bsz, t, d = x3.shape
    tm = min(tm, t)
    rows = pl.BlockSpec((1, tm, d), lambda i, r: (i, r, 0))
    kv = pl.BlockSpec((1,) + mk.shape[1:], lambda i, r: (i, 0, 0))
    wsp = pl.BlockSpec((d, d), lambda i, r: (0, 0))
    vec = pl.BlockSpec((1, d), lambda i, r: (0, 0))
    return pl.pallas_call(
        functools.partial(_attn_block_kernel, n_sub=n_sub),
        grid=(bsz, t // tm),
        in_specs=[rows, rows, kv, kv, wsp, wsp, wsp, vec, vec, vec, vec],
        out_specs=rows,
        out_shape=jax.ShapeDtypeStruct((bsz, t, d), F32),
        compiler_params=_params(2),
        name="attn_block",
    )(mix3, x3, mk, mv, w_out, w_mq, w_mo, ln1_g.reshape(1, d), ln1_b.reshape(1, d),
      ln2_g.reshape(1, d), ln2_b.reshape(1, d))


def _ffn_up_kernel(*refs, tm, grp, row_lo, has_hist):
    if has_hist:
        h_ref, wg_ref, wv_ref, cw_ref, cb_ref, hist_ref, act_ref, buf_ref, xp_ref = refs
    else:
        h_ref, wg_ref, wv_ref, cw_ref, cb_ref, act_ref, buf_ref, xp_ref = refs
        hist_ref = None
    n_hist = FFN_CONV - 1
    r = pl.program_id(2)

    @pl.when(r == 0)
    def _():
        xp_ref[0:SUBLANES, :] = jnp.zeros((SUBLANES, xp_ref.shape[1]), F32)

    hb = h_ref[0].astype(BF16)
    gate = jnp.dot(hb, wg_ref[...], preferred_element_type=F32)
    val = jnp.dot(hb, wv_ref[...], preferred_element_type=F32)
    if hist_ref is not None:
        pos = lax.broadcasted_iota(jnp.int32, (tm, 1), 0) % grp
        gate = jnp.where((pos >= row_lo - n_hist) & (pos < row_lo), hist_ref[0], gate)
    xp_ref[SUBLANES:SUBLANES + tm, :] = gate
    y = gate * cw_ref[n_hist:n_hist + 1, :] + cb_ref[...]
    for tap in range(n_hist):
        off = SUBLANES - n_hist + tap
        y = y + xp_ref[off:off + tm, :] * cw_ref[tap:tap + 1, :]
    gelu = 0.5 * y * (1.0 + lax.erf(y * (2.0 ** -0.5)))
    act_ref[0] = (gelu * val).astype(act_ref.dtype)
    if buf_ref.shape[1] == n_hist:
        buf_ref[0] = xp_ref[SUBLANES + tm - n_hist:SUBLANES + tm, :]
    else:
        buf_ref[0] = gate
    xp_ref[0:SUBLANES, :] = xp_ref[tm:tm + SUBLANES, :]


def _ffn_up(h3, w_up, conv_w, conv_b, hist, *, tm, tn, grp, row_lo):
    g, r, d = h3.shape
    f = w_up.shape[1] // 2
    tm = min(tm, r)
    n_hist = FFN_CONV - 1
    has_hist = hist is not None
    assert grp == r or grp <= tm
    if grp == r:
        buf_spec = pl.BlockSpec((1, n_hist, tn), lambda gi, fi, ri: (gi, 0, fi))
        buf_rows = n_hist
    else:
        buf_spec = pl.BlockSpec((1, tm, tn), lambda gi, fi, ri: (gi, ri, fi))
        buf_rows = r
    in_specs = [pl.BlockSpec((1, tm, d), lambda gi, fi, ri: (gi, ri, 0)),
                pl.BlockSpec((d, tn), lambda gi, fi, ri: (0, fi)),
                pl.BlockSpec((d, tn), lambda gi, fi, ri: (0, fi + f // tn)),
                pl.BlockSpec((FFN_CONV, tn), lambda gi, fi, ri: (0, fi)),
                pl.BlockSpec((1, tn), lambda gi, fi, ri: (0, fi))]
    args = [h3, w_up, w_up, conv_w, conv_b.reshape(1, f)]
    if has_hist:
        in_specs.append(pl.BlockSpec((1, tm, tn), lambda gi, fi, ri: (gi, ri, fi)))
        args.append(hist)
    return pl.pallas_call(
        functools.partial(_ffn_up_kernel, tm=tm, grp=grp, row_lo=row_lo, has_hist=has_hist),
        grid=(g, f // tn, r // tm),
        in_specs=in_specs,
        out_specs=[pl.BlockSpec((1, tm, tn), lambda gi, fi, ri: (gi, ri, fi)),
                   buf_spec],
        out_shape=[jax.ShapeDtypeStruct((g, r, f), BF16),
                   jax.ShapeDtypeStruct((g, buf_rows, f), F32)],
        scratch_shapes=[pltpu.VMEM((SUBLANES + tm, tn), F32)],
        compiler_params=_params(3),
        name="ffn_up",
    )(*args)


def _ffn_block_kernel(h_ref, wg_ref, wv_ref, wd_ref, cw_ref, cb_ref, g_ref, b_ref, y_ref, buf_ref, xp_ref,
                      *, tm, n_split):
    n_hist = FFN_CONV - 1
    f = xp_ref.shape[1]
    fs = f // n_split

    @pl.when(pl.program_id(1) == 0)
    def _():
        xp_ref[0:SUBLANES, :] = jnp.zeros((SUBLANES, f), F32)

    h = h_ref[0]
    hb = h.astype(BF16)
    cols = [slice(j * fs, (j + 1) * fs) for j in range(n_split)]
    gate = [jnp.dot(hb, wg_ref[:, c], preferred_element_type=F32) for c in cols]
    val = [jnp.dot(hb, wv_ref[:, c], preferred_element_type=F32) for c in cols]
    act = []
    for c, gj, vj in zip(cols, gate, val):
        xp_ref[SUBLANES:SUBLANES + tm, c] = gj
        y = gj * cw_ref[n_hist:n_hist + 1, c] + cb_ref[:, c]
        for tap in range(n_hist):
            off = SUBLANES - n_hist + tap
            y = y + xp_ref[off:off + tm, c] * cw_ref[tap:tap + 1, c]
        gelu = 0.5 * y * (1.0 + lax.erf(y * (2.0 ** -0.5)))
        act.append((gelu * vj).astype(BF16))
    acc = ALPHA * h
    for c, aj in zip(cols, act):
        acc = acc + jnp.dot(aj, wd_ref[c, :], preferred_element_type=F32)
    mu = jnp.mean(acc, axis=-1, keepdims=True)
    xc = acc - mu
    var = jnp.mean(xc * xc, axis=-1, keepdims=True)
    y_ref[0] = xc * lax.rsqrt(var + LN_EPS) * g_ref[...] + b_ref[...]
    buf_ref[0] = xp_ref[SUBLANES + tm - n_hist:SUBLANES + tm, :]
    xp_ref[0:SUBLANES, :] = xp_ref[tm:tm + SUBLANES, :]


def _ffn_block(h3, w_up, w_down, conv_w, conv_b, ln_g, ln_b, *, tm, n_split):
    bsz, t, d = h3.shape
    f = w_down.shape[0]
    tm = min(tm, t)
    n_hist = FFN_CONV - 1
    once = pl.Buffered(1)
    const = lambda shape: pl.BlockSpec(shape, lambda i, r: (0,) * len(shape), pipeline_mode=once)
    rows = pl.BlockSpec((1, tm, d), lambda i, r: (i, r, 0))
    return pl.pallas_call(
        functools.partial(_ffn_block_kernel, tm=tm, n_split=n_split),
        grid=(bsz, t // tm),
        in_specs=[rows, const((d, f)), pl.BlockSpec((d, f), lambda i, r: (0, 1), pipeline_mode=once),
                  const((f, d)), const((FFN_CONV, f)), const((1, f)),
                  const((1, d)), const((1, d))],
        out_specs=[rows, pl.BlockSpec((1, n_hist, f), lambda i, r: (i, 0, 0))],
        out_shape=[jax.ShapeDtypeStruct((bsz, t, d), F32), jax.ShapeDtypeStruct((bsz, n_hist, f), F32)],
        scratch_shapes=[pltpu.VMEM((SUBLANES + tm, f), F32)],
        compiler_params=_params(2),
        name="ffn_block",
    )(h3, w_up, w_up, w_down, conv_w, conv_b.reshape(1, f), ln_g.reshape(1, d), ln_b.reshape(1, d))


def _layer(x3, s_hg, s_gd, hist_gd, hist_ffn, mk, mv, weights, *, row_lo, grp, ffn_streams, c_mix, nc_mix,
           nb_mix, fuse_proj, nb_attn):
    (mixer_w, w_out, ln1_g, ln1_b, w_mq, w_mo, ln2_g, ln2_b,
     w_up, w_ffn_conv, b_ffn_conv, w_down, ln3_g, ln3_b) = weights
    bsz, t, d = x3.shape
    rows = bsz * t
    x2 = x3.reshape(rows, d)

    if fuse_proj:
        proj = tail = None
    else:
        proj = _matmul(x2, mixer_w[1], 1024, 512, n=PROJ_COLS).reshape(bsz, t, -1)
        tail = _matmul(x2, mixer_w[2], 1024, HEAD_DIM).reshape(bsz, t, HEAD_DIM)
    mix, new_hg, new_gd, new_buf_gd = _mixer(x3, proj, tail, mixer_w, hist_gd, s_hg, s_gd,
                                             c=c_mix, nc=nc_mix, nb=nb_mix, row_lo=row_lo)

    if nb_attn is None:
        h2 = _attn_block(mix, x3, mk, mv, w_out, w_mq, w_mo, ln1_g, ln1_b, ln2_g, ln2_b, tm=512, n_sub=2).reshape(rows, d)
    else:
        h1 = _matmul_res_ln([mix.reshape(rows, d)], [w_out], x2, ln1_g, ln1_b, 512)
        q = _matmul(h1, w_mq, 1024, 1024)
        att = _mem_attn_core(q.reshape(bsz, t, d), mk, mv, tq=512, nb=nb_attn)
        h2 = _matmul_res_ln([att.reshape(rows, d)], [w_mo], h1, ln2_g, ln2_b, 512)

    if hist_ffn is None and grp == t:
        y, new_buf_ffn = _ffn_block(h2.reshape(bsz, t, d), w_up, w_down, w_ffn_conv, b_ffn_conv,
                                    ln3_g, ln3_b, tm=512, n_split=2)
        return y, new_hg, new_gd, new_buf_gd, new_buf_ffn
    h2_3 = h2.reshape(ffn_streams, rows // ffn_streams, d)
    hist3 = None if hist_ffn is None else hist_ffn.reshape(ffn_streams, rows // ffn_streams, D_FF)
    act, new_buf_ffn = _ffn_up(h2_3, w_up, w_ffn_conv, b_ffn_conv, hist3,
                               tm=512, tn=D_FF // 2, grp=grp, row_lo=row_lo)
    y = _matmul_res_ln([act.reshape(rows, D_FF)], [w_down], h2, ln3_g, ln3_b, 512)
    return y.reshape(bsz, t, d), new_hg, new_gd, new_buf_gd, new_buf_ffn


def _cache_rows(cache):
    b, m, h, d = cache.shape
    return cache.reshape(b, m, h, d // LANES, LANES).transpose(0, 1, 3, 2, 4).reshape(b, m * h * d // LANES, LANES)


def _cache_from_rows(rows):
    b = rows.shape[0]
    halves = MEM_HEAD_DIM // LANES
    return (rows.reshape(b, N_MEM, halves, MEM_HEADS, LANES).transpose(0, 1, 3, 2, 4)
            .reshape(b, N_MEM, MEM_HEADS, MEM_HEAD_DIM))


def kernel(x_prompt, x_sample, state_hgrn, state_gdn, state_gdn_conv, state_ffn_conv, cache_mem_k, cache_mem_v, mem_prompt, hgrn_lb_logits, w_in, w_gd_conv, gd_a_log, gd_dt_bias, hg_norm_g, gd_norm_g, w_out, ln1_g, ln1_b, w_mq, w_mkv, w_mo, ln2_g, ln2_b, w_up, w_ffn_conv, b_ffn_conv, w_down, ln3_g, ln3_b):
    bp, tp, d = x_prompt.shape
    bs, ts, _ = x_sample.shape
    tpad = SUBLANES
    row_lo = tpad - ts
    l = 0

    w_in_b = w_in[l].astype(BF16)
    w_in_tail = jnp.pad(w_in_b[:, PROJ_COLS:], ((0, 0), (0, HEAD_DIM - 2 * N_HEADS)))
    mixer_w = (hgrn_lb_logits, w_in_b, w_in_tail, w_gd_conv[l],
               gd_a_log[l], gd_dt_bias[l], hg_norm_g[l], gd_norm_g[l])
    weights = (mixer_w, w_out[l].astype(BF16), ln1_g[l], ln1_b[l],
               w_mq[l].astype(BF16), w_mo[l].astype(BF16), ln2_g[l], ln2_b[l],
               w_up[l].astype(BF16),
               w_ffn_conv[l], b_ffn_conv[l], w_down[l].astype(BF16), ln3_g[l], ln3_b[l])

    mk = _kv_proj(mem_prompt, w_mkv[l][:, :d].astype(BF16))
    mv = _kv_proj(mem_prompt, w_mkv[l][:, d:].astype(BF16))
    yp, p_hg, p_gd, p_bgd, p_bff = _layer(x_prompt, None, None, None, None, mk, mv, weights,
                                          row_lo=0, grp=tp, ffn_streams=bp, c_mix=128, nc_mix=4, nb_mix=1,
                                          fuse_proj=True, nb_attn=None)

    n_hg = GD_CONV - 1
    n_hf = FFN_CONV - 1
    xs = jnp.pad(x_sample, ((0, 0), (row_lo, 0), (0, 0)))
    hist_gd = jnp.pad(state_gdn_conv[l], ((0, 0), (row_lo - n_hg, ts), (0, 0)))
    hist_ffn = jnp.pad(state_ffn_conv[l], ((0, 0), (row_lo - n_hf, ts), (0, 0)))
    ys, s_hg, s_gd, s_bgd, s_bff = _layer(
        xs, state_hgrn[l], state_gdn[l], hist_gd, hist_ffn,
        _cache_rows(cache_mem_k[l]), _cache_rows(cache_mem_v[l]), weights,
        row_lo=row_lo, grp=tpad, ffn_streams=1, c_mix=tpad, nc_mix=1, nb_mix=8, fuse_proj=False,
        nb_attn=4)
    ys = ys[:, row_lo:]
    s_bff = s_bff.reshape(bs, tpad, D_FF)[:, tpad - n_hf:]

    return (yp, ys, p_hg[None], p_gd[None], p_bgd[None], p_bff[None],
            _cache_from_rows(mk)[None], _cache_from_rows(mv)[None],
            s_hg[None], s_gd[None], s_bgd[None], s_bff[None])
```

```python
import functools

import jax
import jax.numpy as jnp
from jax import lax
from jax.experimental import pallas as pl
from jax.experimental.pallas import tpu as pltpu

F32 = jnp.float32
BF16 = jnp.bfloat16

D_MODEL = 1024
HEAD_DIM = 128
N_HEADS = 4
MIX_HALF = N_HEADS * HEAD_DIM
GD_CONV = 4
FFN_CONV = 3
D_FF = 2816
N_MEM = 256
MEM_HEADS = 4
MEM_HEAD_DIM = D_MODEL // MEM_HEADS
LN_EPS = 1e-5
RMS_EPS = 1e-6
LOG2_E = 1.4426950408889634
DEPTH = 1
ALPHA = (2.0 * DEPTH) ** 0.25

SUBLANES = 8
LANES = 128
VMEM_LIMIT = 56 * 1024 * 1024


def _params(n_axes):
    return pltpu.CompilerParams(dimension_semantics=("arbitrary",) * n_axes,
                                vmem_limit_bytes=VMEM_LIMIT)


def _dot(a, b):
    return jnp.dot(a.astype(BF16), b.astype(BF16), preferred_element_type=F32)


def _dot_nt(a, b):
    return lax.dot_general(a.astype(BF16), b.astype(BF16), (((1,), (1,)), ((), ())),
                           preferred_element_type=F32)


def _dot_tn(a, b):
    return lax.dot_general(a.astype(BF16), b.astype(BF16), (((0,), (0,)), ((), ())),
                           preferred_element_type=F32)


def _split3(x):
    x1 = x.astype(BF16)
    r = x - x1.astype(F32)
    x2 = r.astype(BF16)
    x3 = (r - x2.astype(F32)).astype(BF16)
    return x1, x2, x3


def _cumsum_rows(tri_bf16, x):
    x1, x2, x3 = _split3(x)
    return (jnp.dot(tri_bf16, x1, preferred_element_type=F32)
            + jnp.dot(tri_bf16, x2, preferred_element_type=F32)
            + jnp.dot(tri_bf16, x3, preferred_element_type=F32))


def _multi_dot(a, b, keys, nt=False):
    dot = _dot_nt if nt else _dot
    return {k: dot(a[k], b[k]) for k in keys}


def _silu(x):
    return x * jax.nn.sigmoid(x)


def _mm_kernel(x_ref, w_ref, o_ref, xb_ref):
    @pl.when(pl.program_id(1) == 0)
    def _():
        xb_ref[...] = x_ref[...].astype(BF16)

    o_ref[...] = jnp.dot(xb_ref[...], w_ref[...], preferred_element_type=F32).astype(o_ref.dtype)


def _matmul(x, w, tm, tn, n=None):
    m, k = x.shape
    n = w.shape[1] if n is None else n
    tm = min(tm, m)
    tn = min(tn, n)
    return pl.pallas_call(
        _mm_kernel,
        grid=(m // tm, n // tn),
        in_specs=[pl.BlockSpec((tm, k), lambda i, j: (i, 0)),
                  pl.BlockSpec((k, tn), lambda i, j: (0, j))],
        out_specs=pl.BlockSpec((tm, tn), lambda i, j: (i, j)),
        out_shape=jax.ShapeDtypeStruct((m, n), F32),
        scratch_shapes=[pltpu.VMEM((tm, k), BF16)],
        compiler_params=_params(2),
        name="proj_matmul",
    )(x, w)


ROWS_PER_MEM = MEM_HEADS * MEM_HEAD_DIM // LANES


def _head_rows(h, j):
    return pl.ds(h + MEM_HEADS * j, N_MEM, stride=ROWS_PER_MEM)


def _head_block(ref, b, h):
    return jnp.concatenate([ref[b, _head_rows(h, j), :] for j in range(MEM_HEAD_DIM // LANES)], axis=1)


def _kv_proj_kernel(x_ref, w_ref, o_ref):
    res = jnp.dot(x_ref[0].astype(BF16), w_ref[...], preferred_element_type=F32)
    for h in range(MEM_HEADS):
        for j in range(MEM_HEAD_DIM // LANES):
            c0 = h * MEM_HEAD_DIM + j * LANES
            o_ref[0, _head_rows(h, j), :] = res[:, c0:c0 + LANES]


def _kv_proj(mem3, w):
    bsz, n_mem, d = mem3.shape
    return pl.pallas_call(
        _kv_proj_kernel,
        grid=(bsz,),
        in_specs=[pl.BlockSpec((1, n_mem, d), lambda i: (i, 0, 0)), pl.BlockSpec(w.shape, lambda i: (0, 0))],
        out_specs=pl.BlockSpec((1, n_mem * ROWS_PER_MEM, LANES), lambda i: (i, 0, 0)),
        out_shape=jax.ShapeDtypeStruct((bsz, n_mem * ROWS_PER_MEM, LANES), F32),
        compiler_params=_params(1),
        name="kv_proj",
    )(mem3, w)


def _mm_ln_kernel(*refs, n_in):
    a_refs = refs[:n_in]
    w_refs = refs[n_in:2 * n_in]
    res_ref, g_ref, b_ref, o_ref = refs[2 * n_in:]
    acc = ALPHA * res_ref[...]
    for a_ref, w_ref in zip(a_refs, w_refs):
        acc = acc + jnp.dot(a_ref[...].astype(BF16), w_ref[...], preferred_element_type=F32)
    mu = jnp.mean(acc, axis=-1, keepdims=True)
    xc = acc - mu
    var = jnp.mean(xc * xc, axis=-1, keepdims=True)
    o_ref[...] = xc * lax.rsqrt(var + LN_EPS) * g_ref[...] + b_ref[...]


def _matmul_res_ln(a_list, w_list, res, g, b, tm):
    m, d = res.shape
    tm = min(tm, m)
    n_in = len(a_list)
    in_specs = ([pl.BlockSpec((tm, a.shape[1]), lambda i: (i, 0)) for a in a_list]
                + [pl.BlockSpec(w.shape, lambda i: (0, 0)) for w in w_list]
                + [pl.BlockSpec((tm, d), lambda i: (i, 0)),
                   pl.BlockSpec((1, d), lambda i: (0, 0)),
                   pl.BlockSpec((1, d), lambda i: (0, 0))])
    return pl.pallas_call(
        functools.partial(_mm_ln_kernel, n_in=n_in),
        grid=(m // tm,),
        in_specs=in_specs,
        out_specs=pl.BlockSpec((tm, d), lambda i: (i, 0)),
        out_shape=jax.ShapeDtypeStruct((m, d), F32),
        compiler_params=_params(1),
        name="proj_res_ln",
    )(*a_list, *w_list, res, g.reshape(1, d), b.reshape(1, d))


PROJ_COLS = 8 * MIX_HALF
HG_COL = 0
GD_COL = 4 * MIX_HALF
HGRN_GROUP = 1
GDN_GROUP = 4


def _ref_rows(gc_ref, b, r0, col, m, c):
    blk = 2 * m
    if blk >= SUBLANES:
        parts = [jnp.broadcast_to(gc_ref[b, pl.ds(r0 + j * blk + m - 1, 1), col], (blk, HEAD_DIM))
                 for j in range(c // blk)]
    else:
        sub = lax.broadcasted_iota(jnp.int32, (SUBLANES, HEAD_DIM), 0)
        parts = []
        for i in range(c // SUBLANES):
            tile = None
            for j in range(SUBLANES // blk):
                row = jnp.broadcast_to(gc_ref[b, pl.ds(r0 + i * SUBLANES + j * blk + m - 1, 1), col],
                                       (SUBLANES, HEAD_DIM))
                tile = row if tile is None else jnp.where(sub >= j * blk, row, tile)
            parts.append(tile)
    return parts[0] if len(parts) == 1 else jnp.concatenate(parts, axis=0)


def _hgrn_rows(p_ref, mix_ref, lb, ng, st_ref, gc_ref, *, chunk_ids, c, nb, row_lo):
    rows = lax.broadcasted_iota(jnp.int32, (c, 1), 0)
    ti = lax.broadcasted_iota(jnp.int32, (c, c), 0)
    si = lax.broadcasted_iota(jnp.int32, (c, c), 1)
    tri = (si <= ti).astype(F32).astype(BF16)
    eye = ti == si
    xr = ti ^ si
    cols = [slice(h * HEAD_DIM, (h + 1) * HEAD_DIM) for h in range(N_HEADS)]
    chains = [(b, j, h) for b in range(nb) for j in chunk_ids for h in range(N_HEADS)]

    def proj(b, j, part):
        return p_ref[b, pl.ds(j * c, c), HG_COL + part * MIX_HALF:HG_COL + (part + 1) * MIX_HALF]

    q, k, gc = {}, {}, {}
    for b in range(nb):
        for j in chunk_ids:
            f = lb + (1.0 - lb) * jax.nn.sigmoid(proj(b, j, 1))
            g = jnp.log(f)
            kb = 1.0 - f
            if row_lo:
                g = jnp.where(rows >= row_lo, g, 0.0)
                kb = jnp.where(rows >= row_lo, kb, 0.0)
            qb = _silu(proj(b, j, 0))
            gcb = _cumsum_rows(tri, g)
            gc_ref[b, pl.ds(j * c, c), :] = gcb
            for h in range(N_HEADS):
                q[b, j, h], k[b, j, h], gc[b, j, h] = qb[:, cols[h]], kb[:, cols[h]], gcb[:, cols[h]]

    level_of = jnp.where(eye, 0, -1)
    m = 1
    while m < c:
        level_of = jnp.where((xr >= m) & (xr < 2 * m) & (si < ti), m, level_of)
        m *= 2
    rows_w = lax.broadcasted_iota(jnp.int32, (c, HEAD_DIM), 0)
    q_b = {ch: q[ch].astype(BF16) for ch in chains}
    k_b = {ch: k[ch].astype(BF16) for ch in chains}
    sc = _multi_dot(q_b, k_b, chains, nt=True)
    sc = {ch: jnp.where(level_of == 0, sc[ch], 0.0) for ch in chains}
    m = c // 2
    while m >= 1:
        sign = jnp.where((rows_w & m) != 0, LOG2_E, -LOG2_E)
        qw, kw = {}, {}
        for ch in chains:
            b, j, h = ch
            w = jnp.exp2((gc[ch] - _ref_rows(gc_ref, b, j * c, cols[h], m, c)) * sign).astype(BF16)
            qw[ch], kw[ch] = q_b[ch] * w, k_b[ch] * w
        prod = _multi_dot(qw, kw, chains, nt=True)
        sc = {ch: jnp.where(level_of == m, prod[ch], sc[ch]) for ch in chains}
        m //= 2

    v = {(b, j, h): p_ref[b, pl.ds(j * c, c),
                          HG_COL + 2 * MIX_HALF + h * HEAD_DIM:HG_COL + 2 * MIX_HALF + (h + 1) * HEAD_DIM]
         for b, j, h in chains}
    o_intra = _multi_dot(sc, v, chains)

    for j in chunk_ids:
        group = [(b, j, h) for b in range(nb) for h in range(N_HEADS)]
        st = {ch: st_ref[ch[0], ch[2]] for ch in group}
        o_inter = _multi_dot({ch: q[ch] * jnp.exp(gc[ch]) for ch in group}, st, group, nt=True)
        for ch in group:
            b, _, h = ch
            g_last = gc_ref[b, pl.ds(j * c + c - 1, 1), cols[h]]
            st_ref[b, h] = st[ch] * jnp.exp(g_last) + _dot_tn(v[ch], k[ch] * jnp.exp(g_last - gc[ch]))
        for ch in group:
            b, _, h = ch
            ob = o_intra[ch] + o_inter[ch]
            on = ob * lax.rsqrt(jnp.mean(ob * ob, axis=-1, keepdims=True) + RMS_EPS) * ng
            gate = p_ref[b, pl.ds(j * c, c),
                         HG_COL + 3 * MIX_HALF + h * HEAD_DIM:HG_COL + 3 * MIX_HALF + (h + 1) * HEAD_DIM]
            mix_ref[b, pl.ds(j * c, c), cols[h]] = on * _silu(gate)


def _gdn_rows(p_ref, t_ref, hist_ref, mix_ref, cb_ref, cw_ref, neg_a, dt_bias, ng, s_ref, xp_ref,
              *, c, nc, nb, row_lo):
    n_hist = GD_CONV - 1
    rb = c * nc
    rows = lax.broadcasted_iota(jnp.int32, (rb, 1), 0) % c
    ti = lax.broadcasted_iota(jnp.int32, (c, c), 0)
    si = lax.broadcasted_iota(jnp.int32, (c, c), 1)
    tri = (si <= ti).astype(F32).astype(BF16)
    eye_f = (ti == si).astype(F32)
    incl = si <= ti

    q_all, k_all, v_all, beta_all, la_all = {}, {}, {}, {}, {}
    for b in range(nb):
        qkv = []
        for j in range(3):
            col = slice(j * MIX_HALF, (j + 1) * MIX_HALF)
            x = p_ref[b, :, GD_COL + j * MIX_HALF:GD_COL + (j + 1) * MIX_HALF]
            if hist_ref is not None:
                x = jnp.where((rows >= row_lo - n_hist) & (rows < row_lo), hist_ref[b, :, col], x)
            xp_ref[b, SUBLANES:SUBLANES + rb, col] = x
            y = x * cw_ref[n_hist:n_hist + 1, col]
            for tap in range(n_hist):
                off = SUBLANES - n_hist + tap
                y = y + xp_ref[b, off:off + rb, col] * cw_ref[tap:tap + 1, col]
            qkv.append(_silu(y))
        cb_ref[b] = xp_ref[b, SUBLANES + rb - n_hist:SUBLANES + rb, :]
        xp_ref[b, 0:SUBLANES, :] = xp_ref[b, rb:rb + SUBLANES, :]
        q_all[b], k_all[b], v_all[b] = qkv

        tail = t_ref[b]
        beta_all[b] = jax.nn.sigmoid(tail)
        la_all[b] = neg_a * jax.nn.softplus(tail + dt_bias)
        if row_lo:
            beta_all[b] = jnp.where(rows >= row_lo, beta_all[b], 0.0)
            la_all[b] = jnp.where(rows >= row_lo, la_all[b], 0.0)

    for j0 in range(0, nc, GDN_GROUP):
        _gdn_chunks(list(range(j0, min(nc, j0 + GDN_GROUP))), q_all, k_all, v_all, beta_all, la_all,
                    p_ref, mix_ref, ng, s_ref, c=c, nb=nb)


def _gdn_chunks(chunk_ids, q_all, k_all, v_all, beta_all, la_all, p_ref, mix_ref, ng, s_ref, *, c, nb):
    ti = lax.broadcasted_iota(jnp.int32, (c, c), 0)
    si = lax.broadcasted_iota(jnp.int32, (c, c), 1)
    tri = (si <= ti).astype(F32).astype(BF16)
    eye_f = (ti == si).astype(F32)
    incl = si <= ti
    chains = [(b, j, h) for b in range(nb) for j in chunk_ids for h in range(N_HEADS)]

    gc = {(b, j): _cumsum_rows(tri, la_all[b][j * c:(j + 1) * c]) for b in range(nb) for j in chunk_ids}

    pre = {}
    for ch in chains:
        b, j, h = ch
        rs = slice(j * c, (j + 1) * c)
        col = slice(h * HEAD_DIM, (h + 1) * HEAD_DIM)
        qh, kh, vh = q_all[b][rs, col], k_all[b][rs, col], v_all[b][rs, col]
        qh = qh * lax.rsqrt(jnp.sum(qh * qh, axis=-1, keepdims=True) + RMS_EPS) * (HEAD_DIM ** -0.5)
        kh = kh * lax.rsqrt(jnp.sum(kh * kh, axis=-1, keepdims=True) + RMS_EPS)
        beta = beta_all[b][rs, h:h + 1]
        gcol = gc[b, j][:, N_HEADS + h:N_HEADS + h + 1]
        grow = jnp.sum(eye_f * gcol, axis=0, keepdims=True)
        decay = jnp.where(incl, jnp.exp(jnp.where(incl, gcol - grow, 0.0)), 0.0)
        gamma = jnp.exp(gcol)
        g_last = gcol[c - 1:c, :]
        pre[ch] = dict(
            q=qh, k=kh, beta=beta, decay=decay,
            rhs=jnp.concatenate([(beta * gamma) * kh, beta * vh], axis=1).astype(BF16),
            gq=(gamma * qh).astype(BF16),
            kd=(kh * jnp.exp(g_last - gcol)).astype(BF16),
            eg=jnp.exp(g_last))
    k_b = {ch: pre[ch]["k"].astype(BF16) for ch in chains}
    kk = _multi_dot(k_b, k_b, chains, nt=True)
    qk = _multi_dot({ch: pre[ch]["q"] for ch in chains}, k_b, chains, nt=True)
    a_mat = {ch: (pre[ch]["beta"] * kk[ch] * pre[ch]["decay"]).astype(BF16) for ch in chains}
    aqk = {ch: (qk[ch] * pre[ch]["decay"]).astype(BF16) for ch in chains}

    xr = ti ^ si
    first = (xr < 2) & (si < ti)
    zero_b = jnp.zeros((c, c), BF16)
    t_inv = {ch: (eye_f - jnp.where(first, a_mat[ch], zero_b).astype(F32)).astype(BF16) for ch in chains}
    n = 2
    while n < c:
        lower = (xr < 2 * n) & ((ti & n) != 0) & ((si & n) == 0)
        x = _multi_dot({ch: jnp.where(lower, a_mat[ch], zero_b) for ch in chains}, t_inv, chains)
        tx = _multi_dot(t_inv, x, chains)
        t_inv = {ch: (t_inv[ch].astype(F32) - tx[ch]).astype(BF16) for ch in chains}
        n *= 2
    wu = {ch: _dot(t_inv[ch], pre[ch]["rhs"]) for ch in chains}

    for j in chunk_ids:
        rs = slice(j * c, (j + 1) * c)
        group = [(b, j, h) for b in range(nb) for h in range(N_HEADS)]
        s_old = {ch: s_ref[ch[0], ch[2]] for ch in group}
        ws = _multi_dot({ch: jnp.concatenate([wu[ch][:, :HEAD_DIM].astype(BF16), pre[ch]["gq"]], axis=0)
                         for ch in group}, s_old, group)
        u = {ch: (wu[ch][:, HEAD_DIM:] - ws[ch][:c]).astype(BF16) for ch in group}
        au = _multi_dot(aqk, u, group)
        for ch in group:
            b, _, h = ch
            s_ref[b, h] = pre[ch]["eg"] * s_old[ch] + _dot_tn(pre[ch]["kd"], u[ch])
            o = ws[ch][c:] + au[ch]
            on = o * lax.rsqrt(jnp.mean(o * o, axis=-1, keepdims=True) + RMS_EPS) * ng
            gate = p_ref[b, rs, GD_COL + 3 * MIX_HALF + h * HEAD_DIM:GD_COL + 3 * MIX_HALF + (h + 1) * HEAD_DIM]
            mix_ref[b, rs, MIX_HALF + h * HEAD_DIM:MIX_HALF + (h + 1) * HEAD_DIM] = on * _silu(gate)


def _mixer_kernel(*refs, c, nc, nb, fuse_proj, has_state, row_lo, n_steps):
    refs = list(refs)
    if fuse_proj:
        x_ref, w_ref, wt_ref = refs[:3]
        del refs[:3]
    else:
        p_ref, t_ref = refs[:2]
        del refs[:2]
    lbl_ref, hng_ref, cw_ref, par_ref, gng_ref = refs[:5]
    del refs[:5]
    if has_state:
        hist_ref, shg0_ref, sgd0_ref = refs[:3]
        del refs[:3]
    else:
        hist_ref = shg0_ref = sgd0_ref = None
    mix_ref, shg_ref, sgd_ref, cb_ref, st_ref, s_ref, gc_ref, xp_ref = refs[:8]
    if fuse_proj:
        p_ref, t_ref = refs[8:]
    step = pl.program_id(1)
    pairs = [(b, h) for b in range(nb) for h in range(N_HEADS)]

    @pl.when(step == 0)
    def _():
        for b, h in pairs:
            if has_state:
                st_ref[b, h] = shg0_ref[b, h].T
                s_ref[b, h] = sgd0_ref[b, h]
            else:
                st_ref[b, h] = jnp.zeros((HEAD_DIM, HEAD_DIM), F32)
                s_ref[b, h] = jnp.zeros((HEAD_DIM, HEAD_DIM), F32)
        for b in range(nb):
            xp_ref[b, 0:SUBLANES, :] = jnp.zeros((SUBLANES, 3 * MIX_HALF), F32)

    if fuse_proj:
        for b in range(nb):
            xb = x_ref[b].astype(BF16)
            p_ref[b] = jnp.dot(xb, w_ref[...], preferred_element_type=F32)
            t_ref[b] = jnp.dot(xb, wt_ref[...], preferred_element_type=F32)

    lbl = lbl_ref[...]
    e = jnp.exp(lbl - jnp.max(lbl, axis=0, keepdims=True))
    lb = e[0:1] / jnp.sum(e, axis=0, keepdims=True)
    neg_a = -jnp.exp(par_ref[0:1, :])
    dt_bias = par_ref[1:2, :]

    for j0 in range(0, nc, HGRN_GROUP):
        _hgrn_rows(p_ref, mix_ref, lb, hng_ref[...], st_ref, gc_ref,
                   chunk_ids=list(range(j0, min(nc, j0 + HGRN_GROUP))), c=c, nb=nb, row_lo=row_lo)
    _gdn_rows(p_ref, t_ref, hist_ref, mix_ref, cb_ref, cw_ref, neg_a, dt_bias, gng_ref[...], s_ref, xp_ref,
              c=c, nc=nc, nb=nb, row_lo=row_lo)

    @pl.when(step == n_steps - 1)
    def _():
        for b, h in pairs:
            shg_ref[b, h] = st_ref[b, h].T
            sgd_ref[b, h] = s_ref[b, h]


def _mixer(x3, proj, tail, weights, hist, s_hg, s_gd, *, c, nc, nb, row_lo):
    lb_logits, w_main, w_tail, conv_w, a_log, dt_bias, hg_norm_g, gd_norm_g = weights
    fuse_proj = proj is None
    bsz, t = (x3 if fuse_proj else proj).shape[:2]
    rb = c * nc
    n_steps = t // rb
    has_state = s_hg is not None
    n_hist = GD_CONV - 1
    const = lambda shape: pl.BlockSpec(shape, lambda i, si: (0,) * len(shape))
    rows3 = lambda width: pl.BlockSpec((nb, rb, width), lambda i, si: (i, si, 0))
    st_spec = pl.BlockSpec((nb, N_HEADS, HEAD_DIM, HEAD_DIM), lambda i, si: (i, 0, 0, 0))
    par = jnp.zeros((SUBLANES, HEAD_DIM), F32)
    par = par.at[0, N_HEADS:2 * N_HEADS].set(a_log).at[1, N_HEADS:2 * N_HEADS].set(dt_bias)
    if fuse_proj:
        in_specs = [rows3(D_MODEL), const((D_MODEL, PROJ_COLS)), const(w_tail.shape)]
        args = [x3, w_main, w_tail]
    else:
        in_specs = [rows3(PROJ_COLS), rows3(HEAD_DIM)]
        args = [proj, tail]
    in_specs += [const(lb_logits.shape), const((1, HEAD_DIM)), const(conv_w.shape), const(par.shape),
                 const((1, HEAD_DIM))]
    args += [lb_logits, hg_norm_g.reshape(1, HEAD_DIM), conv_w, par, gd_norm_g.reshape(1, HEAD_DIM)]
    if has_state:
        in_specs += [rows3(3 * MIX_HALF), st_spec, st_spec]
        args += [hist, s_hg, s_gd]
    scratch = [pltpu.VMEM((nb, N_HEADS, HEAD_DIM, HEAD_DIM), F32),
               pltpu.VMEM((nb, N_HEADS, HEAD_DIM, HEAD_DIM), F32),
               pltpu.VMEM((nb, rb, MIX_HALF), F32),
               pltpu.VMEM((nb, SUBLANES + rb, 3 * MIX_HALF), F32)]
    if fuse_proj:
        scratch += [pltpu.VMEM((nb, rb, PROJ_COLS), F32), pltpu.VMEM((nb, rb, HEAD_DIM), F32)]
    return pl.pallas_call(
        functools.partial(_mixer_kernel, c=c, nc=nc, nb=nb, fuse_proj=fuse_proj, has_state=has_state,
                          row_lo=row_lo, n_steps=n_steps),
        grid=(bsz // nb, n_steps),
        in_specs=in_specs,
        out_specs=[rows3(D_MODEL), st_spec, st_spec,
                   pl.BlockSpec((nb, n_hist, 3 * MIX_HALF), lambda i, si: (i, 0, 0))],
        out_shape=[jax.ShapeDtypeStruct((bsz, t, D_MODEL), F32),
                   jax.ShapeDtypeStruct((bsz, N_HEADS, HEAD_DIM, HEAD_DIM), F32),
                   jax.ShapeDtypeStruct((bsz, N_HEADS, HEAD_DIM, HEAD_DIM), F32),
                   jax.ShapeDtypeStruct((bsz, n_hist, 3 * MIX_HALF), F32)],
        scratch_shapes=scratch,
        compiler_params=_params(2),
        name="mixer",
    )(*args)


def _attn_rows(q_ref, k_ref, v_ref, o_ref):
    nb = q_ref.shape[0]
    scale = MEM_HEAD_DIM ** -0.5
    pairs = [(b, h) for b in range(nb) for h in range(MEM_HEADS)]
    s = {bh: _dot_nt(q_ref[bh[0], :, bh[1] * MEM_HEAD_DIM:(bh[1] + 1) * MEM_HEAD_DIM],
                     _head_block(k_ref, *bh)) * scale for bh in pairs}
    p = {}
    for bh in pairs:
        e = jnp.exp(s[bh] - jnp.max(s[bh], axis=-1, keepdims=True))
        p[bh] = e / jnp.sum(e, axis=-1, keepdims=True)
    for b, h in pairs:
        o_ref[b, :, h * MEM_HEAD_DIM:(h + 1) * MEM_HEAD_DIM] = _dot(p[b, h], _head_block(v_ref, b, h))


def _attn_block_kernel(mix_ref, x_ref, k_ref, v_ref, wo_ref, wq_ref, wm_ref, g1_ref, b1_ref, g2_ref, b2_ref,
                       sq_ref, sk_ref, sv_ref, o_ref, so_ref, *, n_sub):
    _attn_rows(sq_ref, sk_ref, sv_ref, so_ref)
    def layer_norm(acc, g_ref, b_ref):
        mu = jnp.mean(acc, axis=-1, keepdims=True)
        xc = acc - mu
        var = jnp.mean(xc * xc, axis=-1, keepdims=True)
        return xc * lax.rsqrt(var + LN_EPS) * g_ref[...] + b_ref[...]

    ts = x_ref.shape[1] // n_sub
    subs = [pl.ds(i * ts, ts) for i in range(n_sub)]
    scale = MEM_HEAD_DIM ** -0.5
    cols = [slice(h * MEM_HEAD_DIM, (h + 1) * MEM_HEAD_DIM) for h in range(MEM_HEADS)]
    heads = range(MEM_HEADS)
    kh = [_head_block(k_ref, 0, h).astype(BF16) for h in heads]
    vh = [_head_block(v_ref, 0, h).astype(BF16) for h in heads]
    h1 = [layer_norm(ALPHA * x_ref[0, r, :]
                     + jnp.dot(mix_ref[0, r, :].astype(BF16), wo_ref[...], preferred_element_type=F32),
                     g1_ref, b1_ref) for r in subs]
    q = [jnp.dot(hi.astype(BF16), wq_ref[...], preferred_element_type=F32).astype(BF16) for hi in h1]
    s = [[_dot_nt(qi[:, cols[h]], kh[h]) * scale for h in heads] for qi in q]
    p = []
    for si in s:
        pi = []
        for sh in si:
            e = jnp.exp(sh - jnp.max(sh, axis=-1, keepdims=True))
            pi.append(e / jnp.sum(e, axis=-1, keepdims=True))
        p.append(pi)
    att = [jnp.concatenate([_dot(pi[h], vh[h]).astype(BF16) for h in heads], axis=1) for pi in p]
    for r, hi, ai in zip(subs, h1, att):
        o_ref[0, r, :] = layer_norm(ALPHA * hi + jnp.dot(ai, wm_ref[...], preferred_element_type=F32),
                                    g2_ref, b2_ref)


def _attn_block(mix3, x3, mk, mv, w_out, w_mq, w_mo, ln1_g, ln1_b, ln2_g, ln2_b, side_q, side_k, side_v,
                *, tm, n_sub):
    bsz, t, d = x3.shape
    tm = min(tm, t)
    n_steps = bsz * (t // tm)
    bs, ts, _ = side_q.shape
    assert bs % n_steps == 0
    nb = bs // n_steps
    flat = lambda i, r: (i * (t // tm) + r, 0, 0)
    side_rows = pl.BlockSpec((nb, ts, d), flat)
    side_kv = pl.BlockSpec((nb,) + side_k.shape[1:], flat)
    rows = pl.BlockSpec((1, tm, d), lambda i, r: (i, r, 0))
    kv = pl.BlockSpec((1,) + mk.shape[1:], lambda i, r: (i, 0, 0))
    wsp = pl.BlockSpec((d, d), lambda i, r: (0, 0))
    vec = pl.BlockSpec((1, d), lambda i, r: (0, 0))
    return pl.pallas_call(
        functools.partial(_attn_block_kernel, n_sub=n_sub),
        grid=(bsz, t // tm),
        in_specs=[rows, rows, kv, kv, wsp, wsp, wsp, vec, vec, vec, vec, side_rows, side_kv, side_kv],
        out_specs=[rows, side_rows],
        out_shape=[jax.ShapeDtypeStruct((bsz, t, d), F32), jax.ShapeDtypeStruct((bs, ts, d), F32)],
        compiler_params=_params(2),
        name="attn_block",
    )(mix3, x3, mk, mv, w_out, w_mq, w_mo, ln1_g.reshape(1, d), ln1_b.reshape(1, d),
      ln2_g.reshape(1, d), ln2_b.reshape(1, d), side_q, side_k, side_v)


def _ffn_up_kernel(*refs, tm, grp, row_lo, has_hist):
    if has_hist:
        h_ref, wg_ref, wv_ref, cw_ref, cb_ref, hist_ref, act_ref, buf_ref, xp_ref = refs
    else:
        h_ref, wg_ref, wv_ref, cw_ref, cb_ref, act_ref, buf_ref, xp_ref = refs
        hist_ref = None
    n_hist = FFN_CONV - 1
    r = pl.program_id(2)

    @pl.when(r == 0)
    def _():
        xp_ref[0:SUBLANES, :] = jnp.zeros((SUBLANES, xp_ref.shape[1]), F32)

    hb = h_ref[0].astype(BF16)
    gate = jnp.dot(hb, wg_ref[...], preferred_element_type=F32)
    val = jnp.dot(hb, wv_ref[...], preferred_element_type=F32)
    if hist_ref is not None:
        pos = lax.broadcasted_iota(jnp.int32, (tm, 1), 0) % grp
        gate = jnp.where((pos >= row_lo - n_hist) & (pos < row_lo), hist_ref[0], gate)
    xp_ref[SUBLANES:SUBLANES + tm, :] = gate
    y = gate * cw_ref[n_hist:n_hist + 1, :] + cb_ref[...]
    for tap in range(n_hist):
        off = SUBLANES - n_hist + tap
        y = y + xp_ref[off:off + tm, :] * cw_ref[tap:tap + 1, :]
    gelu = 0.5 * y * (1.0 + lax.erf(y * (2.0 ** -0.5)))
    act_ref[0] = (gelu * val).astype(act_ref.dtype)
    if buf_ref.shape[1] == n_hist:
        buf_ref[0] = xp_ref[SUBLANES + tm - n_hist:SUBLANES + tm, :]
    else:
        buf_ref[0] = gate
    xp_ref[0:SUBLANES, :] = xp_ref[tm:tm + SUBLANES, :]


def _ffn_up(h3, w_up, conv_w, conv_b, hist, *, tm, tn, grp, row_lo):
    g, r, d = h3.shape
    f = w_up.shape[1] // 2
    tm = min(tm, r)
    n_hist = FFN_CONV - 1
    has_hist = hist is not None
    assert grp == r or grp <= tm
    if grp == r:
        buf_spec = pl.BlockSpec((1, n_hist, tn), lambda gi, fi, ri: (gi, 0, fi))
        buf_rows = n_hist
    else:
        buf_spec = pl.BlockSpec((1, tm, tn), lambda gi, fi, ri: (gi, ri, fi))
        buf_rows = r
    in_specs = [pl.BlockSpec((1, tm, d), lambda gi, fi, ri: (gi, ri, 0)),
                pl.BlockSpec((d, tn), lambda gi, fi, ri: (0, fi)),
                pl.BlockSpec((d, tn), lambda gi, fi, ri: (0, fi + f // tn)),
                pl.BlockSpec((FFN_CONV, tn), lambda gi, fi, ri: (0, fi)),
                pl.BlockSpec((1, tn), lambda gi, fi, ri: (0, fi))]
    args = [h3, w_up, w_up, conv_w, conv_b.reshape(1, f)]
    if has_hist:
        in_specs.append(pl.BlockSpec((1, tm, tn), lambda gi, fi, ri: (gi, ri, fi)))
        args.append(hist)
    return pl.pallas_call(
        functools.partial(_ffn_up_kernel, tm=tm, grp=grp, row_lo=row_lo, has_hist=has_hist),
        grid=(g, f // tn, r // tm),
        in_specs=in_specs,
        out_specs=[pl.BlockSpec((1, tm, tn), lambda gi, fi, ri: (gi, ri, fi)),
                   buf_spec],
        out_shape=[jax.ShapeDtypeStruct((g, r, f), BF16),
                   jax.ShapeDtypeStruct((g, buf_rows, f), F32)],
        scratch_shapes=[pltpu.VMEM((SUBLANES + tm, tn), F32)],
        compiler_params=_params(3),
        name="ffn_up",
    )(*args)


def _ffn_block_kernel(h_ref, wg_ref, wv_ref, wd_ref, cw_ref, cb_ref, g_ref, b_ref, y_ref, buf_ref, xp_ref,
                      *, tm, n_split):
    n_hist = FFN_CONV - 1
    f = xp_ref.shape[1]
    fs = f // n_split

    @pl.when(pl.program_id(1) == 0)
    def _():
        xp_ref[0:SUBLANES, :] = jnp.zeros((SUBLANES, f), F32)

    h = h_ref[0]
    hb = h.astype(BF16)
    cols = [slice(j * fs, (j + 1) * fs) for j in range(n_split)]
    gate = [jnp.dot(hb, wg_ref[:, c], preferred_element_type=F32) for c in cols]
    val = [jnp.dot(hb, wv_ref[:, c], preferred_element_type=F32) for c in cols]
    act = []
    for c, gj, vj in zip(cols, gate, val):
        xp_ref[SUBLANES:SUBLANES + tm, c] = gj
        y = gj * cw_ref[n_hist:n_hist + 1, c] + cb_ref[:, c]
        for tap in range(n_hist):
            off = SUBLANES - n_hist + tap
            y = y + xp_ref[off:off + tm, c] * cw_ref[tap:tap + 1, c]
        gelu = 0.5 * y * (1.0 + lax.erf(y * (2.0 ** -0.5)))
        act.append((gelu * vj).astype(BF16))
    acc = ALPHA * h
    for c, aj in zip(cols, act):
        acc = acc + jnp.dot(aj, wd_ref[c, :], preferred_element_type=F32)
    mu = jnp.mean(acc, axis=-1, keepdims=True)
    xc = acc - mu
    var = jnp.mean(xc * xc, axis=-1, keepdims=True)
    y_ref[0] = xc * lax.rsqrt(var + LN_EPS) * g_ref[...] + b_ref[...]
    buf_ref[0] = xp_ref[SUBLANES + tm - n_hist:SUBLANES + tm, :]
    xp_ref[0:SUBLANES, :] = xp_ref[tm:tm + SUBLANES, :]


def _ffn_block(h3, w_up, w_down, conv_w, conv_b, ln_g, ln_b, *, tm, n_split):
    bsz, t, d = h3.shape
    f = w_down.shape[0]
    tm = min(tm, t)
    n_hist = FFN_CONV - 1
    once = pl.Buffered(1)
    const = lambda shape: pl.BlockSpec(shape, lambda i, r: (0,) * len(shape), pipeline_mode=once)
    rows = pl.BlockSpec((1, tm, d), lambda i, r: (i, r, 0))
    return pl.pallas_call(
        functools.partial(_ffn_block_kernel, tm=tm, n_split=n_split),
        grid=(bsz, t // tm),
        in_specs=[rows, const((d, f)), pl.BlockSpec((d, f), lambda i, r: (0, 1), pipeline_mode=once),
                  const((f, d)), const((FFN_CONV, f)), const((1, f)),
                  const((1, d)), const((1, d))],
        out_specs=[rows, pl.BlockSpec((1, n_hist, f), lambda i, r: (i, 0, 0))],
        out_shape=[jax.ShapeDtypeStruct((bsz, t, d), F32), jax.ShapeDtypeStruct((bsz, n_hist, f), F32)],
        scratch_shapes=[pltpu.VMEM((SUBLANES + tm, f), F32)],
        compiler_params=_params(2),
        name="ffn_block",
    )(h3, w_up, w_up, w_down, conv_w, conv_b.reshape(1, f), ln_g.reshape(1, d), ln_b.reshape(1, d))


def _cache_rows(cache):
    b, m, h, d = cache.shape
    return cache.reshape(b, m, h, d // LANES, LANES).transpose(0, 1, 3, 2, 4).reshape(b, m * h * d // LANES, LANES)


def _cache_from_rows(rows):
    b = rows.shape[0]
    halves = MEM_HEAD_DIM // LANES
    return (rows.reshape(b, N_MEM, halves, MEM_HEADS, LANES).transpose(0, 1, 3, 2, 4)
            .reshape(b, N_MEM, MEM_HEADS, MEM_HEAD_DIM))


def kernel(x_prompt, x_sample, state_hgrn, state_gdn, state_gdn_conv, state_ffn_conv, cache_mem_k, cache_mem_v, mem_prompt, hgrn_lb_logits, w_in, w_gd_conv, gd_a_log, gd_dt_bias, hg_norm_g, gd_norm_g, w_out, ln1_g, ln1_b, w_mq, w_mkv, w_mo, ln2_g, ln2_b, w_up, w_ffn_conv, b_ffn_conv, w_down, ln3_g, ln3_b):
    bp, tp, d = x_prompt.shape
    bs, ts, _ = x_sample.shape
    tpad = SUBLANES
    row_lo = tpad - ts
    l = 0

    w_in_b = w_in[l].astype(BF16)
    w_in_tail = jnp.pad(w_in_b[:, PROJ_COLS:], ((0, 0), (0, HEAD_DIM - 2 * N_HEADS)))
    mixer_w = (hgrn_lb_logits, w_in_b, w_in_tail, w_gd_conv[l],
               gd_a_log[l], gd_dt_bias[l], hg_norm_g[l], gd_norm_g[l])
    (w_out_b, w_mq_b, w_mo_b, w_up_b, w_down_b) = (w.astype(BF16) for w in
                                                    (w_out[l], w_mq[l], w_mo[l], w_up[l], w_down[l]))

    n_hg = GD_CONV - 1
    n_hf = FFN_CONV - 1
    xs = jnp.pad(x_sample, ((0, 0), (row_lo, 0), (0, 0)))
    xs2 = xs.reshape(bs * tpad, d)
    hist_gd = jnp.pad(state_gdn_conv[l], ((0, 0), (row_lo - n_hg, ts), (0, 0)))
    hist_ffn = jnp.pad(state_ffn_conv[l], ((0, 0), (row_lo - n_hf, ts), (0, 0)))
    proj_s = _matmul(xs2, w_in_b, 1024, 512, n=PROJ_COLS).reshape(bs, tpad, PROJ_COLS)
    tail_s = _matmul(xs2, w_in_tail, 1024, HEAD_DIM).reshape(bs, tpad, HEAD_DIM)
    mix_s, s_hg, s_gd, s_bgd = _mixer(None, proj_s, tail_s, mixer_w, hist_gd, state_hgrn[l], state_gdn[l],
                                      c=tpad, nc=1, nb=8, row_lo=row_lo)
    h1_s = _matmul_res_ln([mix_s.reshape(bs * tpad, d)], [w_out_b], xs2, ln1_g[l], ln1_b[l], 512)
    q_s = _matmul(h1_s, w_mq_b, 1024, 1024).reshape(bs, tpad, d)

    mk = _kv_proj(mem_prompt, w_mkv[l][:, :d].astype(BF16))
    mv = _kv_proj(mem_prompt, w_mkv[l][:, d:].astype(BF16))
    mix_p, p_hg, p_gd, p_bgd = _mixer(x_prompt, None, None, mixer_w, None, None, None,
                                      c=128, nc=4, nb=1, row_lo=0)
    h2_p, att_s = _attn_block(mix_p, x_prompt, mk, mv, w_out_b, w_mq_b, w_mo_b, ln1_g[l], ln1_b[l],
                              ln2_g[l], ln2_b[l], q_s, _cache_rows(cache_mem_k[l]), _cache_rows(cache_mem_v[l]),
                              tm=512, n_sub=2)
    yp, p_bff = _ffn_block(h2_p, w_up_b, w_down_b, w_ffn_conv[l], b_ffn_conv[l], ln3_g[l], ln3_b[l],
                           tm=512, n_split=2)

    h2_s = _matmul_res_ln([att_s.reshape(bs * tpad, d)], [w_mo_b], h1_s, ln2_g[l], ln2_b[l], 512)
    act_s, gate_s = _ffn_up(h2_s.reshape(1, bs * tpad, d), w_up_b, w_ffn_conv[l], b_ffn_conv[l],
                            hist_ffn.reshape(1, bs * tpad, D_FF), tm=512, tn=D_FF // 2, grp=tpad, row_lo=row_lo)
    ys = _matmul_res_ln([act_s.reshape(bs * tpad, D_FF)], [w_down_b], h2_s, ln3_g[l], ln3_b[l], 512)
    ys = ys.reshape(bs, tpad, d)[:, row_lo:]
    s_bff = gate_s.reshape(bs, tpad, D_FF)[:, tpad - n_hf:]

    return (yp, ys, p_hg[None], p_gd[None], p_bgd[None], p_bff[None],
            _cache_from_rows(mk)[None], _cache_from_rows(mv)[None],
            s_hg[None], s_gd[None], s_bgd[None], s_bff[None])
```

```python
import functools

import jax
import jax.numpy as jnp
from jax import lax
from jax.experimental import pallas as pl
from jax.experimental.pallas import tpu as pltpu

F32 = jnp.float32
BF16 = jnp.bfloat16

D_MODEL = 1024
HEAD_DIM = 128
N_HEADS = 4
MIX_HALF = N_HEADS * HEAD_DIM
GD_CONV = 4
FFN_CONV = 3
D_FF = 2816
N_MEM = 256
MEM_HEADS = 4
MEM_HEAD_DIM = D_MODEL // MEM_HEADS
LN_EPS = 1e-5
RMS_EPS = 1e-6
LOG2_E = 1.4426950408889634
DEPTH = 1
ALPHA = (2.0 * DEPTH) ** 0.25

SUBLANES = 8
LANES = 128
MXU_TILE = 256
VMEM_LIMIT = 56 * 1024 * 1024


def _params(n_axes):
    return pltpu.CompilerParams(dimension_semantics=("arbitrary",) * n_axes,
                                vmem_limit_bytes=VMEM_LIMIT)


def _dot(a, b):
    return jnp.dot(a.astype(BF16), b.astype(BF16), preferred_element_type=F32)


def _dot_nt(a, b):
    return lax.dot_general(a.astype(BF16), b.astype(BF16), (((1,), (1,)), ((), ())),
                           preferred_element_type=F32)


def _dot_tn(a, b):
    return lax.dot_general(a.astype(BF16), b.astype(BF16), (((0,), (0,)), ((), ())),
                           preferred_element_type=F32)


def _split3(x):
    x1 = x.astype(BF16)
    r = x - x1.astype(F32)
    x2 = r.astype(BF16)
    x3 = (r - x2.astype(F32)).astype(BF16)
    return x1, x2, x3


def _cumsum_rows(tri_bf16, x):
    x1, x2, x3 = _split3(x)
    return (jnp.dot(tri_bf16, x1, preferred_element_type=F32)
            + jnp.dot(tri_bf16, x2, preferred_element_type=F32)
            + jnp.dot(tri_bf16, x3, preferred_element_type=F32))


def _multi_dot(a, b, keys, nt=False):
    dot = _dot_nt if nt else _dot
    return {k: dot(a[k], b[k]) for k in keys}


def _silu(x):
    return x * jax.nn.sigmoid(x)


def _mm_kernel(x_ref, w_ref, o_ref, xb_ref):
    @pl.when(pl.program_id(1) == 0)
    def _():
        xb_ref[...] = x_ref[...].astype(BF16)

    o_ref[...] = jnp.dot(xb_ref[...], w_ref[...], preferred_element_type=F32).astype(o_ref.dtype)


def _matmul(x, w, tm, tn, n=None):
    m, k = x.shape
    n = w.shape[1] if n is None else n
    tm = min(tm, m)
    tn = min(tn, n)
    return pl.pallas_call(
        _mm_kernel,
        grid=(m // tm, n // tn),
        in_specs=[pl.BlockSpec((tm, k), lambda i, j: (i, 0)),
                  pl.BlockSpec((k, tn), lambda i, j: (0, j))],
        out_specs=pl.BlockSpec((tm, tn), lambda i, j: (i, j)),
        out_shape=jax.ShapeDtypeStruct((m, n), F32),
        scratch_shapes=[pltpu.VMEM((tm, k), BF16)],
        compiler_params=_params(2),
        name="proj_matmul",
    )(x, w)


ROWS_PER_MEM = MEM_HEADS * MEM_HEAD_DIM // LANES


def _head_rows(h, j):
    return pl.ds(h + MEM_HEADS * j, N_MEM, stride=ROWS_PER_MEM)


def _head_block(ref, b, h):
    return jnp.concatenate([ref[b, _head_rows(h, j), :] for j in range(MEM_HEAD_DIM // LANES)], axis=1)


def _kv_proj_kernel(x_ref, w_ref, o_ref):
    res = jnp.dot(x_ref[0].astype(BF16), w_ref[...], preferred_element_type=F32)
    for h in range(MEM_HEADS):
        for j in range(MEM_HEAD_DIM // LANES):
            c0 = h * MEM_HEAD_DIM + j * LANES
            o_ref[0, _head_rows(h, j), :] = res[:, c0:c0 + LANES]


def _kv_proj(mem3, w):
    bsz, n_mem, d = mem3.shape
    return pl.pallas_call(
        _kv_proj_kernel,
        grid=(bsz,),
        in_specs=[pl.BlockSpec((1, n_mem, d), lambda i: (i, 0, 0)), pl.BlockSpec(w.shape, lambda i: (0, 0))],
        out_specs=pl.BlockSpec((1, n_mem * ROWS_PER_MEM, LANES), lambda i: (i, 0, 0)),
        out_shape=jax.ShapeDtypeStruct((bsz, n_mem * ROWS_PER_MEM, LANES), F32),
        compiler_params=_params(1),
        name="kv_proj",
    )(mem3, w)


def _mm_ln_kernel(*refs, n_in):
    a_refs = refs[:n_in]
    w_refs = refs[n_in:2 * n_in]
    res_ref, g_ref, b_ref, o_ref = refs[2 * n_in:]
    acc = ALPHA * res_ref[...]
    for a_ref, w_ref in zip(a_refs, w_refs):
        acc = acc + jnp.dot(a_ref[...].astype(BF16), w_ref[...], preferred_element_type=F32)
    mu = jnp.mean(acc, axis=-1, keepdims=True)
    xc = acc - mu
    var = jnp.mean(xc * xc, axis=-1, keepdims=True)
    o_ref[...] = xc * lax.rsqrt(var + LN_EPS) * g_ref[...] + b_ref[...]


def _matmul_res_ln(a_list, w_list, res, g, b, tm):
    m, d = res.shape
    tm = min(tm, m)
    n_in = len(a_list)
    in_specs = ([pl.BlockSpec((tm, a.shape[1]), lambda i: (i, 0)) for a in a_list]
                + [pl.BlockSpec(w.shape, lambda i: (0, 0)) for w in w_list]
                + [pl.BlockSpec((tm, d), lambda i: (i, 0)),
                   pl.BlockSpec((1, d), lambda i: (0, 0)),
                   pl.BlockSpec((1, d), lambda i: (0, 0))])
    return pl.pallas_call(
        functools.partial(_mm_ln_kernel, n_in=n_in),
        grid=(m // tm,),
        in_specs=in_specs,
        out_specs=pl.BlockSpec((tm, d), lambda i: (i, 0)),
        out_shape=jax.ShapeDtypeStruct((m, d), F32),
        compiler_params=_params(1),
        name="proj_res_ln",
    )(*a_list, *w_list, res, g.reshape(1, d), b.reshape(1, d))


PROJ_COLS = 8 * MIX_HALF
HG_COL = 0
GD_COL = 4 * MIX_HALF
HGRN_GROUP = 1
GDN_GROUP = 4


def _ref_rows(gc_ref, b, r0, col, m, c):
    blk = 2 * m
    if blk >= SUBLANES:
        parts = [jnp.broadcast_to(gc_ref[b, pl.ds(r0 + j * blk + m - 1, 1), col], (blk, HEAD_DIM))
                 for j in range(c // blk)]
    else:
        sub = lax.broadcasted_iota(jnp.int32, (SUBLANES, HEAD_DIM), 0)
        parts = []
        for i in range(c // SUBLANES):
            tile = None
            for j in range(SUBLANES // blk):
                row = jnp.broadcast_to(gc_ref[b, pl.ds(r0 + i * SUBLANES + j * blk + m - 1, 1), col],
                                       (SUBLANES, HEAD_DIM))
                tile = row if tile is None else jnp.where(sub >= j * blk, row, tile)
            parts.append(tile)
    return parts[0] if len(parts) == 1 else jnp.concatenate(parts, axis=0)


def _hgrn_rows(p_ref, mix_ref, lb, ng, st_ref, gc_ref, *, chunk_ids, c, nb, row_lo):
    rows = lax.broadcasted_iota(jnp.int32, (c, 1), 0)
    ti = lax.broadcasted_iota(jnp.int32, (c, c), 0)
    si = lax.broadcasted_iota(jnp.int32, (c, c), 1)
    tri = (si <= ti).astype(F32).astype(BF16)
    eye = ti == si
    xr = ti ^ si
    cols = [slice(h * HEAD_DIM, (h + 1) * HEAD_DIM) for h in range(N_HEADS)]
    chains = [(b, j, h) for b in range(nb) for j in chunk_ids for h in range(N_HEADS)]

    def proj(b, j, part):
        return p_ref[b, pl.ds(j * c, c), HG_COL + part * MIX_HALF:HG_COL + (part + 1) * MIX_HALF]

    q, k, gc = {}, {}, {}
    for b in range(nb):
        for j in chunk_ids:
            f = lb + (1.0 - lb) * jax.nn.sigmoid(proj(b, j, 1))
            g = jnp.log(f)
            kb = 1.0 - f
            if row_lo:
                g = jnp.where(rows >= row_lo, g, 0.0)
                kb = jnp.where(rows >= row_lo, kb, 0.0)
            qb = _silu(proj(b, j, 0))
            gcb = _cumsum_rows(tri, g)
            gc_ref[b, pl.ds(j * c, c), :] = gcb
            for h in range(N_HEADS):
                q[b, j, h], k[b, j, h], gc[b, j, h] = qb[:, cols[h]], kb[:, cols[h]], gcb[:, cols[h]]

    level_of = jnp.where(eye, 0, -1)
    m = 1
    while m < c:
        level_of = jnp.where((xr >= m) & (xr < 2 * m) & (si < ti), m, level_of)
        m *= 2
    rows_w = lax.broadcasted_iota(jnp.int32, (c, HEAD_DIM), 0)
    q_b = {ch: q[ch].astype(BF16) for ch in chains}
    k_b = {ch: k[ch].astype(BF16) for ch in chains}
    sc = _multi_dot(q_b, k_b, chains, nt=True)
    sc = {ch: jnp.where(level_of == 0, sc[ch], 0.0) for ch in chains}
    m = c // 2
    while m >= 1:
        sign = jnp.where((rows_w & m) != 0, LOG2_E, -LOG2_E)
        qw, kw = {}, {}
        for ch in chains:
            b, j, h = ch
            w = jnp.exp2((gc[ch] - _ref_rows(gc_ref, b, j * c, cols[h], m, c)) * sign).astype(BF16)
            qw[ch], kw[ch] = q_b[ch] * w, k_b[ch] * w
        prod = _multi_dot(qw, kw, chains, nt=True)
        sc = {ch: jnp.where(level_of == m, prod[ch], sc[ch]) for ch in chains}
        m //= 2

    v = {(b, j, h): p_ref[b, pl.ds(j * c, c),
                          HG_COL + 2 * MIX_HALF + h * HEAD_DIM:HG_COL + 2 * MIX_HALF + (h + 1) * HEAD_DIM]
         for b, j, h in chains}
    o_intra = _multi_dot(sc, v, chains)

    for j in chunk_ids:
        group = [(b, j, h) for b in range(nb) for h in range(N_HEADS)]
        st = {ch: st_ref[ch[0], ch[2]] for ch in group}
        o_inter = _multi_dot({ch: q[ch] * jnp.exp(gc[ch]) for ch in group}, st, group, nt=True)
        for ch in group:
            b, _, h = ch
            g_last = gc_ref[b, pl.ds(j * c + c - 1, 1), cols[h]]
            st_ref[b, h] = st[ch] * jnp.exp(g_last) + _dot_tn(v[ch], k[ch] * jnp.exp(g_last - gc[ch]))
        for ch in group:
            b, _, h = ch
            ob = o_intra[ch] + o_inter[ch]
            on = ob * lax.rsqrt(jnp.mean(ob * ob, axis=-1, keepdims=True) + RMS_EPS) * ng
            gate = p_ref[b, pl.ds(j * c, c),
                         HG_COL + 3 * MIX_HALF + h * HEAD_DIM:HG_COL + 3 * MIX_HALF + (h + 1) * HEAD_DIM]
            mix_ref[b, pl.ds(j * c, c), cols[h]] = on * _silu(gate)


def _gdn_rows(p_ref, t_ref, hist_ref, mix_ref, cb_ref, cw_ref, neg_a, dt_bias, ng, s_ref, xp_ref,
              *, c, nc, nb, row_lo):
    n_hist = GD_CONV - 1
    rb = c * nc
    rows = lax.broadcasted_iota(jnp.int32, (rb, 1), 0) % c
    ti = lax.broadcasted_iota(jnp.int32, (c, c), 0)
    si = lax.broadcasted_iota(jnp.int32, (c, c), 1)
    tri = (si <= ti).astype(F32).astype(BF16)
    eye_f = (ti == si).astype(F32)
    incl = si <= ti

    q_all, k_all, v_all, beta_all, la_all = {}, {}, {}, {}, {}
    for b in range(nb):
        qkv = []
        for j in range(3):
            col = slice(j * MIX_HALF, (j + 1) * MIX_HALF)
            x = p_ref[b, :, GD_COL + j * MIX_HALF:GD_COL + (j + 1) * MIX_HALF]
            if hist_ref is not None:
                x = jnp.where((rows >= row_lo - n_hist) & (rows < row_lo), hist_ref[b, :, col], x)
            xp_ref[b, SUBLANES:SUBLANES + rb, col] = x
            y = x * cw_ref[n_hist:n_hist + 1, col]
            for tap in range(n_hist):
                off = SUBLANES - n_hist + tap
                y = y + xp_ref[b, off:off + rb, col] * cw_ref[tap:tap + 1, col]
            qkv.append(_silu(y))
        cb_ref[b] = xp_ref[b, SUBLANES + rb - n_hist:SUBLANES + rb, :]
        xp_ref[b, 0:SUBLANES, :] = xp_ref[b, rb:rb + SUBLANES, :]
        q_all[b], k_all[b], v_all[b] = qkv

        tail = t_ref[b]
        beta_all[b] = jax.nn.sigmoid(tail)
        la_all[b] = neg_a * jax.nn.softplus(tail + dt_bias)
        if row_lo:
            beta_all[b] = jnp.where(rows >= row_lo, beta_all[b], 0.0)
            la_all[b] = jnp.where(rows >= row_lo, la_all[b], 0.0)

    for j0 in range(0, nc, GDN_GROUP):
        _gdn_chunks(list(range(j0, min(nc, j0 + GDN_GROUP))), q_all, k_all, v_all, beta_all, la_all,
                    p_ref, mix_ref, ng, s_ref, c=c, nb=nb)


def _gdn_chunks(chunk_ids, q_all, k_all, v_all, beta_all, la_all, p_ref, mix_ref, ng, s_ref, *, c, nb):
    ti = lax.broadcasted_iota(jnp.int32, (c, c), 0)
    si = lax.broadcasted_iota(jnp.int32, (c, c), 1)
    tri = (si <= ti).astype(F32).astype(BF16)
    eye_f = (ti == si).astype(F32)
    incl = si <= ti
    chains = [(b, j, h) for b in range(nb) for j in chunk_ids for h in range(N_HEADS)]

    gc = {(b, j): _cumsum_rows(tri, la_all[b][j * c:(j + 1) * c]) for b in range(nb) for j in chunk_ids}

    pre = {}
    for ch in chains:
        b, j, h = ch
        rs = slice(j * c, (j + 1) * c)
        col = slice(h * HEAD_DIM, (h + 1) * HEAD_DIM)
        qh, kh, vh = q_all[b][rs, col], k_all[b][rs, col], v_all[b][rs, col]
        qh = qh * lax.rsqrt(jnp.sum(qh * qh, axis=-1, keepdims=True) + RMS_EPS) * (HEAD_DIM ** -0.5)
        kh = kh * lax.rsqrt(jnp.sum(kh * kh, axis=-1, keepdims=True) + RMS_EPS)
        beta = beta_all[b][rs, h:h + 1]
        gcol = gc[b, j][:, N_HEADS + h:N_HEADS + h + 1]
        grow = jnp.sum(eye_f * gcol, axis=0, keepdims=True)
        decay = jnp.where(incl, jnp.exp(jnp.where(incl, gcol - grow, 0.0)), 0.0)
        gamma = jnp.exp(gcol)
        g_last = gcol[c - 1:c, :]
        pre[ch] = dict(
            q=qh, k=kh, beta=beta, decay=decay,
            rhs=jnp.concatenate([(beta * gamma) * kh, beta * vh], axis=1).astype(BF16),
            gq=(gamma * qh).astype(BF16),
            kd=(kh * jnp.exp(g_last - gcol)).astype(BF16),
            eg=jnp.exp(g_last))
    k_b = {ch: pre[ch]["k"].astype(BF16) for ch in chains}
    kk = _multi_dot(k_b, k_b, chains, nt=True)
    qk = _multi_dot({ch: pre[ch]["q"] for ch in chains}, k_b, chains, nt=True)
    a_mat = {ch: (pre[ch]["beta"] * kk[ch] * pre[ch]["decay"]).astype(BF16) for ch in chains}
    aqk = {ch: (qk[ch] * pre[ch]["decay"]).astype(BF16) for ch in chains}

    xr = ti ^ si
    first = (xr < 2) & (si < ti)
    zero_b = jnp.zeros((c, c), BF16)
    t_inv = {ch: (eye_f - jnp.where(first, a_mat[ch], zero_b).astype(F32)).astype(BF16) for ch in chains}
    n = 2
    while n < c:
        lower = (xr < 2 * n) & ((ti & n) != 0) & ((si & n) == 0)
        x = _multi_dot({ch: jnp.where(lower, a_mat[ch], zero_b) for ch in chains}, t_inv, chains)
        tx = _multi_dot(t_inv, x, chains)
        t_inv = {ch: (t_inv[ch].astype(F32) - tx[ch]).astype(BF16) for ch in chains}
        n *= 2
    wu = {ch: _dot(t_inv[ch], pre[ch]["rhs"]) for ch in chains}

    for j in chunk_ids:
        rs = slice(j * c, (j + 1) * c)
        group = [(b, j, h) for b in range(nb) for h in range(N_HEADS)]
        s_old = {ch: s_ref[ch[0], ch[2]] for ch in group}
        ws = _multi_dot({ch: jnp.concatenate([wu[ch][:, :HEAD_DIM].astype(BF16), pre[ch]["gq"]], axis=0)
                         for ch in group}, s_old, group)
        u = {ch: (wu[ch][:, HEAD_DIM:] - ws[ch][:c]).astype(BF16) for ch in group}
        au = _multi_dot(aqk, u, group)
        for ch in group:
            b, _, h = ch
            s_ref[b, h] = pre[ch]["eg"] * s_old[ch] + _dot_tn(pre[ch]["kd"], u[ch])
            o = ws[ch][c:] + au[ch]
            on = o * lax.rsqrt(jnp.mean(o * o, axis=-1, keepdims=True) + RMS_EPS) * ng
            gate = p_ref[b, rs, GD_COL + 3 * MIX_HALF + h * HEAD_DIM:GD_COL + 3 * MIX_HALF + (h + 1) * HEAD_DIM]
            mix_ref[b, rs, MIX_HALF + h * HEAD_DIM:MIX_HALF + (h + 1) * HEAD_DIM] = on * _silu(gate)


def _mixer_kernel(*refs, c, nc, nb, fuse_proj, has_state, row_lo, n_steps):
    refs = list(refs)
    if fuse_proj:
        x_ref, w_ref, wt_ref = refs[:3]
        del refs[:3]
    else:
        p_ref, t_ref = refs[:2]
        del refs[:2]
    lbl_ref, hng_ref, cw_ref, par_ref, gng_ref = refs[:5]
    del refs[:5]
    if has_state:
        hist_ref, shg0_ref, sgd0_ref = refs[:3]
        del refs[:3]
    else:
        hist_ref = shg0_ref = sgd0_ref = None
    mix_ref, shg_ref, sgd_ref, cb_ref, st_ref, s_ref, gc_ref, xp_ref = refs[:8]
    if fuse_proj:
        p_ref, t_ref = refs[8:]
    step = pl.program_id(1)
    pairs = [(b, h) for b in range(nb) for h in range(N_HEADS)]

    @pl.when(step == 0)
    def _():
        for b, h in pairs:
            if has_state:
                st_ref[b, h] = shg0_ref[b, h].T
                s_ref[b, h] = sgd0_ref[b, h]
            else:
                st_ref[b, h] = jnp.zeros((HEAD_DIM, HEAD_DIM), F32)
                s_ref[b, h] = jnp.zeros((HEAD_DIM, HEAD_DIM), F32)
        for b in range(nb):
            xp_ref[b, 0:SUBLANES, :] = jnp.zeros((SUBLANES, 3 * MIX_HALF), F32)

    if fuse_proj:
        for b in range(nb):
            xb = x_ref[b].astype(BF16)
            p_ref[b] = jnp.dot(xb, w_ref[...], preferred_element_type=F32)
            t_ref[b] = jnp.dot(xb, wt_ref[...], preferred_element_type=F32)

    lbl = lbl_ref[...]
    e = jnp.exp(lbl - jnp.max(lbl, axis=0, keepdims=True))
    lb = e[0:1] / jnp.sum(e, axis=0, keepdims=True)
    neg_a = -jnp.exp(par_ref[0:1, :])
    dt_bias = par_ref[1:2, :]

    for j0 in range(0, nc, HGRN_GROUP):
        _hgrn_rows(p_ref, mix_ref, lb, hng_ref[...], st_ref, gc_ref,
                   chunk_ids=list(range(j0, min(nc, j0 + HGRN_GROUP))), c=c, nb=nb, row_lo=row_lo)
    _gdn_rows(p_ref, t_ref, hist_ref, mix_ref, cb_ref, cw_ref, neg_a, dt_bias, gng_ref[...], s_ref, xp_ref,
              c=c, nc=nc, nb=nb, row_lo=row_lo)

    @pl.when(step == n_steps - 1)
    def _():
        for b, h in pairs:
            shg_ref[b, h] = st_ref[b, h].T
            sgd_ref[b, h] = s_ref[b, h]


def _mixer(x3, proj, tail, weights, hist, s_hg, s_gd, *, c, nc, nb, row_lo):
    lb_logits, w_main, w_tail, conv_w, a_log, dt_bias, hg_norm_g, gd_norm_g = weights
    fuse_proj = proj is None
    bsz, t = (x3 if fuse_proj else proj).shape[:2]
    rb = c * nc
    n_steps = t // rb
    has_state = s_hg is not None
    n_hist = GD_CONV - 1
    const = lambda shape: pl.BlockSpec(shape, lambda i, si: (0,) * len(shape))
    rows3 = lambda width: pl.BlockSpec((nb, rb, width), lambda i, si: (i, si, 0))
    st_spec = pl.BlockSpec((nb, N_HEADS, HEAD_DIM, HEAD_DIM), lambda i, si: (i, 0, 0, 0))
    par = jnp.zeros((SUBLANES, HEAD_DIM), F32)
    par = par.at[0, N_HEADS:2 * N_HEADS].set(a_log).at[1, N_HEADS:2 * N_HEADS].set(dt_bias)
    if fuse_proj:
        in_specs = [rows3(D_MODEL), const((D_MODEL, PROJ_COLS)), const(w_tail.shape)]
        args = [x3, w_main, w_tail]
    else:
        in_specs = [rows3(PROJ_COLS), rows3(HEAD_DIM)]
        args = [proj, tail]
    in_specs += [const(lb_logits.shape), const((1, HEAD_DIM)), const(conv_w.shape), const(par.shape),
                 const((1, HEAD_DIM))]
    args += [lb_logits, hg_norm_g.reshape(1, HEAD_DIM), conv_w, par, gd_norm_g.reshape(1, HEAD_DIM)]
    if has_state:
        in_specs += [rows3(3 * MIX_HALF), st_spec, st_spec]
        args += [hist, s_hg, s_gd]
    scratch = [pltpu.VMEM((nb, N_HEADS, HEAD_DIM, HEAD_DIM), F32),
               pltpu.VMEM((nb, N_HEADS, HEAD_DIM, HEAD_DIM), F32),
               pltpu.VMEM((nb, rb, MIX_HALF), F32),
               pltpu.VMEM((nb, SUBLANES + rb, 3 * MIX_HALF), F32)]
    if fuse_proj:
        scratch += [pltpu.VMEM((nb, rb, PROJ_COLS), F32), pltpu.VMEM((nb, rb, HEAD_DIM), F32)]
    return pl.pallas_call(
        functools.partial(_mixer_kernel, c=c, nc=nc, nb=nb, fuse_proj=fuse_proj, has_state=has_state,
                          row_lo=row_lo, n_steps=n_steps),
        grid=(bsz // nb, n_steps),
        in_specs=in_specs,
        out_specs=[rows3(D_MODEL), st_spec, st_spec,
                   pl.BlockSpec((nb, n_hist, 3 * MIX_HALF), lambda i, si: (i, 0, 0))],
        out_shape=[jax.ShapeDtypeStruct((bsz, t, D_MODEL), F32),
                   jax.ShapeDtypeStruct((bsz, N_HEADS, HEAD_DIM, HEAD_DIM), F32),
                   jax.ShapeDtypeStruct((bsz, N_HEADS, HEAD_DIM, HEAD_DIM), F32),
                   jax.ShapeDtypeStruct((bsz, n_hist, 3 * MIX_HALF), F32)],
        scratch_shapes=scratch,
        compiler_params=_params(2),
        name="mixer",
    )(*args)


def _attn_rows(q_ref, k_ref, v_ref, o_ref):
    nb = q_ref.shape[0]
    scale = MEM_HEAD_DIM ** -0.5
    pairs = [(b, h) for b in range(nb) for h in range(MEM_HEADS)]
    s = {bh: _dot_nt(q_ref[bh[0], :, bh[1] * MEM_HEAD_DIM:(bh[1] + 1) * MEM_HEAD_DIM],
                     _head_block(k_ref, *bh)) * scale for bh in pairs}
    p = {}
    for bh in pairs:
        e = jnp.exp(s[bh] - jnp.max(s[bh], axis=-1, keepdims=True))
        p[bh] = e / jnp.sum(e, axis=-1, keepdims=True)
    for b, h in pairs:
        o_ref[b, :, h * MEM_HEAD_DIM:(h + 1) * MEM_HEAD_DIM] = _dot(p[b, h], _head_block(v_ref, b, h))


def _attn_block_kernel(mix_ref, x_ref, k_ref, v_ref, wo_ref, wq_ref, wm_ref, g1_ref, b1_ref, g2_ref, b2_ref,
                       sq_ref, sk_ref, sv_ref, o_ref, so_ref, *, n_sub):
    _attn_rows(sq_ref, sk_ref, sv_ref, so_ref)
    def layer_norm(acc, g_ref, b_ref):
        mu = jnp.mean(acc, axis=-1, keepdims=True)
        xc = acc - mu
        var = jnp.mean(xc * xc, axis=-1, keepdims=True)
        return xc * lax.rsqrt(var + LN_EPS) * g_ref[...] + b_ref[...]

    ts = x_ref.shape[1] // n_sub
    subs = [pl.ds(i * ts, ts) for i in range(n_sub)]
    scale = MEM_HEAD_DIM ** -0.5
    cols = [slice(h * MEM_HEAD_DIM, (h + 1) * MEM_HEAD_DIM) for h in range(MEM_HEADS)]
    heads = range(MEM_HEADS)
    kh = [_head_block(k_ref, 0, h).astype(BF16) for h in heads]
    vh = [_head_block(v_ref, 0, h).astype(BF16) for h in heads]
    h1 = [layer_norm(ALPHA * x_ref[0, r, :]
                     + jnp.dot(mix_ref[0, r, :].astype(BF16), wo_ref[...], preferred_element_type=F32),
                     g1_ref, b1_ref) for r in subs]
    q = [jnp.dot(hi.astype(BF16), wq_ref[...], preferred_element_type=F32).astype(BF16) for hi in h1]
    s = [[_dot_nt(qi[:, cols[h]], kh[h]) * scale for h in heads] for qi in q]
    p = []
    for si in s:
        pi = []
        for sh in si:
            e = jnp.exp(sh - jnp.max(sh, axis=-1, keepdims=True))
            pi.append(e / jnp.sum(e, axis=-1, keepdims=True))
        p.append(pi)
    att = [jnp.concatenate([_dot(pi[h], vh[h]).astype(BF16) for h in heads], axis=1) for pi in p]
    for r, hi, ai in zip(subs, h1, att):
        o_ref[0, r, :] = layer_norm(ALPHA * hi + jnp.dot(ai, wm_ref[...], preferred_element_type=F32),
                                    g2_ref, b2_ref)


def _attn_block(mix3, x3, mk, mv, w_out, w_mq, w_mo, ln1_g, ln1_b, ln2_g, ln2_b, side_q, side_k, side_v,
                *, tm, n_sub):
    bsz, t, d = x3.shape
    tm = min(tm, t)
    n_steps = bsz * (t // tm)
    bs, ts, _ = side_q.shape
    assert bs % n_steps == 0
    nb = bs // n_steps
    flat = lambda i, r: (i * (t // tm) + r, 0, 0)
    side_rows = pl.BlockSpec((nb, ts, d), flat)
    side_kv = pl.BlockSpec((nb,) + side_k.shape[1:], flat)
    rows = pl.BlockSpec((1, tm, d), lambda i, r: (i, r, 0))
    kv = pl.BlockSpec((1,) + mk.shape[1:], lambda i, r: (i, 0, 0))
    wsp = pl.BlockSpec((d, d), lambda i, r: (0, 0))
    vec = pl.BlockSpec((1, d), lambda i, r: (0, 0))
    return pl.pallas_call(
        functools.partial(_attn_block_kernel, n_sub=n_sub),
        grid=(bsz, t // tm),
        in_specs=[rows, rows, kv, kv, wsp, wsp, wsp, vec, vec, vec, vec, side_rows, side_kv, side_kv],
        out_specs=[rows, side_rows],
        out_shape=[jax.ShapeDtypeStruct((bsz, t, d), F32), jax.ShapeDtypeStruct((bs, ts, d), F32)],
        compiler_params=_params(2),
        name="attn_block",
    )(mix3, x3, mk, mv, w_out, w_mq, w_mo, ln1_g.reshape(1, d), ln1_b.reshape(1, d),
      ln2_g.reshape(1, d), ln2_b.reshape(1, d), side_q, side_k, side_v)


def _ffn_up_kernel(*refs, tm, grp, row_lo, has_hist):
    if has_hist:
        h_ref, wg_ref, wv_ref, cw_ref, cb_ref, hist_ref, act_ref, buf_ref, xp_ref = refs
    else:
        h_ref, wg_ref, wv_ref, cw_ref, cb_ref, act_ref, buf_ref, xp_ref = refs
        hist_ref = None
    n_hist = FFN_CONV - 1
    r = pl.program_id(2)

    @pl.when(r == 0)
    def _():
        xp_ref[0:SUBLANES, :] = jnp.zeros((SUBLANES, xp_ref.shape[1]), F32)

    hb = h_ref[0].astype(BF16)
    gate = jnp.dot(hb, wg_ref[...], preferred_element_type=F32)
    val = jnp.dot(hb, wv_ref[...], preferred_element_type=F32)
    if hist_ref is not None:
        pos = lax.broadcasted_iota(jnp.int32, (tm, 1), 0) % grp
        gate = jnp.where((pos >= row_lo - n_hist) & (pos < row_lo), hist_ref[0], gate)
    xp_ref[SUBLANES:SUBLANES + tm, :] = gate
    y = gate * cw_ref[n_hist:n_hist + 1, :] + cb_ref[...]
    for tap in range(n_hist):
        off = SUBLANES - n_hist + tap
        y = y + xp_ref[off:off + tm, :] * cw_ref[tap:tap + 1, :]
    gelu = 0.5 * y * (1.0 + lax.erf(y * (2.0 ** -0.5)))
    act_ref[0] = (gelu * val).astype(act_ref.dtype)
    if buf_ref.shape[1] == n_hist:
        buf_ref[0] = xp_ref[SUBLANES + tm - n_hist:SUBLANES + tm, :]
    else:
        buf_ref[0] = gate
    xp_ref[0:SUBLANES, :] = xp_ref[tm:tm + SUBLANES, :]


def _ffn_up(h3, w_up, conv_w, conv_b, hist, *, tm, tn, grp, row_lo):
    g, r, d = h3.shape
    f = w_up.shape[1] // 2
    tm = min(tm, r)
    n_hist = FFN_CONV - 1
    has_hist = hist is not None
    assert grp == r or grp <= tm
    if grp == r:
        buf_spec = pl.BlockSpec((1, n_hist, tn), lambda gi, fi, ri: (gi, 0, fi))
        buf_rows = n_hist
    else:
        buf_spec = pl.BlockSpec((1, tm, tn), lambda gi, fi, ri: (gi, ri, fi))
        buf_rows = r
    in_specs = [pl.BlockSpec((1, tm, d), lambda gi, fi, ri: (gi, ri, 0)),
                pl.BlockSpec((d, tn), lambda gi, fi, ri: (0, fi)),
                pl.BlockSpec((d, tn), lambda gi, fi, ri: (0, fi + f // tn)),
                pl.BlockSpec((FFN_CONV, tn), lambda gi, fi, ri: (0, fi)),
                pl.BlockSpec((1, tn), lambda gi, fi, ri: (0, fi))]
    args = [h3, w_up, w_up, conv_w, conv_b.reshape(1, f)]
    if has_hist:
        in_specs.append(pl.BlockSpec((1, tm, tn), lambda gi, fi, ri: (gi, ri, fi)))
        args.append(hist)
    return pl.pallas_call(
        functools.partial(_ffn_up_kernel, tm=tm, grp=grp, row_lo=row_lo, has_hist=has_hist),
        grid=(g, f // tn, r // tm),
        in_specs=in_specs,
        out_specs=[pl.BlockSpec((1, tm, tn), lambda gi, fi, ri: (gi, ri, fi)),
                   buf_spec],
        out_shape=[jax.ShapeDtypeStruct((g, r, f), BF16),
                   jax.ShapeDtypeStruct((g, buf_rows, f), F32)],
        scratch_shapes=[pltpu.VMEM((SUBLANES + tm, tn), F32)],
        compiler_params=_params(3),
        name="ffn_up",
    )(*args)


def _ffn_block_kernel(h_ref, wg_ref, wv_ref, wd_ref, cw_ref, cb_ref, g_ref, b_ref, y_ref, buf_ref, xp_ref,
                      *, tm, n_split):
    n_hist = FFN_CONV - 1
    f = xp_ref.shape[1]
    tiles = f // MXU_TILE
    bounds = [MXU_TILE * -(-tiles * j // n_split) for j in range(n_split)] + [f]

    @pl.when(pl.program_id(1) == 0)
    def _():
        xp_ref[0:SUBLANES, :] = jnp.zeros((SUBLANES, f), F32)

    h = h_ref[0]
    hb = h.astype(BF16)
    cols = [slice(bounds[j], bounds[j + 1]) for j in range(n_split)]
    gate = [jnp.dot(hb, wg_ref[:, c], preferred_element_type=F32) for c in cols]
    val = [jnp.dot(hb, wv_ref[:, c], preferred_element_type=F32) for c in cols]
    act = []
    for c, gj, vj in zip(cols, gate, val):
        xp_ref[SUBLANES:SUBLANES + tm, c] = gj
        y = gj * cw_ref[n_hist:n_hist + 1, c] + cb_ref[:, c]
        for tap in range(n_hist):
            off = SUBLANES - n_hist + tap
            y = y + xp_ref[off:off + tm, c] * cw_ref[tap:tap + 1, c]
        gelu = 0.5 * y * (1.0 + lax.erf(y * (2.0 ** -0.5)))
        act.append((gelu * vj).astype(BF16))
    acc = ALPHA * h
    for c, aj in zip(cols, act):
        acc = acc + jnp.dot(aj, wd_ref[c, :], preferred_element_type=F32)
    mu = jnp.mean(acc, axis=-1, keepdims=True)
    xc = acc - mu
    var = jnp.mean(xc * xc, axis=-1, keepdims=True)
    y_ref[0] = xc * lax.rsqrt(var + LN_EPS) * g_ref[...] + b_ref[...]
    buf_ref[0] = xp_ref[SUBLANES + tm - n_hist:SUBLANES + tm, :]
    xp_ref[0:SUBLANES, :] = xp_ref[tm:tm + SUBLANES, :]


def _ffn_block(h3, w_up, w_down, conv_w, conv_b, ln_g, ln_b, *, tm, n_split):
    bsz, t, d = h3.shape
    f = w_down.shape[0]
    tm = min(tm, t)
    n_hist = FFN_CONV - 1
    once = pl.Buffered(1)
    const = lambda shape: pl.BlockSpec(shape, lambda i, r: (0,) * len(shape), pipeline_mode=once)
    rows = pl.BlockSpec((1, tm, d), lambda i, r: (i, r, 0))
    return pl.pallas_call(
        functools.partial(_ffn_block_kernel, tm=tm, n_split=n_split),
        grid=(bsz, t // tm),
        in_specs=[rows, const((d, f)), pl.BlockSpec((d, f), lambda i, r: (0, 1), pipeline_mode=once),
                  const((f, d)), const((FFN_CONV, f)), const((1, f)),
                  const((1, d)), const((1, d))],
        out_specs=[rows, pl.BlockSpec((1, n_hist, f), lambda i, r: (i, 0, 0))],
        out_shape=[jax.ShapeDtypeStruct((bsz, t, d), F32), jax.ShapeDtypeStruct((bsz, n_hist, f), F32)],
        scratch_shapes=[pltpu.VMEM((SUBLANES + tm, f), F32)],
        compiler_params=_params(2),
        name="ffn_block",
    )(h3, w_up, w_up, w_down, conv_w, conv_b.reshape(1, f), ln_g.reshape(1, d), ln_b.reshape(1, d))


def _cache_rows(cache):
    b, m, h, d = cache.shape
    return cache.reshape(b, m, h, d // LANES, LANES).transpose(0, 1, 3, 2, 4).reshape(b, m * h * d // LANES, LANES)


def _cache_from_rows(rows):
    b = rows.shape[0]
    halves = MEM_HEAD_DIM // LANES
    return (rows.reshape(b, N_MEM, halves, MEM_HEADS, LANES).transpose(0, 1, 3, 2, 4)
            .reshape(b, N_MEM, MEM_HEADS, MEM_HEAD_DIM))


def kernel(x_prompt, x_sample, state_hgrn, state_gdn, state_gdn_conv, state_ffn_conv, cache_mem_k, cache_mem_v, mem_prompt, hgrn_lb_logits, w_in, w_gd_conv, gd_a_log, gd_dt_bias, hg_norm_g, gd_norm_g, w_out, ln1_g, ln1_b, w_mq, w_mkv, w_mo, ln2_g, ln2_b, w_up, w_ffn_conv, b_ffn_conv, w_down, ln3_g, ln3_b):
    bp, tp, d = x_prompt.shape
    bs, ts, _ = x_sample.shape
    tpad = SUBLANES
    row_lo = tpad - ts
    l = 0

    w_in_b = w_in[l].astype(BF16)
    w_in_tail = jnp.pad(w_in_b[:, PROJ_COLS:], ((0, 0), (0, HEAD_DIM - 2 * N_HEADS)))
    mixer_w = (hgrn_lb_logits, w_in_b, w_in_tail, w_gd_conv[l],
               gd_a_log[l], gd_dt_bias[l], hg_norm_g[l], gd_norm_g[l])
    (w_out_b, w_mq_b, w_mo_b, w_up_b, w_down_b) = (w.astype(BF16) for w in
                                                    (w_out[l], w_mq[l], w_mo[l], w_up[l], w_down[l]))

    n_hg = GD_CONV - 1
    n_hf = FFN_CONV - 1
    xs = jnp.pad(x_sample, ((0, 0), (row_lo, 0), (0, 0)))
    xs2 = xs.reshape(bs * tpad, d)
    hist_gd = jnp.pad(state_gdn_conv[l], ((0, 0), (row_lo - n_hg, ts), (0, 0)))
    hist_ffn = jnp.pad(state_ffn_conv[l], ((0, 0), (row_lo - n_hf, ts), (0, 0)))
    proj_s = _matmul(xs2, w_in_b, 1024, 512, n=PROJ_COLS).reshape(bs, tpad, PROJ_COLS)
    tail_s = _matmul(xs2, w_in_tail, 1024, HEAD_DIM).reshape(bs, tpad, HEAD_DIM)
    mix_s, s_hg, s_gd, s_bgd = _mixer(None, proj_s, tail_s, mixer_w, hist_gd, state_hgrn[l], state_gdn[l],
                                      c=tpad, nc=1, nb=8, row_lo=row_lo)
    h1_s = _matmul_res_ln([mix_s.reshape(bs * tpad, d)], [w_out_b], xs2, ln1_g[l], ln1_b[l], 512)
    q_s = _matmul(h1_s, w_mq_b, 1024, 1024).reshape(bs, tpad, d)

    mk = _kv_proj(mem_prompt, w_mkv[l][:, :d].astype(BF16))
    mv = _kv_proj(mem_prompt, w_mkv[l][:, d:].astype(BF16))
    mix_p, p_hg, p_gd, p_bgd = _mixer(x_prompt, None, None, mixer_w, None, None, None,
                                      c=128, nc=4, nb=1, row_lo=0)
    h2_p, att_s = _attn_block(mix_p, x_prompt, mk, mv, w_out_b, w_mq_b, w_mo_b, ln1_g[l], ln1_b[l],
                              ln2_g[l], ln2_b[l], q_s, _cache_rows(cache_mem_k[l]), _cache_rows(cache_mem_v[l]),
                              tm=512, n_sub=2)
    yp, p_bff = _ffn_block(h2_p, w_up_b, w_down_b, w_ffn_conv[l], b_ffn_conv[l], ln3_g[l], ln3_b[l],
                           tm=512, n_split=2)

    h2_s = _matmul_res_ln([att_s.reshape(bs * tpad, d)], [w_mo_b], h1_s, ln2_g[l], ln2_b[l], 512)
    act_s, gate_s = _ffn_up(h2_s.reshape(1, bs * tpad, d), w_up_b, w_ffn_conv[l], b_ffn_conv[l],
                            hist_ffn.reshape(1, bs * tpad, D_FF), tm=512, tn=D_FF // 2, grp=tpad, row_lo=row_lo)
    ys = _matmul_res_ln([act_s.reshape(bs * tpad, D_FF)], [w_down_b], h2_s, ln3_g[l], ln3_b[l], 512)
    ys = ys.reshape(bs, tpad, d)[:, row_lo:]
    s_bff = gate_s.reshape(bs, tpad, D_FF)[:, tpad - n_hf:]

    return (yp, ys, p_hg[None], p_gd[None], p_bgd[None], p_bff[None],
            _cache_from_rows(mk)[None], _cache_from_rows(mv)[None],
            s_hg[None], s_gd[None], s_bgd[None], s_bff[None])
```

```python
import functools

import jax
import jax.numpy as jnp
from jax import lax
from jax.experimental import pallas as pl
from jax.experimental.pallas import tpu as pltpu

F32 = jnp.float32
BF16 = jnp.bfloat16

D_MODEL = 1024
HEAD_DIM = 128
N_HEADS = 4
MIX_HALF = N_HEADS * HEAD_DIM
GD_CONV = 4
FFN_CONV = 3
D_FF = 2816
N_MEM = 256
MEM_HEADS = 4
MEM_HEAD_DIM = D_MODEL // MEM_HEADS
LN_EPS = 1e-5
RMS_EPS = 1e-6
LOG2_E = 1.4426950408889634
DEPTH = 1
ALPHA = (2.0 * DEPTH) ** 0.25

SUBLANES = 8
LANES = 128
MXU_TILE = 256
VMEM_LIMIT = 56 * 1024 * 1024


def _params(n_axes):
    return pltpu.CompilerParams(dimension_semantics=("arbitrary",) * n_axes,
                                vmem_limit_bytes=VMEM_LIMIT)


def _dot(a, b):
    return jnp.dot(a.astype(BF16), b.astype(BF16), preferred_element_type=F32)


def _dot_nt(a, b):
    return lax.dot_general(a.astype(BF16), b.astype(BF16), (((1,), (1,)), ((), ())),
                           preferred_element_type=F32)


def _dot_tn(a, b):
    return lax.dot_general(a.astype(BF16), b.astype(BF16), (((0,), (0,)), ((), ())),
                           preferred_element_type=F32)


def _split3(x):
    x1 = x.astype(BF16)
    r = x - x1.astype(F32)
    x2 = r.astype(BF16)
    x3 = (r - x2.astype(F32)).astype(BF16)
    return x1, x2, x3


def _cumsum_rows(tri_bf16, x):
    x1, x2, x3 = _split3(x)
    return (jnp.dot(tri_bf16, x1, preferred_element_type=F32)
            + jnp.dot(tri_bf16, x2, preferred_element_type=F32)
            + jnp.dot(tri_bf16, x3, preferred_element_type=F32))


def _multi_dot(a, b, keys, nt=False):
    dot = _dot_nt if nt else _dot
    return {k: dot(a[k], b[k]) for k in keys}


def _silu(x):
    return x * jax.nn.sigmoid(x)


def _mm_kernel(x_ref, w_ref, o_ref, xb_ref):
    @pl.when(pl.program_id(1) == 0)
    def _():
        xb_ref[...] = x_ref[...].astype(BF16)

    o_ref[...] = jnp.dot(xb_ref[...], w_ref[...], preferred_element_type=F32).astype(o_ref.dtype)


def _matmul(x, w, tm, tn, n=None):
    m, k = x.shape
    n = w.shape[1] if n is None else n
    tm = min(tm, m)
    tn = min(tn, n)
    return pl.pallas_call(
        _mm_kernel,
        grid=(m // tm, n // tn),
        in_specs=[pl.BlockSpec((tm, k), lambda i, j: (i, 0)),
                  pl.BlockSpec((k, tn), lambda i, j: (0, j))],
        out_specs=pl.BlockSpec((tm, tn), lambda i, j: (i, j)),
        out_shape=jax.ShapeDtypeStruct((m, n), F32),
        scratch_shapes=[pltpu.VMEM((tm, k), BF16)],
        compiler_params=_params(2),
        name="proj_matmul",
    )(x, w)


ROWS_PER_MEM = MEM_HEADS * MEM_HEAD_DIM // LANES


def _head_rows(h, j):
    return pl.ds(h + MEM_HEADS * j, N_MEM, stride=ROWS_PER_MEM)


def _head_block(ref, b, h):
    return jnp.concatenate([ref[b, _head_rows(h, j), :] for j in range(MEM_HEAD_DIM // LANES)], axis=1)


def _kv_proj_kernel(x_ref, wk_ref, wv_ref, k_ref, v_ref):
    xb = x_ref[0].astype(BF16)
    for w_ref, o_ref in ((wk_ref, k_ref), (wv_ref, v_ref)):
        res = jnp.dot(xb, w_ref[...], preferred_element_type=F32)
        for h in range(MEM_HEADS):
            for j in range(MEM_HEAD_DIM // LANES):
                c0 = h * MEM_HEAD_DIM + j * LANES
                o_ref[0, _head_rows(h, j), :] = res[:, c0:c0 + LANES]


def _kv_proj(mem3, w_kv):
    bsz, n_mem, d = mem3.shape
    n = w_kv.shape[1] // 2
    out_spec = pl.BlockSpec((1, n_mem * ROWS_PER_MEM, LANES), lambda i: (i, 0, 0))
    out_shape = jax.ShapeDtypeStruct((bsz, n_mem * ROWS_PER_MEM, LANES), F32)
    return pl.pallas_call(
        _kv_proj_kernel,
        grid=(bsz,),
        in_specs=[pl.BlockSpec((1, n_mem, d), lambda i: (i, 0, 0)),
                  pl.BlockSpec((d, n), lambda i: (0, 0)), pl.BlockSpec((d, n), lambda i: (0, 1))],
        out_specs=[out_spec, out_spec],
        out_shape=[out_shape, out_shape],
        compiler_params=_params(1),
        name="kv_proj",
    )(mem3, w_kv, w_kv)


def _mm_ln_kernel(*refs, n_in):
    a_refs = refs[:n_in]
    w_refs = refs[n_in:2 * n_in]
    res_ref, g_ref, b_ref, o_ref = refs[2 * n_in:]
    acc = ALPHA * res_ref[...]
    for a_ref, w_ref in zip(a_refs, w_refs):
        acc = acc + jnp.dot(a_ref[...].astype(BF16), w_ref[...], preferred_element_type=F32)
    mu = jnp.mean(acc, axis=-1, keepdims=True)
    xc = acc - mu
    var = jnp.mean(xc * xc, axis=-1, keepdims=True)
    o_ref[...] = xc * lax.rsqrt(var + LN_EPS) * g_ref[...] + b_ref[...]


def _matmul_res_ln(a_list, w_list, res, g, b, tm):
    m, d = res.shape
    tm = min(tm, m)
    n_in = len(a_list)
    in_specs = ([pl.BlockSpec((tm, a.shape[1]), lambda i: (i, 0)) for a in a_list]
                + [pl.BlockSpec(w.shape, lambda i: (0, 0)) for w in w_list]
                + [pl.BlockSpec((tm, d), lambda i: (i, 0)),
                   pl.BlockSpec((1, d), lambda i: (0, 0)),
                   pl.BlockSpec((1, d), lambda i: (0, 0))])
    return pl.pallas_call(
        functools.partial(_mm_ln_kernel, n_in=n_in),
        grid=(m // tm,),
        in_specs=in_specs,
        out_specs=pl.BlockSpec((tm, d), lambda i: (i, 0)),
        out_shape=jax.ShapeDtypeStruct((m, d), F32),
        compiler_params=_params(1),
        name="proj_res_ln",
    )(*a_list, *w_list, res, g.reshape(1, d), b.reshape(1, d))


PROJ_COLS = 8 * MIX_HALF
HG_COL = 0
GD_COL = 4 * MIX_HALF
HGRN_GROUP = 1
GDN_GROUP = 4


def _ref_rows(gc_ref, b, r0, col, m, c):
    blk = 2 * m
    if blk >= SUBLANES:
        parts = [jnp.broadcast_to(gc_ref[b, pl.ds(r0 + j * blk + m - 1, 1), col], (blk, HEAD_DIM))
                 for j in range(c // blk)]
    else:
        sub = lax.broadcasted_iota(jnp.int32, (SUBLANES, HEAD_DIM), 0)
        parts = []
        for i in range(c // SUBLANES):
            tile = None
            for j in range(SUBLANES // blk):
                row = jnp.broadcast_to(gc_ref[b, pl.ds(r0 + i * SUBLANES + j * blk + m - 1, 1), col],
                                       (SUBLANES, HEAD_DIM))
                tile = row if tile is None else jnp.where(sub >= j * blk, row, tile)
            parts.append(tile)
    return parts[0] if len(parts) == 1 else jnp.concatenate(parts, axis=0)


def _hgrn_rows(p_ref, mix_ref, lb, ng, st_ref, gc_ref, *, chunk_ids, c, nb, row_lo):
    rows = lax.broadcasted_iota(jnp.int32, (c, 1), 0)
    ti = lax.broadcasted_iota(jnp.int32, (c, c), 0)
    si = lax.broadcasted_iota(jnp.int32, (c, c), 1)
    tri = (si <= ti).astype(F32).astype(BF16)
    eye = ti == si
    xr = ti ^ si
    cols = [slice(h * HEAD_DIM, (h + 1) * HEAD_DIM) for h in range(N_HEADS)]
    chains = [(b, j, h) for b in range(nb) for j in chunk_ids for h in range(N_HEADS)]

    def proj(b, j, part):
        return p_ref[b, pl.ds(j * c, c), HG_COL + part * MIX_HALF:HG_COL + (part + 1) * MIX_HALF]

    q, k, gc = {}, {}, {}
    for b in range(nb):
        for j in chunk_ids:
            f = lb + (1.0 - lb) * jax.nn.sigmoid(proj(b, j, 1))
            g = jnp.log(f)
            kb = 1.0 - f
            if row_lo:
                g = jnp.where(rows >= row_lo, g, 0.0)
                kb = jnp.where(rows >= row_lo, kb, 0.0)
            qb = _silu(proj(b, j, 0))
            gcb = _cumsum_rows(tri, g)
            gc_ref[b, pl.ds(j * c, c), :] = gcb
            for h in range(N_HEADS):
                q[b, j, h], k[b, j, h], gc[b, j, h] = qb[:, cols[h]], kb[:, cols[h]], gcb[:, cols[h]]

    level_of = jnp.where(eye, 0, -1)
    m = 1
    while m < c:
        level_of = jnp.where((xr >= m) & (xr < 2 * m) & (si < ti), m, level_of)
        m *= 2
    rows_w = lax.broadcasted_iota(jnp.int32, (c, HEAD_DIM), 0)
    q_b = {ch: q[ch].astype(BF16) for ch in chains}
    k_b = {ch: k[ch].astype(BF16) for ch in chains}
    sc = _multi_dot(q_b, k_b, chains, nt=True)
    on_diag = level_of == 0
    sc = {ch: jnp.where(on_diag, sc[ch], 0.0) for ch in chains}
    m = c // 2
    while m >= 1:
        sign = jnp.where((rows_w & m) != 0, LOG2_E, -LOG2_E)
        qw, kw = {}, {}
        for ch in chains:
            b, j, h = ch
            w = jnp.exp2((gc[ch] - _ref_rows(gc_ref, b, j * c, cols[h], m, c)) * sign).astype(BF16)
            qw[ch], kw[ch] = q_b[ch] * w, k_b[ch] * w
        prod = _multi_dot(qw, kw, chains, nt=True)
        at_level = level_of == m
        sc = {ch: jnp.where(at_level, prod[ch], sc[ch]) for ch in chains}
        m //= 2

    v = {(b, j, h): p_ref[b, pl.ds(j * c, c),
                          HG_COL + 2 * MIX_HALF + h * HEAD_DIM:HG_COL + 2 * MIX_HALF + (h + 1) * HEAD_DIM]
         for b, j, h in chains}
    o_intra = _multi_dot(sc, v, chains)

    for j in chunk_ids:
        group = [(b, j, h) for b in range(nb) for h in range(N_HEADS)]
        st = {ch: st_ref[ch[0], ch[2]] for ch in group}
        o_inter = _multi_dot({ch: q[ch] * jnp.exp(gc[ch]) for ch in group}, st, group, nt=True)
        for ch in group:
            b, _, h = ch
            g_last = gc_ref[b, pl.ds(j * c + c - 1, 1), cols[h]]
            st_ref[b, h] = st[ch] * jnp.exp(g_last) + _dot_tn(v[ch], k[ch] * jnp.exp(g_last - gc[ch]))
        for ch in group:
            b, _, h = ch
            ob = o_intra[ch] + o_inter[ch]
            on = ob * lax.rsqrt(jnp.mean(ob * ob, axis=-1, keepdims=True) + RMS_EPS) * ng
            gate = p_ref[b, pl.ds(j * c, c),
                         HG_COL + 3 * MIX_HALF + h * HEAD_DIM:HG_COL + 3 * MIX_HALF + (h + 1) * HEAD_DIM]
            mix_ref[b, pl.ds(j * c, c), cols[h]] = on * _silu(gate)


def _gdn_rows(p_ref, t_ref, hist_ref, mix_ref, cb_ref, cw_ref, neg_a, dt_bias, ng, s_ref, xp_ref,
              *, c, nc, nb, row_lo):
    n_hist = GD_CONV - 1
    rb = c * nc
    rows = lax.broadcasted_iota(jnp.int32, (rb, 1), 0) % c
    ti = lax.broadcasted_iota(jnp.int32, (c, c), 0)
    si = lax.broadcasted_iota(jnp.int32, (c, c), 1)
    tri = (si <= ti).astype(F32).astype(BF16)
    eye_f = (ti == si).astype(F32)
    incl = si <= ti

    q_all, k_all, v_all, beta_all, la_all = {}, {}, {}, {}, {}
    for b in range(nb):
        qkv = []
        for j in range(3):
            col = slice(j * MIX_HALF, (j + 1) * MIX_HALF)
            x = p_ref[b, :, GD_COL + j * MIX_HALF:GD_COL + (j + 1) * MIX_HALF]
            if hist_ref is not None:
                x = jnp.where((rows >= row_lo - n_hist) & (rows < row_lo), hist_ref[b, :, col], x)
            xp_ref[b, SUBLANES:SUBLANES + rb, col] = x
            y = x * cw_ref[n_hist:n_hist + 1, col]
            for tap in range(n_hist):
                off = SUBLANES - n_hist + tap
                y = y + xp_ref[b, off:off + rb, col] * cw_ref[tap:tap + 1, col]
            qkv.append(_silu(y))
        cb_ref[b] = xp_ref[b, SUBLANES + rb - n_hist:SUBLANES + rb, :]
        xp_ref[b, 0:SUBLANES, :] = xp_ref[b, rb:rb + SUBLANES, :]
        q_all[b], k_all[b], v_all[b] = qkv

        tail = t_ref[b]
        beta_all[b] = jax.nn.sigmoid(tail)
        la_all[b] = neg_a * jax.nn.softplus(tail + dt_bias)
        if row_lo:
            beta_all[b] = jnp.where(rows >= row_lo, beta_all[b], 0.0)
            la_all[b] = jnp.where(rows >= row_lo, la_all[b], 0.0)

    for j0 in range(0, nc, GDN_GROUP):
        _gdn_chunks(list(range(j0, min(nc, j0 + GDN_GROUP))), q_all, k_all, v_all, beta_all, la_all,
                    p_ref, mix_ref, ng, s_ref, c=c, nb=nb)


def _gdn_chunks(chunk_ids, q_all, k_all, v_all, beta_all, la_all, p_ref, mix_ref, ng, s_ref, *, c, nb):
    ti = lax.broadcasted_iota(jnp.int32, (c, c), 0)
    si = lax.broadcasted_iota(jnp.int32, (c, c), 1)
    tri = (si <= ti).astype(F32).astype(BF16)
    eye_f = (ti == si).astype(F32)
    incl = si <= ti
    chains = [(b, j, h) for b in range(nb) for j in chunk_ids for h in range(N_HEADS)]

    gc = {(b, j): _cumsum_rows(tri, la_all[b][j * c:(j + 1) * c]) for b in range(nb) for j in chunk_ids}

    pre = {}
    for ch in chains:
        b, j, h = ch
        rs = slice(j * c, (j + 1) * c)
        col = slice(h * HEAD_DIM, (h + 1) * HEAD_DIM)
        qh, kh, vh = q_all[b][rs, col], k_all[b][rs, col], v_all[b][rs, col]
        qh = qh * lax.rsqrt(jnp.sum(qh * qh, axis=-1, keepdims=True) + RMS_EPS) * (HEAD_DIM ** -0.5)
        kh = kh * lax.rsqrt(jnp.sum(kh * kh, axis=-1, keepdims=True) + RMS_EPS)
        beta = beta_all[b][rs, h:h + 1]
        gcol = gc[b, j][:, N_HEADS + h:N_HEADS + h + 1]
        grow = jnp.sum(eye_f * gcol, axis=0, keepdims=True)
        decay = jnp.where(incl, jnp.exp(jnp.where(incl, gcol - grow, 0.0)), 0.0)
        gamma = jnp.exp(gcol)
        g_last = gcol[c - 1:c, :]
        pre[ch] = dict(
            q=qh, k=kh, beta=beta, decay=decay,
            rhs=jnp.concatenate([(beta * gamma) * kh, beta * vh], axis=1).astype(BF16),
            gq=(gamma * qh).astype(BF16),
            kd=(kh * jnp.exp(g_last - gcol)).astype(BF16),
            eg=jnp.exp(g_last))
    k_b = {ch: pre[ch]["k"].astype(BF16) for ch in chains}
    kk = _multi_dot(k_b, k_b, chains, nt=True)
    qk = _multi_dot({ch: pre[ch]["q"] for ch in chains}, k_b, chains, nt=True)
    a_mat = {ch: (pre[ch]["beta"] * kk[ch] * pre[ch]["decay"]).astype(BF16) for ch in chains}
    aqk = {ch: (qk[ch] * pre[ch]["decay"]).astype(BF16) for ch in chains}

    xr = ti ^ si
    first = (xr < 2) & (si < ti)
    zero_b = jnp.zeros((c, c), BF16)
    t_inv = {ch: (eye_f - jnp.where(first, a_mat[ch], zero_b).astype(F32)).astype(BF16) for ch in chains}
    n = 2
    while n < c:
        lower = (xr < 2 * n) & ((ti & n) != 0) & ((si & n) == 0)
        x = _multi_dot({ch: jnp.where(lower, a_mat[ch], zero_b) for ch in chains}, t_inv, chains)
        tx = _multi_dot(t_inv, x, chains)
        t_inv = {ch: (t_inv[ch].astype(F32) - tx[ch]).astype(BF16) for ch in chains}
        n *= 2
    wu = {ch: _dot(t_inv[ch], pre[ch]["rhs"]) for ch in chains}

    for j in chunk_ids:
        rs = slice(j * c, (j + 1) * c)
        group = [(b, j, h) for b in range(nb) for h in range(N_HEADS)]
        s_old = {ch: s_ref[ch[0], ch[2]] for ch in group}
        ws = _multi_dot({ch: jnp.concatenate([wu[ch][:, :HEAD_DIM].astype(BF16), pre[ch]["gq"]], axis=0)
                         for ch in group}, s_old, group)
        u = {ch: (wu[ch][:, HEAD_DIM:] - ws[ch][:c]).astype(BF16) for ch in group}
        au = _multi_dot(aqk, u, group)
        for ch in group:
            b, _, h = ch
            s_ref[b, h] = pre[ch]["eg"] * s_old[ch] + _dot_tn(pre[ch]["kd"], u[ch])
            o = ws[ch][c:] + au[ch]
            on = o * lax.rsqrt(jnp.mean(o * o, axis=-1, keepdims=True) + RMS_EPS) * ng
            gate = p_ref[b, rs, GD_COL + 3 * MIX_HALF + h * HEAD_DIM:GD_COL + 3 * MIX_HALF + (h + 1) * HEAD_DIM]
            mix_ref[b, rs, MIX_HALF + h * HEAD_DIM:MIX_HALF + (h + 1) * HEAD_DIM] = on * _silu(gate)


def _mixer_kernel(*refs, c, nc, nb, fuse_proj, has_state, row_lo, n_steps):
    refs = list(refs)
    if fuse_proj:
        x_ref, w_ref, wt_ref = refs[:3]
        del refs[:3]
    else:
        p_ref, t_ref = refs[:2]
        del refs[:2]
    lbl_ref, hng_ref, cw_ref, par_ref, gng_ref = refs[:5]
    del refs[:5]
    if has_state:
        hist_ref, shg0_ref, sgd0_ref = refs[:3]
        del refs[:3]
    else:
        hist_ref = shg0_ref = sgd0_ref = None
    mix_ref, shg_ref, sgd_ref, cb_ref, st_ref, s_ref, gc_ref, xp_ref = refs[:8]
    if fuse_proj:
        p_ref, t_ref = refs[8:]
    step = pl.program_id(1)
    pairs = [(b, h) for b in range(nb) for h in range(N_HEADS)]

    @pl.when(step == 0)
    def _():
        for b, h in pairs:
            if has_state:
                st_ref[b, h] = shg0_ref[b, h].T
                s_ref[b, h] = sgd0_ref[b, h]
            else:
                st_ref[b, h] = jnp.zeros((HEAD_DIM, HEAD_DIM), F32)
                s_ref[b, h] = jnp.zeros((HEAD_DIM, HEAD_DIM), F32)
        for b in range(nb):
            xp_ref[b, 0:SUBLANES, :] = jnp.zeros((SUBLANES, 3 * MIX_HALF), F32)

    if fuse_proj:
        for b in range(nb):
            xb = x_ref[b].astype(BF16)
            p_ref[b] = jnp.dot(xb, w_ref[...], preferred_element_type=F32)
            t_ref[b] = jnp.dot(xb, wt_ref[...], preferred_element_type=F32)

    lbl = lbl_ref[...]
    e = jnp.exp(lbl - jnp.max(lbl, axis=0, keepdims=True))
    lb = e[0:1] / jnp.sum(e, axis=0, keepdims=True)
    neg_a = -jnp.exp(par_ref[0:1, :])
    dt_bias = par_ref[1:2, :]

    for j0 in range(0, nc, HGRN_GROUP):
        _hgrn_rows(p_ref, mix_ref, lb, hng_ref[...], st_ref, gc_ref,
                   chunk_ids=list(range(j0, min(nc, j0 + HGRN_GROUP))), c=c, nb=nb, row_lo=row_lo)
    _gdn_rows(p_ref, t_ref, hist_ref, mix_ref, cb_ref, cw_ref, neg_a, dt_bias, gng_ref[...], s_ref, xp_ref,
              c=c, nc=nc, nb=nb, row_lo=row_lo)

    @pl.when(step == n_steps - 1)
    def _():
        for b, h in pairs:
            shg_ref[b, h] = st_ref[b, h].T
            sgd_ref[b, h] = s_ref[b, h]


def _mixer(x3, proj, tail, weights, hist, s_hg, s_gd, *, c, nc, nb, row_lo):
    lb_logits, w_main, w_tail, conv_w, a_log, dt_bias, hg_norm_g, gd_norm_g = weights
    fuse_proj = proj is None
    bsz, t = (x3 if fuse_proj else proj).shape[:2]
    rb = c * nc
    n_steps = t // rb
    has_state = s_hg is not None
    n_hist = GD_CONV - 1
    const = lambda shape: pl.BlockSpec(shape, lambda i, si: (0,) * len(shape))
    rows3 = lambda width: pl.BlockSpec((nb, rb, width), lambda i, si: (i, si, 0))
    st_spec = pl.BlockSpec((nb, N_HEADS, HEAD_DIM, HEAD_DIM), lambda i, si: (i, 0, 0, 0))
    par = jnp.zeros((SUBLANES, HEAD_DIM), F32)
    par = par.at[0, N_HEADS:2 * N_HEADS].set(a_log).at[1, N_HEADS:2 * N_HEADS].set(dt_bias)
    if fuse_proj:
        in_specs = [rows3(D_MODEL), const((D_MODEL, PROJ_COLS)), const(w_tail.shape)]
        args = [x3, w_main, w_tail]
    else:
        in_specs = [rows3(PROJ_COLS), rows3(HEAD_DIM)]
        args = [proj, tail]
    in_specs += [const(lb_logits.shape), const((1, HEAD_DIM)), const(conv_w.shape), const(par.shape),
                 const((1, HEAD_DIM))]
    args += [lb_logits, hg_norm_g.reshape(1, HEAD_DIM), conv_w, par, gd_norm_g.reshape(1, HEAD_DIM)]
    if has_state:
        in_specs += [rows3(3 * MIX_HALF), st_spec, st_spec]
        args += [hist, s_hg, s_gd]
    scratch = [pltpu.VMEM((nb, N_HEADS, HEAD_DIM, HEAD_DIM), F32),
               pltpu.VMEM((nb, N_HEADS, HEAD_DIM, HEAD_DIM), F32),
               pltpu.VMEM((nb, rb, MIX_HALF), F32),
               pltpu.VMEM((nb, SUBLANES + rb, 3 * MIX_HALF), F32)]
    if fuse_proj:
        scratch += [pltpu.VMEM((nb, rb, PROJ_COLS), F32), pltpu.VMEM((nb, rb, HEAD_DIM), F32)]
    return pl.pallas_call(
        functools.partial(_mixer_kernel, c=c, nc=nc, nb=nb, fuse_proj=fuse_proj, has_state=has_state,
                          row_lo=row_lo, n_steps=n_steps),
        grid=(bsz // nb, n_steps),
        in_specs=in_specs,
        out_specs=[rows3(D_MODEL), st_spec, st_spec,
                   pl.BlockSpec((nb, n_hist, 3 * MIX_HALF), lambda i, si: (i, 0, 0))],
        out_shape=[jax.ShapeDtypeStruct((bsz, t, D_MODEL), F32),
                   jax.ShapeDtypeStruct((bsz, N_HEADS, HEAD_DIM, HEAD_DIM), F32),
                   jax.ShapeDtypeStruct((bsz, N_HEADS, HEAD_DIM, HEAD_DIM), F32),
                   jax.ShapeDtypeStruct((bsz, n_hist, 3 * MIX_HALF), F32)],
        scratch_shapes=scratch,
        compiler_params=_params(2),
        name="mixer",
    )(*args)


def _attn_rows(q_ref, k_ref, v_ref, o_ref):
    nb = q_ref.shape[0]
    scale = MEM_HEAD_DIM ** -0.5
    pairs = [(b, h) for b in range(nb) for h in range(MEM_HEADS)]
    s = {bh: _dot_nt(q_ref[bh[0], :, bh[1] * MEM_HEAD_DIM:(bh[1] + 1) * MEM_HEAD_DIM],
                     _head_block(k_ref, *bh)) * scale for bh in pairs}
    p = {}
    for bh in pairs:
        e = jnp.exp(s[bh] - jnp.max(s[bh], axis=-1, keepdims=True))
        p[bh] = e / jnp.sum(e, axis=-1, keepdims=True)
    for b, h in pairs:
        o_ref[b, :, h * MEM_HEAD_DIM:(h + 1) * MEM_HEAD_DIM] = _dot(p[b, h], _head_block(v_ref, b, h))


def _attn_block_kernel(mix_ref, x_ref, k_ref, v_ref, wo_ref, wq_ref, wm_ref, g1_ref, b1_ref, g2_ref, b2_ref,
                       sq_ref, sk_ref, sv_ref, o_ref, so_ref, *, n_sub):
    _attn_rows(sq_ref, sk_ref, sv_ref, so_ref)
    def layer_norm(acc, g_ref, b_ref):
        mu = jnp.mean(acc, axis=-1, keepdims=True)
        xc = acc - mu
        var = jnp.mean(xc * xc, axis=-1, keepdims=True)
        return xc * lax.rsqrt(var + LN_EPS) * g_ref[...] + b_ref[...]

    ts = x_ref.shape[1] // n_sub
    subs = [pl.ds(i * ts, ts) for i in range(n_sub)]
    scale = MEM_HEAD_DIM ** -0.5
    cols = [slice(h * MEM_HEAD_DIM, (h + 1) * MEM_HEAD_DIM) for h in range(MEM_HEADS)]
    heads = range(MEM_HEADS)
    kh = [_head_block(k_ref, 0, h).astype(BF16) for h in heads]
    vh = [_head_block(v_ref, 0, h).astype(BF16) for h in heads]
    h1 = [layer_norm(ALPHA * x_ref[0, r, :]
                     + jnp.dot(mix_ref[0, r, :].astype(BF16), wo_ref[...], preferred_element_type=F32),
                     g1_ref, b1_ref) for r in subs]
    q = [jnp.dot(hi.astype(BF16), wq_ref[...], preferred_element_type=F32).astype(BF16) for hi in h1]
    s = [[_dot_nt(qi[:, cols[h]], kh[h]) * scale for h in heads] for qi in q]
    p = []
    for si in s:
        pi = []
        for sh in si:
            e = jnp.exp(sh - jnp.max(sh, axis=-1, keepdims=True))
            pi.append(e / jnp.sum(e, axis=-1, keepdims=True))
        p.append(pi)
    att = [jnp.concatenate([_dot(pi[h], vh[h]).astype(BF16) for h in heads], axis=1) for pi in p]
    for r, hi, ai in zip(subs, h1, att):
        o_ref[0, r, :] = layer_norm(ALPHA * hi + jnp.dot(ai, wm_ref[...], preferred_element_type=F32),
                                    g2_ref, b2_ref)


def _attn_block(mix3, x3, mk, mv, w_out, w_mq, w_mo, ln1_g, ln1_b, ln2_g, ln2_b, side_q, side_k, side_v,
                *, tm, n_sub):
    bsz, t, d = x3.shape
    tm = min(tm, t)
    n_steps = bsz * (t // tm)
    bs, ts, _ = side_q.shape
    assert bs % n_steps == 0
    nb = bs // n_steps
    flat = lambda i, r: (i * (t // tm) + r, 0, 0)
    side_rows = pl.BlockSpec((nb, ts, d), flat)
    side_kv = pl.BlockSpec((nb,) + side_k.shape[1:], flat)
    rows = pl.BlockSpec((1, tm, d), lambda i, r: (i, r, 0))
    kv = pl.BlockSpec((1,) + mk.shape[1:], lambda i, r: (i, 0, 0))
    wsp = pl.BlockSpec((d, d), lambda i, r: (0, 0))
    vec = pl.BlockSpec((1, d), lambda i, r: (0, 0))
    return pl.pallas_call(
        functools.partial(_attn_block_kernel, n_sub=n_sub),
        grid=(bsz, t // tm),
        in_specs=[rows, rows, kv, kv, wsp, wsp, wsp, vec, vec, vec, vec, side_rows, side_kv, side_kv],
        out_specs=[rows, side_rows],
        out_shape=[jax.ShapeDtypeStruct((bsz, t, d), F32), jax.ShapeDtypeStruct((bs, ts, d), F32)],
        compiler_params=_params(2),
        name="attn_block",
    )(mix3, x3, mk, mv, w_out, w_mq, w_mo, ln1_g.reshape(1, d), ln1_b.reshape(1, d),
      ln2_g.reshape(1, d), ln2_b.reshape(1, d), side_q, side_k, side_v)


def _ffn_block_kernel(h_ref, wg_ref, wv_ref, wd_ref, cw_ref, cb_ref, g_ref, b_ref, sh_ref, shist_ref,
                      y_ref, buf_ref, sy_ref, sgate_ref, xp_ref, sxp_ref, *, tm, n_split, grp, row_lo):
    n_hist = FFN_CONV - 1
    f = xp_ref.shape[1]
    ns = sh_ref.shape[0]
    tiles = f // MXU_TILE
    bounds = [MXU_TILE * -(-tiles * j // n_split) for j in range(n_split)] + [f]

    @pl.when(pl.program_id(1) == 0)
    def _():
        xp_ref[0:SUBLANES, :] = jnp.zeros((SUBLANES, f), F32)
        sxp_ref[0:SUBLANES, :] = jnp.zeros((SUBLANES, f), F32)

    h = jnp.concatenate([h_ref[0], sh_ref[...]], axis=0)
    hb = h.astype(BF16)
    cols = [slice(bounds[j], bounds[j + 1]) for j in range(n_split)]
    pos = lax.broadcasted_iota(jnp.int32, (ns, 1), 0) % grp
    is_hist = (pos >= row_lo - n_hist) & (pos < row_lo)

    def conv(ref, rows, gate, c):
        ref[SUBLANES:SUBLANES + rows, c] = gate
        y = gate * cw_ref[n_hist:n_hist + 1, c] + cb_ref[:, c]
        for tap in range(n_hist):
            off = SUBLANES - n_hist + tap
            y = y + ref[off:off + rows, c] * cw_ref[tap:tap + 1, c]
        return y

    gate = [jnp.dot(hb, wg_ref[:, c], preferred_element_type=F32) for c in cols]
    val = [jnp.dot(hb, wv_ref[:, c], preferred_element_type=F32) for c in cols]
    act = []
    for c, gj, vj in zip(cols, gate, val):
        side_gate = jnp.where(is_hist, shist_ref[:, c], gj[tm:])
        sgate_ref[:, c] = side_gate
        y = jnp.concatenate([conv(xp_ref, tm, gj[:tm], c), conv(sxp_ref, ns, side_gate, c)], axis=0)
        gelu = 0.5 * y * (1.0 + lax.erf(y * (2.0 ** -0.5)))
        act.append((gelu * vj).astype(BF16))
    acc = ALPHA * h
    for c, aj in zip(cols, act):
        acc = acc + jnp.dot(aj, wd_ref[c, :], preferred_element_type=F32)
    mu = jnp.mean(acc, axis=-1, keepdims=True)
    xc = acc - mu
    var = jnp.mean(xc * xc, axis=-1, keepdims=True)
    out = xc * lax.rsqrt(var + LN_EPS) * g_ref[...] + b_ref[...]
    y_ref[0] = out[:tm]
    sy_ref[...] = out[tm:]
    buf_ref[0] = xp_ref[SUBLANES + tm - n_hist:SUBLANES + tm, :]
    xp_ref[0:SUBLANES, :] = xp_ref[tm:tm + SUBLANES, :]


def _ffn_block(h3, w_up, w_down, conv_w, conv_b, ln_g, ln_b, side_h, side_hist, *, tm, n_split, grp, row_lo):
    bsz, t, d = h3.shape
    f = w_down.shape[0]
    tm = min(tm, t)
    n_hist = FFN_CONV - 1
    n_steps = bsz * (t // tm)
    rs = side_h.shape[0]
    assert rs % (n_steps * grp) == 0
    ns = rs // n_steps
    once = pl.Buffered(1)
    const = lambda shape: pl.BlockSpec(shape, lambda i, r: (0,) * len(shape), pipeline_mode=once)
    rows = pl.BlockSpec((1, tm, d), lambda i, r: (i, r, 0))
    side = lambda width: pl.BlockSpec((ns, width), lambda i, r: (i * (t // tm) + r, 0))
    return pl.pallas_call(
        functools.partial(_ffn_block_kernel, tm=tm, n_split=n_split, grp=grp, row_lo=row_lo),
        grid=(bsz, t // tm),
        in_specs=[rows, const((d, f)), pl.BlockSpec((d, f), lambda i, r: (0, 1), pipeline_mode=once),
                  const((f, d)), const((FFN_CONV, f)), const((1, f)),
                  const((1, d)), const((1, d)), side(d), side(f)],
        out_specs=[rows, pl.BlockSpec((1, n_hist, f), lambda i, r: (i, 0, 0)), side(d), side(f)],
        out_shape=[jax.ShapeDtypeStruct((bsz, t, d), F32), jax.ShapeDtypeStruct((bsz, n_hist, f), F32),
                   jax.ShapeDtypeStruct((rs, d), F32), jax.ShapeDtypeStruct((rs, f), F32)],
        scratch_shapes=[pltpu.VMEM((SUBLANES + tm, f), F32), pltpu.VMEM((SUBLANES + ns, f), F32)],
        compiler_params=_params(2),
        name="ffn_block",
    )(h3, w_up, w_up, w_down, conv_w, conv_b.reshape(1, f), ln_g.reshape(1, d), ln_b.reshape(1, d),
      side_h, side_hist)


def _cache_rows(cache):
    b, m, h, d = cache.shape
    return cache.reshape(b, m, h, d // LANES, LANES).transpose(0, 1, 3, 2, 4).reshape(b, m * h * d // LANES, LANES)


def _cache_from_rows(rows):
    b = rows.shape[0]
    halves = MEM_HEAD_DIM // LANES
    return (rows.reshape(b, N_MEM, halves, MEM_HEADS, LANES).transpose(0, 1, 3, 2, 4)
            .reshape(b, N_MEM, MEM_HEADS, MEM_HEAD_DIM))


def kernel(x_prompt, x_sample, state_hgrn, state_gdn, state_gdn_conv, state_ffn_conv, cache_mem_k, cache_mem_v, mem_prompt, hgrn_lb_logits, w_in, w_gd_conv, gd_a_log, gd_dt_bias, hg_norm_g, gd_norm_g, w_out, ln1_g, ln1_b, w_mq, w_mkv, w_mo, ln2_g, ln2_b, w_up, w_ffn_conv, b_ffn_conv, w_down, ln3_g, ln3_b):
    bp, tp, d = x_prompt.shape
    bs, ts, _ = x_sample.shape
    tpad = SUBLANES
    row_lo = tpad - ts
    l = 0

    w_in_b = w_in[l].astype(BF16)
    w_in_tail = jnp.pad(w_in_b[:, PROJ_COLS:], ((0, 0), (0, HEAD_DIM - 2 * N_HEADS)))
    mixer_w = (hgrn_lb_logits, w_in_b, w_in_tail, w_gd_conv[l],
               gd_a_log[l], gd_dt_bias[l], hg_norm_g[l], gd_norm_g[l])
    (w_out_b, w_mq_b, w_mo_b, w_up_b, w_down_b) = (w.astype(BF16) for w in
                                                    (w_out[l], w_mq[l], w_mo[l], w_up[l], w_down[l]))

    n_hg = GD_CONV - 1
    n_hf = FFN_CONV - 1
    xs = jnp.pad(x_sample, ((0, 0), (row_lo, 0), (0, 0)))
    xs2 = xs.reshape(bs * tpad, d)
    hist_gd = jnp.pad(state_gdn_conv[l], ((0, 0), (row_lo - n_hg, ts), (0, 0)))
    hist_ffn = jnp.pad(state_ffn_conv[l], ((0, 0), (row_lo - n_hf, ts), (0, 0)))
    proj_s = _matmul(xs2, w_in_b, 1024, 512, n=PROJ_COLS).reshape(bs, tpad, PROJ_COLS)
    tail_s = _matmul(xs2, w_in_tail, 1024, HEAD_DIM).reshape(bs, tpad, HEAD_DIM)
    mix_s, s_hg, s_gd, s_bgd = _mixer(None, proj_s, tail_s, mixer_w, hist_gd, state_hgrn[l], state_gdn[l],
                                      c=tpad, nc=1, nb=8, row_lo=row_lo)
    h1_s = _matmul_res_ln([mix_s.reshape(bs * tpad, d)], [w_out_b], xs2, ln1_g[l], ln1_b[l], 512)
    q_s = _matmul(h1_s, w_mq_b, 1024, 1024).reshape(bs, tpad, d)

    mk, mv = _kv_proj(mem_prompt, w_mkv[l].astype(BF16))
    mix_p, p_hg, p_gd, p_bgd = _mixer(x_prompt, None, None, mixer_w, None, None, None,
                                      c=128, nc=4, nb=1, row_lo=0)
    h2_p, att_s = _attn_block(mix_p, x_prompt, mk, mv, w_out_b, w_mq_b, w_mo_b, ln1_g[l], ln1_b[l],
                              ln2_g[l], ln2_b[l], q_s, _cache_rows(cache_mem_k[l]), _cache_rows(cache_mem_v[l]),
                              tm=512, n_sub=2)
    h2_s = _matmul_res_ln([att_s.reshape(bs * tpad, d)], [w_mo_b], h1_s, ln2_g[l], ln2_b[l], 512)
    yp, p_bff, ys, gate_s = _ffn_block(h2_p, w_up_b, w_down_b, w_ffn_conv[l], b_ffn_conv[l], ln3_g[l], ln3_b[l],
                                       h2_s, hist_ffn.reshape(bs * tpad, D_FF),
                                       tm=512, n_split=2, grp=tpad, row_lo=row_lo)
    ys = ys.reshape(bs, tpad, d)[:, row_lo:]
    s_bff = gate_s.reshape(bs, tpad, D_FF)[:, tpad - n_hf:]

    return (yp, ys, p_hg[None], p_gd[None], p_bgd[None], p_bff[None],
            _cache_from_rows(mk)[None], _cache_from_rows(mv)[None],
            s_hg[None], s_gd[None], s_bgd[None], s_bff[None])
```

```python
import functools

import jax
import jax.numpy as jnp
from jax import lax
from jax.experimental import pallas as pl
from jax.experimental.pallas import tpu as pltpu

F32 = jnp.float32
BF16 = jnp.bfloat16

D_MODEL = 1024
HEAD_DIM = 128
N_HEADS = 4
MIX_HALF = N_HEADS * HEAD_DIM
GD_CONV = 4
FFN_CONV = 3
D_FF = 2816
N_MEM = 256
MEM_HEADS = 4
MEM_HEAD_DIM = D_MODEL // MEM_HEADS
LN_EPS = 1e-5
RMS_EPS = 1e-6
LOG2_E = 1.4426950408889634
DEPTH = 1
ALPHA = (2.0 * DEPTH) ** 0.25

SUBLANES = 8
LANES = 128
MXU_TILE = 256
VMEM_LIMIT = 56 * 1024 * 1024


def _params(n_axes):
    return pltpu.CompilerParams(dimension_semantics=("arbitrary",) * n_axes,
                                vmem_limit_bytes=VMEM_LIMIT)


def _dot(a, b):
    return jnp.dot(a.astype(BF16), b.astype(BF16), preferred_element_type=F32)


def _dot_nt(a, b):
    return lax.dot_general(a.astype(BF16), b.astype(BF16), (((1,), (1,)), ((), ())),
                           preferred_element_type=F32)


def _dot_tn(a, b):
    return lax.dot_general(a.astype(BF16), b.astype(BF16), (((0,), (0,)), ((), ())),
                           preferred_element_type=F32)


def _split3(x):
    x1 = x.astype(BF16)
    r = x - x1.astype(F32)
    x2 = r.astype(BF16)
    x3 = (r - x2.astype(F32)).astype(BF16)
    return x1, x2, x3


def _cumsum_rows(tri_bf16, x):
    x1, x2, x3 = _split3(x)
    return (jnp.dot(tri_bf16, x1, preferred_element_type=F32)
            + jnp.dot(tri_bf16, x2, preferred_element_type=F32)
            + jnp.dot(tri_bf16, x3, preferred_element_type=F32))


def _multi_dot(a, b, keys, nt=False):
    dot = _dot_nt if nt else _dot
    return {k: dot(a[k], b[k]) for k in keys}


def _silu(x):
    return x * jax.nn.sigmoid(x)


def _softmax_rows(s):
    e = jnp.exp(s - jnp.max(s, axis=-1, keepdims=True))
    return e / jnp.sum(e, axis=-1, keepdims=True)


ROWS_PER_MEM = MEM_HEADS * MEM_HEAD_DIM // LANES


def _head_rows(h, j):
    return pl.ds(h + MEM_HEADS * j, N_MEM, stride=ROWS_PER_MEM)


def _head_block(ref, b, h):
    return jnp.concatenate([ref[b, _head_rows(h, j), :] for j in range(MEM_HEAD_DIM // LANES)], axis=1)


def _kv_proj_kernel(x_ref, wk_ref, wv_ref, k_ref, v_ref):
    xb = x_ref[0].astype(BF16)
    for w_ref, o_ref in ((wk_ref, k_ref), (wv_ref, v_ref)):
        res = jnp.dot(xb, w_ref[...], preferred_element_type=F32)
        for h in range(MEM_HEADS):
            for j in range(MEM_HEAD_DIM // LANES):
                c0 = h * MEM_HEAD_DIM + j * LANES
                o_ref[0, _head_rows(h, j), :] = res[:, c0:c0 + LANES]


def _kv_proj(mem3, w_kv):
    bsz, n_mem, d = mem3.shape
    n = w_kv.shape[1] // 2
    out_spec = pl.BlockSpec((1, n_mem * ROWS_PER_MEM, LANES), lambda i: (i, 0, 0))
    out_shape = jax.ShapeDtypeStruct((bsz, n_mem * ROWS_PER_MEM, LANES), F32)
    return pl.pallas_call(
        _kv_proj_kernel,
        grid=(bsz,),
        in_specs=[pl.BlockSpec((1, n_mem, d), lambda i: (i, 0, 0)),
                  pl.BlockSpec((d, n), lambda i: (0, 0)), pl.BlockSpec((d, n), lambda i: (0, 1))],
        out_specs=[out_spec, out_spec],
        out_shape=[out_shape, out_shape],
        compiler_params=_params(1),
        name="kv_proj",
    )(mem3, w_kv, w_kv)


PROJ_COLS = 8 * MIX_HALF
HG_COL = 0
GD_COL = 4 * MIX_HALF
HGRN_GROUP = 1
GDN_GROUP = 4


def _ref_rows(gc_ref, b, r0, col, m, c):
    blk = 2 * m
    if blk >= SUBLANES:
        parts = [jnp.broadcast_to(gc_ref[b, pl.ds(r0 + j * blk + m - 1, 1), col], (blk, HEAD_DIM))
                 for j in range(c // blk)]
    else:
        sub = lax.broadcasted_iota(jnp.int32, (SUBLANES, HEAD_DIM), 0)
        parts = []
        for i in range(c // SUBLANES):
            tile = None
            for j in range(SUBLANES // blk):
                row = jnp.broadcast_to(gc_ref[b, pl.ds(r0 + i * SUBLANES + j * blk + m - 1, 1), col],
                                       (SUBLANES, HEAD_DIM))
                tile = row if tile is None else jnp.where(sub >= j * blk, row, tile)
            parts.append(tile)
    return parts[0] if len(parts) == 1 else jnp.concatenate(parts, axis=0)


def _hgrn_rows(p_ref, mix_ref, lb, ng, st_ref, gc_ref, *, chunk_ids, c, nb, row_lo):
    rows = lax.broadcasted_iota(jnp.int32, (c, 1), 0)
    ti = lax.broadcasted_iota(jnp.int32, (c, c), 0)
    si = lax.broadcasted_iota(jnp.int32, (c, c), 1)
    tri = (si <= ti).astype(F32).astype(BF16)
    eye = ti == si
    xr = ti ^ si
    cols = [slice(h * HEAD_DIM, (h + 1) * HEAD_DIM) for h in range(N_HEADS)]
    chains = [(b, j, h) for b in range(nb) for j in chunk_ids for h in range(N_HEADS)]

    def proj(b, j, part):
        return p_ref[b, pl.ds(j * c, c), HG_COL + part * MIX_HALF:HG_COL + (part + 1) * MIX_HALF]

    q, k, gc = {}, {}, {}
    for b in range(nb):
        for j in chunk_ids:
            f = lb + (1.0 - lb) * jax.nn.sigmoid(proj(b, j, 1))
            g = jnp.log(f)
            kb = 1.0 - f
            if row_lo:
                g = jnp.where(rows >= row_lo, g, 0.0)
                kb = jnp.where(rows >= row_lo, kb, 0.0)
            qb = _silu(proj(b, j, 0))
            gcb = _cumsum_rows(tri, g)
            gc_ref[b, pl.ds(j * c, c), :] = gcb
            for h in range(N_HEADS):
                q[b, j, h], k[b, j, h], gc[b, j, h] = qb[:, cols[h]], kb[:, cols[h]], gcb[:, cols[h]]

    level_of = jnp.where(eye, 0, -1)
    m = 1
    while m < c:
        level_of = jnp.where((xr >= m) & (xr < 2 * m) & (si < ti), m, level_of)
        m *= 2
    rows_w = lax.broadcasted_iota(jnp.int32, (c, HEAD_DIM), 0)
    q_b = {ch: q[ch].astype(BF16) for ch in chains}
    k_b = {ch: k[ch].astype(BF16) for ch in chains}
    sc = _multi_dot(q_b, k_b, chains, nt=True)
    on_diag = level_of == 0
    sc = {ch: jnp.where(on_diag, sc[ch], 0.0) for ch in chains}
    m = c // 2
    while m >= 1:
        sign = jnp.where((rows_w & m) != 0, LOG2_E, -LOG2_E)
        qw, kw = {}, {}
        for ch in chains:
            b, j, h = ch
            w = jnp.exp2((gc[ch] - _ref_rows(gc_ref, b, j * c, cols[h], m, c)) * sign).astype(BF16)
            qw[ch], kw[ch] = q_b[ch] * w, k_b[ch] * w
        prod = _multi_dot(qw, kw, chains, nt=True)
        at_level = level_of == m
        sc = {ch: jnp.where(at_level, prod[ch], sc[ch]) for ch in chains}
        m //= 2

    v = {(b, j, h): p_ref[b, pl.ds(j * c, c),
                          HG_COL + 2 * MIX_HALF + h * HEAD_DIM:HG_COL + 2 * MIX_HALF + (h + 1) * HEAD_DIM]
         for b, j, h in chains}
    o_intra = _multi_dot(sc, v, chains)

    for j in chunk_ids:
        group = [(b, j, h) for b in range(nb) for h in range(N_HEADS)]
        st = {ch: st_ref[ch[0], ch[2]] for ch in group}
        o_inter = _multi_dot({ch: q[ch] * jnp.exp(gc[ch]) for ch in group}, st, group, nt=True)
        for ch in group:
            b, _, h = ch
            g_last = gc_ref[b, pl.ds(j * c + c - 1, 1), cols[h]]
            st_ref[b, h] = st[ch] * jnp.exp(g_last) + _dot_tn(v[ch], k[ch] * jnp.exp(g_last - gc[ch]))
        for ch in group:
            b, _, h = ch
            ob = o_intra[ch] + o_inter[ch]
            on = ob * lax.rsqrt(jnp.mean(ob * ob, axis=-1, keepdims=True) + RMS_EPS) * ng
            gate = p_ref[b, pl.ds(j * c, c),
                         HG_COL + 3 * MIX_HALF + h * HEAD_DIM:HG_COL + 3 * MIX_HALF + (h + 1) * HEAD_DIM]
            mix_ref[b, pl.ds(j * c, c), cols[h]] = on * _silu(gate)


def _gdn_rows(p_ref, t_ref, hist_ref, mix_ref, cb_ref, cw_ref, neg_a, dt_bias, ng, s_ref, xp_ref,
              *, c, nc, nb, row_lo):
    n_hist = GD_CONV - 1
    rb = c * nc
    rows = lax.broadcasted_iota(jnp.int32, (rb, 1), 0) % c
    ti = lax.broadcasted_iota(jnp.int32, (c, c), 0)
    si = lax.broadcasted_iota(jnp.int32, (c, c), 1)
    tri = (si <= ti).astype(F32).astype(BF16)
    eye_f = (ti == si).astype(F32)
    incl = si <= ti

    q_all, k_all, v_all, beta_all, la_all = {}, {}, {}, {}, {}
    for b in range(nb):
        qkv = []
        for j in range(3):
            col = slice(j * MIX_HALF, (j + 1) * MIX_HALF)
            x = p_ref[b, :, GD_COL + j * MIX_HALF:GD_COL + (j + 1) * MIX_HALF]
            if hist_ref is not None:
                x = jnp.where((rows >= row_lo - n_hist) & (rows < row_lo), hist_ref[b, :, col], x)
            xp_ref[b, SUBLANES:SUBLANES + rb, col] = x
            y = x * cw_ref[n_hist:n_hist + 1, col]
            for tap in range(n_hist):
                off = SUBLANES - n_hist + tap
                y = y + xp_ref[b, off:off + rb, col] * cw_ref[tap:tap + 1, col]
            qkv.append(_silu(y))
        cb_ref[b] = xp_ref[b, SUBLANES + rb - n_hist:SUBLANES + rb, :]
        xp_ref[b, 0:SUBLANES, :] = xp_ref[b, rb:rb + SUBLANES, :]
        q_all[b], k_all[b], v_all[b] = qkv

        tail = t_ref[b]
        beta_all[b] = jax.nn.sigmoid(tail)
        la_all[b] = neg_a * jax.nn.softplus(tail + dt_bias)
        if row_lo:
            beta_all[b] = jnp.where(rows >= row_lo, beta_all[b], 0.0)
            la_all[b] = jnp.where(rows >= row_lo, la_all[b], 0.0)

    for j0 in range(0, nc, GDN_GROUP):
        _gdn_chunks(list(range(j0, min(nc, j0 + GDN_GROUP))), q_all, k_all, v_all, beta_all, la_all,
                    p_ref, mix_ref, ng, s_ref, c=c, nb=nb)


def _gdn_chunks(chunk_ids, q_all, k_all, v_all, beta_all, la_all, p_ref, mix_ref, ng, s_ref, *, c, nb):
    ti = lax.broadcasted_iota(jnp.int32, (c, c), 0)
    si = lax.broadcasted_iota(jnp.int32, (c, c), 1)
    tri = (si <= ti).astype(F32).astype(BF16)
    eye_f = (ti == si).astype(F32)
    incl = si <= ti
    chains = [(b, j, h) for b in range(nb) for j in chunk_ids for h in range(N_HEADS)]

    gc = {(b, j): _cumsum_rows(tri, la_all[b][j * c:(j + 1) * c]) for b in range(nb) for j in chunk_ids}

    pre = {}
    for ch in chains:
        b, j, h = ch
        rs = slice(j * c, (j + 1) * c)
        col = slice(h * HEAD_DIM, (h + 1) * HEAD_DIM)
        qh, kh, vh = q_all[b][rs, col], k_all[b][rs, col], v_all[b][rs, col]
        qh = qh * lax.rsqrt(jnp.sum(qh * qh, axis=-1, keepdims=True) + RMS_EPS) * (HEAD_DIM ** -0.5)
        kh = kh * lax.rsqrt(jnp.sum(kh * kh, axis=-1, keepdims=True) + RMS_EPS)
        beta = beta_all[b][rs, h:h + 1]
        gcol = gc[b, j][:, N_HEADS + h:N_HEADS + h + 1]
        grow = jnp.sum(eye_f * gcol, axis=0, keepdims=True)
        decay = jnp.where(incl, jnp.exp(jnp.where(incl, gcol - grow, 0.0)), 0.0)
        gamma = jnp.exp(gcol)
        g_last = gcol[c - 1:c, :]
        pre[ch] = dict(
            q=qh, k=kh, beta=beta, decay=decay,
            rhs=jnp.concatenate([(beta * gamma) * kh, beta * vh], axis=1).astype(BF16),
            gq=(gamma * qh).astype(BF16),
            kd=(kh * jnp.exp(g_last - gcol)).astype(BF16),
            eg=jnp.exp(g_last))
    k_b = {ch: pre[ch]["k"].astype(BF16) for ch in chains}
    kk = _multi_dot(k_b, k_b, chains, nt=True)
    qk = _multi_dot({ch: pre[ch]["q"] for ch in chains}, k_b, chains, nt=True)
    a_mat = {ch: (pre[ch]["beta"] * kk[ch] * pre[ch]["decay"]).astype(BF16) for ch in chains}
    aqk = {ch: (qk[ch] * pre[ch]["decay"]).astype(BF16) for ch in chains}

    xr = ti ^ si
    first = (xr < 2) & (si < ti)
    zero_b = jnp.zeros((c, c), BF16)
    t_inv = {ch: (eye_f - jnp.where(first, a_mat[ch], zero_b).astype(F32)).astype(BF16) for ch in chains}
    n = 2
    while n < c:
        lower = (xr < 2 * n) & ((ti & n) != 0) & ((si & n) == 0)
        x = _multi_dot({ch: jnp.where(lower, a_mat[ch], zero_b) for ch in chains}, t_inv, chains)
        tx = _multi_dot(t_inv, x, chains)
        t_inv = {ch: (t_inv[ch].astype(F32) - tx[ch]).astype(BF16) for ch in chains}
        n *= 2
    wu = {ch: _dot(t_inv[ch], pre[ch]["rhs"]) for ch in chains}

    for j in chunk_ids:
        rs = slice(j * c, (j + 1) * c)
        group = [(b, j, h) for b in range(nb) for h in range(N_HEADS)]
        s_old = {ch: s_ref[ch[0], ch[2]] for ch in group}
        ws = _multi_dot({ch: jnp.concatenate([wu[ch][:, :HEAD_DIM].astype(BF16), pre[ch]["gq"]], axis=0)
                         for ch in group}, s_old, group)
        u = {ch: (wu[ch][:, HEAD_DIM:] - ws[ch][:c]).astype(BF16) for ch in group}
        au = _multi_dot(aqk, u, group)
        for ch in group:
            b, _, h = ch
            s_ref[b, h] = pre[ch]["eg"] * s_old[ch] + _dot_tn(pre[ch]["kd"], u[ch])
            o = ws[ch][c:] + au[ch]
            on = o * lax.rsqrt(jnp.mean(o * o, axis=-1, keepdims=True) + RMS_EPS) * ng
            gate = p_ref[b, rs, GD_COL + 3 * MIX_HALF + h * HEAD_DIM:GD_COL + 3 * MIX_HALF + (h + 1) * HEAD_DIM]
            mix_ref[b, rs, MIX_HALF + h * HEAD_DIM:MIX_HALF + (h + 1) * HEAD_DIM] = on * _silu(gate)


def _mixer_kernel(*refs, c, nc, nb, fuse_proj, has_state, row_lo, n_steps):
    refs = list(refs)
    if fuse_proj:
        x_ref, sx_ref, w_ref, wt_ref = refs[:4]
        del refs[:4]
    else:
        p_ref, t_ref = refs[:2]
        del refs[:2]
    lbl_ref, hng_ref, cw_ref, par_ref, gng_ref = refs[:5]
    del refs[:5]
    if has_state:
        hist_ref, shg0_ref, sgd0_ref = refs[:3]
        del refs[:3]
    else:
        hist_ref = shg0_ref = sgd0_ref = None
    if fuse_proj:
        mix_ref, shg_ref, sgd_ref, cb_ref, sp_ref, stl_ref, st_ref, s_ref, gc_ref, xp_ref, p_ref, t_ref = refs
    else:
        mix_ref, shg_ref, sgd_ref, cb_ref, st_ref, s_ref, gc_ref, xp_ref = refs
    step = pl.program_id(1)
    pairs = [(b, h) for b in range(nb) for h in range(N_HEADS)]

    @pl.when(step == 0)
    def _():
        for b, h in pairs:
            if has_state:
                st_ref[b, h] = shg0_ref[b, h].T
                s_ref[b, h] = sgd0_ref[b, h]
            else:
                st_ref[b, h] = jnp.zeros((HEAD_DIM, HEAD_DIM), F32)
                s_ref[b, h] = jnp.zeros((HEAD_DIM, HEAD_DIM), F32)
        for b in range(nb):
            xp_ref[b, 0:SUBLANES, :] = jnp.zeros((SUBLANES, 3 * MIX_HALF), F32)

    if fuse_proj:
        rb = x_ref.shape[1]
        xb = jnp.concatenate([x_ref[0], sx_ref[...]], axis=0).astype(BF16)
        proj = jnp.dot(xb, w_ref[...], preferred_element_type=F32)
        tail = jnp.dot(xb, wt_ref[...], preferred_element_type=F32)
        p_ref[0] = proj[:rb]
        t_ref[0] = tail[:rb]
        sp_ref[...] = proj[rb:]
        stl_ref[...] = tail[rb:]

    lbl = lbl_ref[...]
    e = jnp.exp(lbl - jnp.max(lbl, axis=0, keepdims=True))
    lb = e[0:1] / jnp.sum(e, axis=0, keepdims=True)
    neg_a = -jnp.exp(par_ref[0:1, :])
    dt_bias = par_ref[1:2, :]

    for j0 in range(0, nc, HGRN_GROUP):
        _hgrn_rows(p_ref, mix_ref, lb, hng_ref[...], st_ref, gc_ref,
                   chunk_ids=list(range(j0, min(nc, j0 + HGRN_GROUP))), c=c, nb=nb, row_lo=row_lo)
    _gdn_rows(p_ref, t_ref, hist_ref, mix_ref, cb_ref, cw_ref, neg_a, dt_bias, gng_ref[...], s_ref, xp_ref,
              c=c, nc=nc, nb=nb, row_lo=row_lo)

    @pl.when(step == n_steps - 1)
    def _():
        for b, h in pairs:
            shg_ref[b, h] = st_ref[b, h].T
            sgd_ref[b, h] = s_ref[b, h]


def _mixer(x3, proj, tail, weights, hist, s_hg, s_gd, *, c, nc, nb, row_lo, side_x=None):
    lb_logits, w_main, w_tail, conv_w, a_log, dt_bias, hg_norm_g, gd_norm_g = weights
    fuse_proj = proj is None
    bsz, t = (x3 if fuse_proj else proj).shape[:2]
    rb = c * nc
    n_steps = t // rb
    has_state = s_hg is not None
    n_hist = GD_CONV - 1
    const = lambda shape: pl.BlockSpec(shape, lambda i, si: (0,) * len(shape))
    rows3 = lambda width: pl.BlockSpec((nb, rb, width), lambda i, si: (i, si, 0))
    st_spec = pl.BlockSpec((nb, N_HEADS, HEAD_DIM, HEAD_DIM), lambda i, si: (i, 0, 0, 0))
    par = jnp.zeros((SUBLANES, HEAD_DIM), F32)
    par = par.at[0, N_HEADS:2 * N_HEADS].set(a_log).at[1, N_HEADS:2 * N_HEADS].set(dt_bias)
    if fuse_proj:
        assert nb == 1 and side_x.shape[0] % (bsz * n_steps * SUBLANES) == 0
        ns = side_x.shape[0] // (bsz * n_steps)
        side = lambda width: pl.BlockSpec((ns, width), lambda i, si: (i * n_steps + si, 0))
        in_specs = [rows3(D_MODEL), side(D_MODEL), const((D_MODEL, PROJ_COLS)), const(w_tail.shape)]
        args = [x3, side_x, w_main, w_tail]
    else:
        in_specs = [rows3(PROJ_COLS), rows3(HEAD_DIM)]
        args = [proj, tail]
    in_specs += [const(lb_logits.shape), const((1, HEAD_DIM)), const(conv_w.shape), const(par.shape),
                 const((1, HEAD_DIM))]
    args += [lb_logits, hg_norm_g.reshape(1, HEAD_DIM), conv_w, par, gd_norm_g.reshape(1, HEAD_DIM)]
    if has_state:
        in_specs += [rows3(3 * MIX_HALF), st_spec, st_spec]
        args += [hist, s_hg, s_gd]
    scratch = [pltpu.VMEM((nb, N_HEADS, HEAD_DIM, HEAD_DIM), F32),
               pltpu.VMEM((nb, N_HEADS, HEAD_DIM, HEAD_DIM), F32),
               pltpu.VMEM((nb, rb, MIX_HALF), F32),
               pltpu.VMEM((nb, SUBLANES + rb, 3 * MIX_HALF), F32)]
    if fuse_proj:
        scratch += [pltpu.VMEM((nb, rb, PROJ_COLS), F32), pltpu.VMEM((nb, rb, HEAD_DIM), F32)]
    out_specs = [rows3(D_MODEL), st_spec, st_spec,
                 pl.BlockSpec((nb, n_hist, 3 * MIX_HALF), lambda i, si: (i, 0, 0))]
    out_shape = [jax.ShapeDtypeStruct((bsz, t, D_MODEL), F32),
                 jax.ShapeDtypeStruct((bsz, N_HEADS, HEAD_DIM, HEAD_DIM), F32),
                 jax.ShapeDtypeStruct((bsz, N_HEADS, HEAD_DIM, HEAD_DIM), F32),
                 jax.ShapeDtypeStruct((bsz, n_hist, 3 * MIX_HALF), F32)]
    if fuse_proj:
        out_specs += [side(PROJ_COLS), side(HEAD_DIM)]
        out_shape += [jax.ShapeDtypeStruct((side_x.shape[0], PROJ_COLS), F32),
                      jax.ShapeDtypeStruct((side_x.shape[0], HEAD_DIM), F32)]
    return pl.pallas_call(
        functools.partial(_mixer_kernel, c=c, nc=nc, nb=nb, fuse_proj=fuse_proj, has_state=has_state,
                          row_lo=row_lo, n_steps=n_steps),
        grid=(bsz // nb, n_steps),
        in_specs=in_specs,
        out_specs=out_specs,
        out_shape=out_shape,
        scratch_shapes=scratch,
        compiler_params=_params(2),
        name="mixer",
    )(*args)


def _attn_block_kernel(mix_ref, x_ref, k_ref, v_ref, wo_ref, wq_ref, wm_ref, g1_ref, b1_ref, g2_ref, b2_ref,
                       smix_ref, sx_ref, sk_ref, sv_ref, o_ref, so_ref, *, n_sub):
    def layer_norm(acc, g_ref, b_ref):
        mu = jnp.mean(acc, axis=-1, keepdims=True)
        xc = acc - mu
        var = jnp.mean(xc * xc, axis=-1, keepdims=True)
        return xc * lax.rsqrt(var + LN_EPS) * g_ref[...] + b_ref[...]

    nb, st, d = sx_ref.shape
    ts = x_ref.shape[1] // n_sub
    subs = [pl.ds(i * ts, ts) for i in range(n_sub)]
    scale = MEM_HEAD_DIM ** -0.5
    cols = [slice(h * MEM_HEAD_DIM, (h + 1) * MEM_HEAD_DIM) for h in range(MEM_HEADS)]
    heads = range(MEM_HEADS)
    kh = [_head_block(k_ref, 0, h).astype(BF16) for h in heads]
    vh = [_head_block(v_ref, 0, h).astype(BF16) for h in heads]

    x_rows = [x_ref[0, r, :] for r in subs]
    mix_rows = [mix_ref[0, r, :] for r in subs]
    x_rows[0] = jnp.concatenate([x_rows[0], sx_ref[...].reshape(nb * st, d)], axis=0)
    mix_rows[0] = jnp.concatenate([mix_rows[0], smix_ref[...].reshape(nb * st, d)], axis=0)

    h1 = [layer_norm(ALPHA * xr + jnp.dot(mr.astype(BF16), wo_ref[...], preferred_element_type=F32),
                     g1_ref, b1_ref) for xr, mr in zip(x_rows, mix_rows)]
    q = [jnp.dot(hi.astype(BF16), wq_ref[...], preferred_element_type=F32).astype(BF16) for hi in h1]
    s_main = [[_dot_nt(qi[:ts, cols[h]], kh[h]) * scale for h in heads] for qi in q]
    pairs = [(b, h) for b in range(nb) for h in heads]
    s_side = {(b, h): _dot_nt(q[0][ts + b * st:ts + (b + 1) * st, cols[h]], _head_block(sk_ref, b, h)) * scale
              for b, h in pairs}
    p_main = [[_softmax_rows(sh) for sh in si] for si in s_main]
    p_side = {bh: _softmax_rows(s_side[bh]) for bh in pairs}
    att = [jnp.concatenate([_dot(pi[h], vh[h]).astype(BF16) for h in heads], axis=1) for pi in p_main]
    att_side = [jnp.concatenate([_dot(p_side[b, h], _head_block(sv_ref, b, h)).astype(BF16) for h in heads],
                                axis=1) for b in range(nb)]
    att[0] = jnp.concatenate([att[0]] + att_side, axis=0)
    h2 = [layer_norm(ALPHA * hi + jnp.dot(ai, wm_ref[...], preferred_element_type=F32), g2_ref, b2_ref)
          for hi, ai in zip(h1, att)]
    for r, hi in zip(subs, h2):
        o_ref[0, r, :] = hi[:ts]
    so_ref[...] = h2[0][ts:].reshape(nb, st, d)


def _attn_block(mix3, x3, mk, mv, w_out, w_mq, w_mo, ln1_g, ln1_b, ln2_g, ln2_b, side_mix, side_x, side_k,
                side_v, *, tm, n_sub):
    bsz, t, d = x3.shape
    tm = min(tm, t)
    n_steps = bsz * (t // tm)
    bs, ts, _ = side_x.shape
    assert bs % n_steps == 0
    nb = bs // n_steps
    flat = lambda i, r: (i * (t // tm) + r, 0, 0)
    side_rows = pl.BlockSpec((nb, ts, d), flat)
    side_kv = pl.BlockSpec((nb,) + side_k.shape[1:], flat)
    rows = pl.BlockSpec((1, tm, d), lambda i, r: (i, r, 0))
    kv = pl.BlockSpec((1,) + mk.shape[1:], lambda i, r: (i, 0, 0))
    wsp = pl.BlockSpec((d, d), lambda i, r: (0, 0))
    vec = pl.BlockSpec((1, d), lambda i, r: (0, 0))
    return pl.pallas_call(
        functools.partial(_attn_block_kernel, n_sub=n_sub),
        grid=(bsz, t // tm),
        in_specs=[rows, rows, kv, kv, wsp, wsp, wsp, vec, vec, vec, vec, side_rows, side_rows, side_kv, side_kv],
        out_specs=[rows, side_rows],
        out_shape=[jax.ShapeDtypeStruct((bsz, t, d), F32), jax.ShapeDtypeStruct((bs, ts, d), F32)],
        compiler_params=_params(2),
        name="attn_block",
    )(mix3, x3, mk, mv, w_out, w_mq, w_mo, ln1_g.reshape(1, d), ln1_b.reshape(1, d),
      ln2_g.reshape(1, d), ln2_b.reshape(1, d), side_mix, side_x, side_k, side_v)


def _ffn_block_kernel(h_ref, wg_ref, wv_ref, wd_ref, cw_ref, cb_ref, g_ref, b_ref, sh_ref, shist_ref,
                      y_ref, buf_ref, sy_ref, sgate_ref, xp_ref, sxp_ref, *, tm, n_split, grp, row_lo):
    n_hist = FFN_CONV - 1
    f = xp_ref.shape[1]
    ns = sh_ref.shape[0]
    tiles = f // MXU_TILE
    bounds = [MXU_TILE * -(-tiles * j // n_split) for j in range(n_split)] + [f]

    @pl.when(pl.program_id(1) == 0)
    def _():
        xp_ref[0:SUBLANES, :] = jnp.zeros((SUBLANES, f), F32)
        sxp_ref[0:SUBLANES, :] = jnp.zeros((SUBLANES, f), F32)

    h = jnp.concatenate([h_ref[0], sh_ref[...]], axis=0)
    hb = h.astype(BF16)
    cols = [slice(bounds[j], bounds[j + 1]) for j in range(n_split)]
    pos = lax.broadcasted_iota(jnp.int32, (ns, 1), 0) % grp
    is_hist = (pos >= row_lo - n_hist) & (pos < row_lo)

    def conv(ref, rows, gate, c):
        ref[SUBLANES:SUBLANES + rows, c] = gate
        y = gate * cw_ref[n_hist:n_hist + 1, c] + cb_ref[:, c]
        for tap in range(n_hist):
            off = SUBLANES - n_hist + tap
            y = y + ref[off:off + rows, c] * cw_ref[tap:tap + 1, c]
        return y

    gate = [jnp.dot(hb, wg_ref[:, c], preferred_element_type=F32) for c in cols]
    val = [jnp.dot(hb, wv_ref[:, c], preferred_element_type=F32) for c in cols]
    act = []
    for c, gj, vj in zip(cols, gate, val):
        side_gate = jnp.where(is_hist, shist_ref[:, c], gj[tm:])
        sgate_ref[:, c] = side_gate
        y = jnp.concatenate([conv(xp_ref, tm, gj[:tm], c), conv(sxp_ref, ns, side_gate, c)], axis=0)
        gelu = 0.5 * y * (1.0 + lax.erf(y * (2.0 ** -0.5)))
        act.append((gelu * vj).astype(BF16))
    acc = ALPHA * h
    for c, aj in zip(cols, act):
        acc = acc + jnp.dot(aj, wd_ref[c, :], preferred_element_type=F32)
    mu = jnp.mean(acc, axis=-1, keepdims=True)
    xc = acc - mu
    var = jnp.mean(xc * xc, axis=-1, keepdims=True)
    out = xc * lax.rsqrt(var + LN_EPS) * g_ref[...] + b_ref[...]
    y_ref[0] = out[:tm]
    sy_ref[...] = out[tm:]
    buf_ref[0] = xp_ref[SUBLANES + tm - n_hist:SUBLANES + tm, :]
    xp_ref[0:SUBLANES, :] = xp_ref[tm:tm + SUBLANES, :]


def _ffn_block(h3, w_up, w_down, conv_w, conv_b, ln_g, ln_b, side_h, side_hist, *, tm, n_split, grp, row_lo):
    bsz, t, d = h3.shape
    f = w_down.shape[0]
    tm = min(tm, t)
    n_hist = FFN_CONV - 1
    n_steps = bsz * (t // tm)
    rs = side_h.shape[0]
    assert rs % (n_steps * grp) == 0
    ns = rs // n_steps
    once = pl.Buffered(1)
    const = lambda shape: pl.BlockSpec(shape, lambda i, r: (0,) * len(shape), pipeline_mode=once)
    rows = pl.BlockSpec((1, tm, d), lambda i, r: (i, r, 0))
    side = lambda width: pl.BlockSpec((ns, width), lambda i, r: (i * (t // tm) + r, 0))
    return pl.pallas_call(
        functools.partial(_ffn_block_kernel, tm=tm, n_split=n_split, grp=grp, row_lo=row_lo),
        grid=(bsz, t // tm),
        in_specs=[rows, const((d, f)), pl.BlockSpec((d, f), lambda i, r: (0, 1), pipeline_mode=once),
                  const((f, d)), const((FFN_CONV, f)), const((1, f)),
                  const((1, d)), const((1, d)), side(d), side(f)],
        out_specs=[rows, pl.BlockSpec((1, n_hist, f), lambda i, r: (i, 0, 0)), side(d), side(f)],
        out_shape=[jax.ShapeDtypeStruct((bsz, t, d), F32), jax.ShapeDtypeStruct((bsz, n_hist, f), F32),
                   jax.ShapeDtypeStruct((rs, d), F32), jax.ShapeDtypeStruct((rs, f), F32)],
        scratch_shapes=[pltpu.VMEM((SUBLANES + tm, f), F32), pltpu.VMEM((SUBLANES + ns, f), F32)],
        compiler_params=_params(2),
        name="ffn_block",
    )(h3, w_up, w_up, w_down, conv_w, conv_b.reshape(1, f), ln_g.reshape(1, d), ln_b.reshape(1, d),
      side_h, side_hist)


def _cache_rows(cache):
    b, m, h, d = cache.shape
    return cache.reshape(b, m, h, d // LANES, LANES).transpose(0, 1, 3, 2, 4).reshape(b, m * h * d // LANES, LANES)


def _cache_from_rows(rows):
    b = rows.shape[0]
    halves = MEM_HEAD_DIM // LANES
    return (rows.reshape(b, N_MEM, halves, MEM_HEADS, LANES).transpose(0, 1, 3, 2, 4)
            .reshape(b, N_MEM, MEM_HEADS, MEM_HEAD_DIM))


def kernel(x_prompt, x_sample, state_hgrn, state_gdn, state_gdn_conv, state_ffn_conv, cache_mem_k, cache_mem_v, mem_prompt, hgrn_lb_logits, w_in, w_gd_conv, gd_a_log, gd_dt_bias, hg_norm_g, gd_norm_g, w_out, ln1_g, ln1_b, w_mq, w_mkv, w_mo, ln2_g, ln2_b, w_up, w_ffn_conv, b_ffn_conv, w_down, ln3_g, ln3_b):
    bp, tp, d = x_prompt.shape
    bs, ts, _ = x_sample.shape
    tpad = SUBLANES
    row_lo = tpad - ts
    l = 0

    w_in_b = w_in[l].astype(BF16)
    w_in_tail = jnp.pad(w_in_b[:, PROJ_COLS:], ((0, 0), (0, HEAD_DIM - 2 * N_HEADS)))
    mixer_w = (hgrn_lb_logits, w_in_b, w_in_tail, w_gd_conv[l],
               gd_a_log[l], gd_dt_bias[l], hg_norm_g[l], gd_norm_g[l])
    (w_out_b, w_mq_b, w_mo_b, w_up_b, w_down_b) = (w.astype(BF16) for w in
                                                    (w_out[l], w_mq[l], w_mo[l], w_up[l], w_down[l]))

    n_hg = GD_CONV - 1
    n_hf = FFN_CONV - 1
    xs = jnp.pad(x_sample, ((0, 0), (row_lo, 0), (0, 0)))
    hist_gd = jnp.pad(state_gdn_conv[l], ((0, 0), (row_lo - n_hg, ts), (0, 0)))
    hist_ffn = jnp.pad(state_ffn_conv[l], ((0, 0), (row_lo - n_hf, ts), (0, 0)))

    mk, mv = _kv_proj(mem_prompt, w_mkv[l].astype(BF16))
    mix_p, p_hg, p_gd, p_bgd, proj_s, tail_s = _mixer(x_prompt, None, None, mixer_w, None, None, None,
                                                      c=128, nc=4, nb=1, row_lo=0,
                                                      side_x=xs.reshape(bs * tpad, d))
    mix_s, s_hg, s_gd, s_bgd = _mixer(None, proj_s.reshape(bs, tpad, PROJ_COLS), tail_s.reshape(bs, tpad, HEAD_DIM),
                                      mixer_w, hist_gd, state_hgrn[l], state_gdn[l],
                                      c=tpad, nc=1, nb=8, row_lo=row_lo)
    h2_p, h2_s = _attn_block(mix_p, x_prompt, mk, mv, w_out_b, w_mq_b, w_mo_b, ln1_g[l], ln1_b[l],
                             ln2_g[l], ln2_b[l], mix_s, xs, _cache_rows(cache_mem_k[l]),
                             _cache_rows(cache_mem_v[l]), tm=512, n_sub=2)
    yp, p_bff, ys, gate_s = _ffn_block(h2_p, w_up_b, w_down_b, w_ffn_conv[l], b_ffn_conv[l], ln3_g[l], ln3_b[l],
                                       h2_s.reshape(bs * tpad, d), hist_ffn.reshape(bs * tpad, D_FF),
                                       tm=512, n_split=2, grp=tpad, row_lo=row_lo)
    ys = ys.reshape(bs, tpad, d)[:, row_lo:]
    s_bff = gate_s.reshape(bs, tpad, D_FF)[:, tpad - n_hf:]

    return (yp, ys, p_hg[None], p_gd[None], p_bgd[None], p_bff[None],
            _cache_from_rows(mk)[None], _cache_from_rows(mv)[None],
            s_hg[None], s_gd[None], s_bgd[None], s_bff[None])
```

```python
import functools

import jax
import jax.numpy as jnp
from jax import lax
from jax.experimental import pallas as pl
from jax.experimental.pallas import tpu as pltpu

F32 = jnp.float32
BF16 = jnp.bfloat16

D_MODEL = 1024
HEAD_DIM = 128
N_HEADS = 4
MIX_HALF = N_HEADS * HEAD_DIM
GD_CONV = 4
FFN_CONV = 3
D_FF = 2816
N_MEM = 256
MEM_HEADS = 4
MEM_HEAD_DIM = D_MODEL // MEM_HEADS
LN_EPS = 1e-5
RMS_EPS = 1e-6
LOG2_E = 1.4426950408889634
DEPTH = 1
ALPHA = (2.0 * DEPTH) ** 0.25

SUBLANES = 8
LANES = 128
MXU_TILE = 256
VMEM_LIMIT = 56 * 1024 * 1024


def _params(n_axes):
    return pltpu.CompilerParams(dimension_semantics=("arbitrary",) * n_axes,
                                vmem_limit_bytes=VMEM_LIMIT)


def _dot(a, b):
    return jnp.dot(a.astype(BF16), b.astype(BF16), preferred_element_type=F32)


def _dot_nt(a, b):
    return lax.dot_general(a.astype(BF16), b.astype(BF16), (((1,), (1,)), ((), ())),
                           preferred_element_type=F32)


def _dot_tn(a, b):
    return lax.dot_general(a.astype(BF16), b.astype(BF16), (((0,), (0,)), ((), ())),
                           preferred_element_type=F32)


def _split3(x):
    x1 = x.astype(BF16)
    r = x - x1.astype(F32)
    x2 = r.astype(BF16)
    x3 = (r - x2.astype(F32)).astype(BF16)
    return x1, x2, x3


def _cumsum_rows(tri_bf16, x):
    x1, x2, x3 = _split3(x)
    return (jnp.dot(tri_bf16, x1, preferred_element_type=F32)
            + jnp.dot(tri_bf16, x2, preferred_element_type=F32)
            + jnp.dot(tri_bf16, x3, preferred_element_type=F32))


def _multi_dot(a, b, keys, nt=False):
    dot = _dot_nt if nt else _dot
    return {k: dot(a[k], b[k]) for k in keys}


def _silu(x):
    return x * jax.nn.sigmoid(x)


def _softmax_rows(s):
    e = jnp.exp(s - jnp.max(s, axis=-1, keepdims=True))
    return e / jnp.sum(e, axis=-1, keepdims=True)


ROWS_PER_MEM = MEM_HEADS * MEM_HEAD_DIM // LANES


def _head_rows(h, j):
    return pl.ds(h + MEM_HEADS * j, N_MEM, stride=ROWS_PER_MEM)


def _head_block(ref, b, h):
    return jnp.concatenate([ref[b, _head_rows(h, j), :] for j in range(MEM_HEAD_DIM // LANES)], axis=1)


def _kv_proj_kernel(x_ref, wk_ref, wv_ref, k_ref, v_ref):
    xb = x_ref[0].astype(BF16)
    for w_ref, o_ref in ((wk_ref, k_ref), (wv_ref, v_ref)):
        res = jnp.dot(xb, w_ref[...], preferred_element_type=F32)
        for h in range(MEM_HEADS):
            for j in range(MEM_HEAD_DIM // LANES):
                c0 = h * MEM_HEAD_DIM + j * LANES
                o_ref[0, _head_rows(h, j), :] = res[:, c0:c0 + LANES]


def _kv_proj(mem3, w_kv):
    bsz, n_mem, d = mem3.shape
    n = w_kv.shape[1] // 2
    out_spec = pl.BlockSpec((1, n_mem * ROWS_PER_MEM, LANES), lambda i: (i, 0, 0))
    out_shape = jax.ShapeDtypeStruct((bsz, n_mem * ROWS_PER_MEM, LANES), F32)
    return pl.pallas_call(
        _kv_proj_kernel,
        grid=(bsz,),
        in_specs=[pl.BlockSpec((1, n_mem, d), lambda i: (i, 0, 0)),
                  pl.BlockSpec((d, n), lambda i: (0, 0)), pl.BlockSpec((d, n), lambda i: (0, 1))],
        out_specs=[out_spec, out_spec],
        out_shape=[out_shape, out_shape],
        compiler_params=_params(1),
        name="kv_proj",
    )(mem3, w_kv, w_kv)


PROJ_COLS = 8 * MIX_HALF
HG_COL = 0
GD_COL = 4 * MIX_HALF
HGRN_GROUP = 1
GDN_GROUP = 4


def _ref_rows(gc_ref, b, r0, col, m, c):
    blk = 2 * m
    if blk >= SUBLANES:
        parts = [jnp.broadcast_to(gc_ref[b, pl.ds(r0 + j * blk + m - 1, 1), col], (blk, HEAD_DIM))
                 for j in range(c // blk)]
    else:
        sub = lax.broadcasted_iota(jnp.int32, (SUBLANES, HEAD_DIM), 0)
        parts = []
        for i in range(c // SUBLANES):
            tile = None
            for j in range(SUBLANES // blk):
                row = jnp.broadcast_to(gc_ref[b, pl.ds(r0 + i * SUBLANES + j * blk + m - 1, 1), col],
                                       (SUBLANES, HEAD_DIM))
                tile = row if tile is None else jnp.where(sub >= j * blk, row, tile)
            parts.append(tile)
    return parts[0] if len(parts) == 1 else jnp.concatenate(parts, axis=0)


def _hgrn_rows(p_ref, mix_ref, lb, ng, st_ref, gc_ref, *, chunk_ids, c, nb, row_lo):
    rows = lax.broadcasted_iota(jnp.int32, (c, 1), 0)
    ti = lax.broadcasted_iota(jnp.int32, (c, c), 0)
    si = lax.broadcasted_iota(jnp.int32, (c, c), 1)
    tri = (si <= ti).astype(F32).astype(BF16)
    eye = ti == si
    xr = ti ^ si
    cols = [slice(h * HEAD_DIM, (h + 1) * HEAD_DIM) for h in range(N_HEADS)]
    chains = [(b, j, h) for b in range(nb) for j in chunk_ids for h in range(N_HEADS)]

    def proj(b, j, part):
        return p_ref[b, pl.ds(j * c, c), HG_COL + part * MIX_HALF:HG_COL + (part + 1) * MIX_HALF]

    q, k, gc = {}, {}, {}
    for b in range(nb):
        for j in chunk_ids:
            f = lb + (1.0 - lb) * jax.nn.sigmoid(proj(b, j, 1))
            g = jnp.log(f)
            kb = 1.0 - f
            if row_lo:
                g = jnp.where(rows >= row_lo, g, 0.0)
                kb = jnp.where(rows >= row_lo, kb, 0.0)
            qb = _silu(proj(b, j, 0))
            gcb = _cumsum_rows(tri, g)
            gc_ref[b, pl.ds(j * c, c), :] = gcb
            for h in range(N_HEADS):
                q[b, j, h], k[b, j, h], gc[b, j, h] = qb[:, cols[h]], kb[:, cols[h]], gcb[:, cols[h]]

    level_of = jnp.where(eye, 0, -1)
    m = 1
    while m < c:
        level_of = jnp.where((xr >= m) & (xr < 2 * m) & (si < ti), m, level_of)
        m *= 2
    rows_w = lax.broadcasted_iota(jnp.int32, (c, HEAD_DIM), 0)
    q_b = {ch: q[ch].astype(BF16) for ch in chains}
    k_b = {ch: k[ch].astype(BF16) for ch in chains}
    sc = _multi_dot(q_b, k_b, chains, nt=True)
    on_diag = level_of == 0
    sc = {ch: jnp.where(on_diag, sc[ch], 0.0) for ch in chains}
    m = c // 2
    while m >= 1:
        sign = jnp.where((rows_w & m) != 0, LOG2_E, -LOG2_E)
        qw, kw = {}, {}
        for ch in chains:
            b, j, h = ch
            w = jnp.exp2((gc[ch] - _ref_rows(gc_ref, b, j * c, cols[h], m, c)) * sign).astype(BF16)
            qw[ch], kw[ch] = q_b[ch] * w, k_b[ch] * w
        prod = _multi_dot(qw, kw, chains, nt=True)
        at_level = level_of == m
        sc = {ch: jnp.where(at_level, prod[ch], sc[ch]) for ch in chains}
        m //= 2

    v = {(b, j, h): p_ref[b, pl.ds(j * c, c),
                          HG_COL + 2 * MIX_HALF + h * HEAD_DIM:HG_COL + 2 * MIX_HALF + (h + 1) * HEAD_DIM]
         for b, j, h in chains}
    o_intra = _multi_dot(sc, v, chains)

    for j in chunk_ids:
        group = [(b, j, h) for b in range(nb) for h in range(N_HEADS)]
        st = {ch: st_ref[ch[0], ch[2]] for ch in group}
        o_inter = _multi_dot({ch: q[ch] * jnp.exp(gc[ch]) for ch in group}, st, group, nt=True)
        for ch in group:
            b, _, h = ch
            g_last = gc_ref[b, pl.ds(j * c + c - 1, 1), cols[h]]
            st_ref[b, h] = st[ch] * jnp.exp(g_last) + _dot_tn(v[ch], k[ch] * jnp.exp(g_last - gc[ch]))
        for ch in group:
            b, _, h = ch
            ob = o_intra[ch] + o_inter[ch]
            on = ob * lax.rsqrt(jnp.mean(ob * ob, axis=-1, keepdims=True) + RMS_EPS) * ng
            gate = p_ref[b, pl.ds(j * c, c),
                         HG_COL + 3 * MIX_HALF + h * HEAD_DIM:HG_COL + 3 * MIX_HALF + (h + 1) * HEAD_DIM]
            mix_ref[b, pl.ds(j * c, c), cols[h]] = on * _silu(gate)


def _gdn_rows(p_ref, t_ref, hist_ref, mix_ref, cb_ref, cw_ref, neg_a, dt_bias, ng, s_ref, xp_ref,
              *, c, nc, nb, row_lo):
    n_hist = GD_CONV - 1
    rb = c * nc
    rows = lax.broadcasted_iota(jnp.int32, (rb, 1), 0) % c
    ti = lax.broadcasted_iota(jnp.int32, (c, c), 0)
    si = lax.broadcasted_iota(jnp.int32, (c, c), 1)
    tri = (si <= ti).astype(F32).astype(BF16)
    eye_f = (ti == si).astype(F32)
    incl = si <= ti

    q_all, k_all, v_all, beta_all, la_all = {}, {}, {}, {}, {}
    for b in range(nb):
        qkv = []
        for j in range(3):
            col = slice(j * MIX_HALF, (j + 1) * MIX_HALF)
            x = p_ref[b, :, GD_COL + j * MIX_HALF:GD_COL + (j + 1) * MIX_HALF]
            if hist_ref is not None:
                x = jnp.where((rows >= row_lo - n_hist) & (rows < row_lo), hist_ref[b, :, col], x)
            xp_ref[b, SUBLANES:SUBLANES + rb, col] = x
            y = x * cw_ref[n_hist:n_hist + 1, col]
            for tap in range(n_hist):
                off = SUBLANES - n_hist + tap
                y = y + xp_ref[b, off:off + rb, col] * cw_ref[tap:tap + 1, col]
            qkv.append(_silu(y))
        cb_ref[b] = xp_ref[b, SUBLANES + rb - n_hist:SUBLANES + rb, :]
        xp_ref[b, 0:SUBLANES, :] = xp_ref[b, rb:rb + SUBLANES, :]
        q_all[b], k_all[b], v_all[b] = qkv

        tail = t_ref[b]
        beta_all[b] = jax.nn.sigmoid(tail)
        la_all[b] = neg_a * jax.nn.softplus(tail + dt_bias)
        if row_lo:
            beta_all[b] = jnp.where(rows >= row_lo, beta_all[b], 0.0)
            la_all[b] = jnp.where(rows >= row_lo, la_all[b], 0.0)

    for j0 in range(0, nc, GDN_GROUP):
        _gdn_chunks(list(range(j0, min(nc, j0 + GDN_GROUP))), q_all, k_all, v_all, beta_all, la_all,
                    p_ref, mix_ref, ng, s_ref, c=c, nb=nb)


def _gdn_chunks(chunk_ids, q_all, k_all, v_all, beta_all, la_all, p_ref, mix_ref, ng, s_ref, *, c, nb):
    ti = lax.broadcasted_iota(jnp.int32, (c, c), 0)
    si = lax.broadcasted_iota(jnp.int32, (c, c), 1)
    tri = (si <= ti).astype(F32).astype(BF16)
    eye_f = (ti == si).astype(F32)
    incl = si <= ti
    chains = [(b, j, h) for b in range(nb) for j in chunk_ids for h in range(N_HEADS)]

    gc = {(b, j): _cumsum_rows(tri, la_all[b][j * c:(j + 1) * c]) for b in range(nb) for j in chunk_ids}

    pre = {}
    for ch in chains:
        b, j, h = ch
        rs = slice(j * c, (j + 1) * c)
        col = slice(h * HEAD_DIM, (h + 1) * HEAD_DIM)
        qh, kh, vh = q_all[b][rs, col], k_all[b][rs, col], v_all[b][rs, col]
        qh = qh * lax.rsqrt(jnp.sum(qh * qh, axis=-1, keepdims=True) + RMS_EPS) * (HEAD_DIM ** -0.5)
        kh = kh * lax.rsqrt(jnp.sum(kh * kh, axis=-1, keepdims=True) + RMS_EPS)
        beta = beta_all[b][rs, h:h + 1]
        gcol = gc[b, j][:, N_HEADS + h:N_HEADS + h + 1]
        grow = jnp.sum(eye_f * gcol, axis=0, keepdims=True)
        decay = jnp.where(incl, jnp.exp(jnp.where(incl, gcol - grow, 0.0)), 0.0)
        gamma = jnp.exp(gcol)
        g_last = gcol[c - 1:c, :]
        pre[ch] = dict(
            q=qh, k=kh, beta=beta, decay=decay,
            rhs=jnp.concatenate([(beta * gamma) * kh, beta * vh], axis=1).astype(BF16),
            gq=(gamma * qh).astype(BF16),
            kd=(kh * jnp.exp(g_last - gcol)).astype(BF16),
            eg=jnp.exp(g_last))
    k_b = {ch: pre[ch]["k"].astype(BF16) for ch in chains}
    kk = _multi_dot(k_b, k_b, chains, nt=True)
    qk = _multi_dot({ch: pre[ch]["q"] for ch in chains}, k_b, chains, nt=True)
    a_mat = {ch: (pre[ch]["beta"] * kk[ch] * pre[ch]["decay"]).astype(BF16) for ch in chains}
    aqk = {ch: (qk[ch] * pre[ch]["decay"]).astype(BF16) for ch in chains}

    xr = ti ^ si
    first = (xr < 2) & (si < ti)
    zero_b = jnp.zeros((c, c), BF16)
    t_inv = {ch: (eye_f - jnp.where(first, a_mat[ch], zero_b).astype(F32)).astype(BF16) for ch in chains}
    n = 2
    while n < c:
        lower = (xr < 2 * n) & ((ti & n) != 0) & ((si & n) == 0)
        x = _multi_dot({ch: jnp.where(lower, a_mat[ch], zero_b) for ch in chains}, t_inv, chains)
        tx = _multi_dot(t_inv, x, chains)
        t_inv = {ch: (t_inv[ch].astype(F32) - tx[ch]).astype(BF16) for ch in chains}
        n *= 2
    wu = {ch: _dot(t_inv[ch], pre[ch]["rhs"]) for ch in chains}

    for j in chunk_ids:
        rs = slice(j * c, (j + 1) * c)
        group = [(b, j, h) for b in range(nb) for h in range(N_HEADS)]
        s_old = {ch: s_ref[ch[0], ch[2]] for ch in group}
        ws = _multi_dot({ch: jnp.concatenate([wu[ch][:, :HEAD_DIM].astype(BF16), pre[ch]["gq"]], axis=0)
                         for ch in group}, s_old, group)
        u = {ch: (wu[ch][:, HEAD_DIM:] - ws[ch][:c]).astype(BF16) for ch in group}
        au = _multi_dot(aqk, u, group)
        for ch in group:
            b, _, h = ch
            s_ref[b, h] = pre[ch]["eg"] * s_old[ch] + _dot_tn(pre[ch]["kd"], u[ch])
            o = ws[ch][c:] + au[ch]
            on = o * lax.rsqrt(jnp.mean(o * o, axis=-1, keepdims=True) + RMS_EPS) * ng
            gate = p_ref[b, rs, GD_COL + 3 * MIX_HALF + h * HEAD_DIM:GD_COL + 3 * MIX_HALF + (h + 1) * HEAD_DIM]
            mix_ref[b, rs, MIX_HALF + h * HEAD_DIM:MIX_HALF + (h + 1) * HEAD_DIM] = on * _silu(gate)


def _mixer_kernel(*refs, c, nc, nb, fuse_proj, has_state, row_lo, n_steps):
    refs = list(refs)
    if fuse_proj:
        x_ref, sx_ref, w_ref, wt_ref = refs[:4]
        del refs[:4]
    else:
        p_ref, t_ref = refs[:2]
        del refs[:2]
    lbl_ref, hng_ref, cw_ref, par_ref, gng_ref = refs[:5]
    del refs[:5]
    if has_state:
        hist_ref, shg0_ref, sgd0_ref = refs[:3]
        del refs[:3]
    else:
        hist_ref = shg0_ref = sgd0_ref = None
    if fuse_proj:
        (mix_ref, shg_ref, sgd_ref, cb_ref, sp_ref, stl_ref, st_ref, s_ref, gc_ref, xp_ref, p_ref, t_ref,
         wb_ref) = refs
    else:
        mix_ref, shg_ref, sgd_ref, cb_ref, st_ref, s_ref, gc_ref, xp_ref = refs
    step = pl.program_id(1)
    pairs = [(b, h) for b in range(nb) for h in range(N_HEADS)]

    @pl.when(step == 0)
    def _():
        for b, h in pairs:
            if has_state:
                st_ref[b, h] = shg0_ref[b, h].T
                s_ref[b, h] = sgd0_ref[b, h]
            else:
                st_ref[b, h] = jnp.zeros((HEAD_DIM, HEAD_DIM), F32)
                s_ref[b, h] = jnp.zeros((HEAD_DIM, HEAD_DIM), F32)
        for b in range(nb):
            xp_ref[b, 0:SUBLANES, :] = jnp.zeros((SUBLANES, 3 * MIX_HALF), F32)

    if fuse_proj:
        @pl.when((pl.program_id(0) == 0) & (step == 0))
        def _():
            wb_ref[...] = w_ref[...].astype(BF16)

        rb = x_ref.shape[1]
        xb = jnp.concatenate([x_ref[0], sx_ref[...]], axis=0).astype(BF16)
        proj = jnp.dot(xb, wb_ref[...], preferred_element_type=F32)
        tail = jnp.dot(xb, wt_ref[...], preferred_element_type=F32)
        p_ref[0] = proj[:rb]
        t_ref[0] = tail[:rb]
        sp_ref[...] = proj[rb:]
        stl_ref[...] = tail[rb:]

    lbl = lbl_ref[...]
    e = jnp.exp(lbl - jnp.max(lbl, axis=0, keepdims=True))
    lb = e[0:1] / jnp.sum(e, axis=0, keepdims=True)
    neg_a = -jnp.exp(par_ref[0:1, :])
    dt_bias = par_ref[1:2, :]

    for j0 in range(0, nc, HGRN_GROUP):
        _hgrn_rows(p_ref, mix_ref, lb, hng_ref[...], st_ref, gc_ref,
                   chunk_ids=list(range(j0, min(nc, j0 + HGRN_GROUP))), c=c, nb=nb, row_lo=row_lo)
    _gdn_rows(p_ref, t_ref, hist_ref, mix_ref, cb_ref, cw_ref, neg_a, dt_bias, gng_ref[...], s_ref, xp_ref,
              c=c, nc=nc, nb=nb, row_lo=row_lo)

    @pl.when(step == n_steps - 1)
    def _():
        for b, h in pairs:
            shg_ref[b, h] = st_ref[b, h].T
            sgd_ref[b, h] = s_ref[b, h]


def _mixer(x3, proj, tail, weights, hist, s_hg, s_gd, *, c, nc, nb, row_lo, side_x=None):
    lb_logits, w_main, w_tail, conv_w, a_log, dt_bias, hg_norm_g, gd_norm_g = weights
    fuse_proj = proj is None
    bsz, t = (x3 if fuse_proj else proj).shape[:2]
    rb = c * nc
    n_steps = t // rb
    has_state = s_hg is not None
    n_hist = GD_CONV - 1
    const = lambda shape: pl.BlockSpec(shape, lambda i, si: (0,) * len(shape))
    rows3 = lambda width: pl.BlockSpec((nb, rb, width), lambda i, si: (i, si, 0))
    st_spec = pl.BlockSpec((nb, N_HEADS, HEAD_DIM, HEAD_DIM), lambda i, si: (i, 0, 0, 0))
    par = jnp.zeros((SUBLANES, HEAD_DIM), F32)
    par = par.at[0, N_HEADS:2 * N_HEADS].set(a_log).at[1, N_HEADS:2 * N_HEADS].set(dt_bias)
    if fuse_proj:
        assert nb == 1 and side_x.shape[0] % (bsz * n_steps * SUBLANES) == 0
        ns = side_x.shape[0] // (bsz * n_steps)
        side = lambda width: pl.BlockSpec((ns, width), lambda i, si: (i * n_steps + si, 0))
        in_specs = [rows3(D_MODEL), side(D_MODEL),
                    pl.BlockSpec((D_MODEL, PROJ_COLS), lambda i, si: (0, 0), pipeline_mode=pl.Buffered(1)),
                    const(w_tail.shape)]
        args = [x3, side_x, w_main, w_tail]
    else:
        in_specs = [rows3(PROJ_COLS), rows3(HEAD_DIM)]
        args = [proj, tail]
    in_specs += [const(lb_logits.shape), const((1, HEAD_DIM)), const(conv_w.shape), const(par.shape),
                 const((1, HEAD_DIM))]
    args += [lb_logits, hg_norm_g.reshape(1, HEAD_DIM), conv_w, par, gd_norm_g.reshape(1, HEAD_DIM)]
    if has_state:
        in_specs += [rows3(3 * MIX_HALF), st_spec, st_spec]
        args += [hist, s_hg, s_gd]
    scratch = [pltpu.VMEM((nb, N_HEADS, HEAD_DIM, HEAD_DIM), F32),
               pltpu.VMEM((nb, N_HEADS, HEAD_DIM, HEAD_DIM), F32),
               pltpu.VMEM((nb, rb, MIX_HALF), F32),
               pltpu.VMEM((nb, SUBLANES + rb, 3 * MIX_HALF), F32)]
    if fuse_proj:
        scratch += [pltpu.VMEM((nb, rb, PROJ_COLS), F32), pltpu.VMEM((nb, rb, HEAD_DIM), F32),
                    pltpu.VMEM((D_MODEL, PROJ_COLS), BF16)]
    out_specs = [rows3(D_MODEL), st_spec, st_spec,
                 pl.BlockSpec((nb, n_hist, 3 * MIX_HALF), lambda i, si: (i, 0, 0))]
    out_shape = [jax.ShapeDtypeStruct((bsz, t, D_MODEL), F32),
                 jax.ShapeDtypeStruct((bsz, N_HEADS, HEAD_DIM, HEAD_DIM), F32),
                 jax.ShapeDtypeStruct((bsz, N_HEADS, HEAD_DIM, HEAD_DIM), F32),
                 jax.ShapeDtypeStruct((bsz, n_hist, 3 * MIX_HALF), F32)]
    if fuse_proj:
        out_specs += [side(PROJ_COLS), side(HEAD_DIM)]
        out_shape += [jax.ShapeDtypeStruct((side_x.shape[0], PROJ_COLS), F32),
                      jax.ShapeDtypeStruct((side_x.shape[0], HEAD_DIM), F32)]
    return pl.pallas_call(
        functools.partial(_mixer_kernel, c=c, nc=nc, nb=nb, fuse_proj=fuse_proj, has_state=has_state,
                          row_lo=row_lo, n_steps=n_steps),
        grid=(bsz // nb, n_steps),
        in_specs=in_specs,
        out_specs=out_specs,
        out_shape=out_shape,
        scratch_shapes=scratch,
        compiler_params=_params(2),
        name="mixer",
    )(*args)


def _attn_block_kernel(mix_ref, x_ref, k_ref, v_ref, wo_ref, wq_ref, wm_ref, g1_ref, b1_ref, g2_ref, b2_ref,
                       smix_ref, sx_ref, sk_ref, sv_ref, o_ref, so_ref, *, n_sub):
    def layer_norm(acc, g_ref, b_ref):
        mu = jnp.mean(acc, axis=-1, keepdims=True)
        xc = acc - mu
        var = jnp.mean(xc * xc, axis=-1, keepdims=True)
        return xc * lax.rsqrt(var + LN_EPS) * g_ref[...] + b_ref[...]

    nb, st, d = sx_ref.shape
    ts = x_ref.shape[1] // n_sub
    subs = [pl.ds(i * ts, ts) for i in range(n_sub)]
    scale = MEM_HEAD_DIM ** -0.5
    cols = [slice(h * MEM_HEAD_DIM, (h + 1) * MEM_HEAD_DIM) for h in range(MEM_HEADS)]
    heads = range(MEM_HEADS)
    kh = [_head_block(k_ref, 0, h).astype(BF16) for h in heads]
    vh = [_head_block(v_ref, 0, h).astype(BF16) for h in heads]

    x_rows = [x_ref[0, r, :] for r in subs]
    mix_rows = [mix_ref[0, r, :] for r in subs]
    x_rows[0] = jnp.concatenate([x_rows[0], sx_ref[...].reshape(nb * st, d)], axis=0)
    mix_rows[0] = jnp.concatenate([mix_rows[0], smix_ref[...].reshape(nb * st, d)], axis=0)

    h1 = [layer_norm(ALPHA * xr + jnp.dot(mr.astype(BF16), wo_ref[...], preferred_element_type=F32),
                     g1_ref, b1_ref) for xr, mr in zip(x_rows, mix_rows)]
    q = [jnp.dot(hi.astype(BF16), wq_ref[...], preferred_element_type=F32).astype(BF16) for hi in h1]
    s_main = [[_dot_nt(qi[:ts, cols[h]], kh[h]) * scale for h in heads] for qi in q]
    pairs = [(b, h) for b in range(nb) for h in heads]
    s_side = {(b, h): _dot_nt(q[0][ts + b * st:ts + (b + 1) * st, cols[h]], _head_block(sk_ref, b, h)) * scale
              for b, h in pairs}
    p_main = [[_softmax_rows(sh) for sh in si] for si in s_main]
    p_side = {bh: _softmax_rows(s_side[bh]) for bh in pairs}
    att = [jnp.concatenate([_dot(pi[h], vh[h]).astype(BF16) for h in heads], axis=1) for pi in p_main]
    att_side = [jnp.concatenate([_dot(p_side[b, h], _head_block(sv_ref, b, h)).astype(BF16) for h in heads],
                                axis=1) for b in range(nb)]
    att[0] = jnp.concatenate([att[0]] + att_side, axis=0)
    h2 = [layer_norm(ALPHA * hi + jnp.dot(ai, wm_ref[...], preferred_element_type=F32), g2_ref, b2_ref)
          for hi, ai in zip(h1, att)]
    for r, hi in zip(subs, h2):
        o_ref[0, r, :] = hi[:ts]
    so_ref[...] = h2[0][ts:].reshape(nb, st, d)


def _attn_block(mix3, x3, mk, mv, w_out, w_mq, w_mo, ln1_g, ln1_b, ln2_g, ln2_b, side_mix, side_x, side_k,
                side_v, *, tm, n_sub):
    bsz, t, d = x3.shape
    tm = min(tm, t)
    n_steps = bsz * (t // tm)
    bs, ts, _ = side_x.shape
    assert bs % n_steps == 0
    nb = bs // n_steps
    flat = lambda i, r: (i * (t // tm) + r, 0, 0)
    side_rows = pl.BlockSpec((nb, ts, d), flat)
    side_kv = pl.BlockSpec((nb,) + side_k.shape[1:], flat)
    rows = pl.BlockSpec((1, tm, d), lambda i, r: (i, r, 0))
    kv = pl.BlockSpec((1,) + mk.shape[1:], lambda i, r: (i, 0, 0))
    wsp = pl.BlockSpec((d, d), lambda i, r: (0, 0))
    vec = pl.BlockSpec((1, d), lambda i, r: (0, 0))
    return pl.pallas_call(
        functools.partial(_attn_block_kernel, n_sub=n_sub),
        grid=(bsz, t // tm),
        in_specs=[rows, rows, kv, kv, wsp, wsp, wsp, vec, vec, vec, vec, side_rows, side_rows, side_kv, side_kv],
        out_specs=[rows, side_rows],
        out_shape=[jax.ShapeDtypeStruct((bsz, t, d), F32), jax.ShapeDtypeStruct((bs, ts, d), F32)],
        compiler_params=_params(2),
        name="attn_block",
    )(mix3, x3, mk, mv, w_out, w_mq, w_mo, ln1_g.reshape(1, d), ln1_b.reshape(1, d),
      ln2_g.reshape(1, d), ln2_b.reshape(1, d), side_mix, side_x, side_k, side_v)


def _ffn_block_kernel(h_ref, wg_ref, wv_ref, wd_ref, cw_ref, cb_ref, g_ref, b_ref, sh_ref, shist_ref,
                      y_ref, buf_ref, sy_ref, sgate_ref, xp_ref, sxp_ref, *, tm, n_split, grp, row_lo):
    n_hist = FFN_CONV - 1
    f = xp_ref.shape[1]
    ns = sh_ref.shape[0]
    tiles = f // MXU_TILE
    bounds = [MXU_TILE * -(-tiles * j // n_split) for j in range(n_split)] + [f]

    @pl.when(pl.program_id(1) == 0)
    def _():
        xp_ref[0:SUBLANES, :] = jnp.zeros((SUBLANES, f), F32)
        sxp_ref[0:SUBLANES, :] = jnp.zeros((SUBLANES, f), F32)

    h = jnp.concatenate([h_ref[0], sh_ref[...]], axis=0)
    hb = h.astype(BF16)
    cols = [slice(bounds[j], bounds[j + 1]) for j in range(n_split)]
    pos = lax.broadcasted_iota(jnp.int32, (ns, 1), 0) % grp
    is_hist = (pos >= row_lo - n_hist) & (pos < row_lo)

    def conv(ref, rows, gate, c):
        ref[SUBLANES:SUBLANES + rows, c] = gate
        y = gate * cw_ref[n_hist:n_hist + 1, c] + cb_ref[:, c]
        for tap in range(n_hist):
            off = SUBLANES - n_hist + tap
            y = y + ref[off:off + rows, c] * cw_ref[tap:tap + 1, c]
        return y

    gate = [jnp.dot(hb, wg_ref[:, c], preferred_element_type=F32) for c in cols]
    val = [jnp.dot(hb, wv_ref[:, c], preferred_element_type=F32) for c in cols]
    act = []
    for c, gj, vj in zip(cols, gate, val):
        side_gate = jnp.where(is_hist, shist_ref[:, c], gj[tm:])
        sgate_ref[:, c] = side_gate
        y = jnp.concatenate([conv(xp_ref, tm, gj[:tm], c), conv(sxp_ref, ns, side_gate, c)], axis=0)
        gelu = 0.5 * y * (1.0 + lax.erf(y * (2.0 ** -0.5)))
        act.append((gelu * vj).astype(BF16))
    acc = ALPHA * h
    for c, aj in zip(cols, act):
        acc = acc + jnp.dot(aj, wd_ref[c, :], preferred_element_type=F32)
    mu = jnp.mean(acc, axis=-1, keepdims=True)
    xc = acc - mu
    var = jnp.mean(xc * xc, axis=-1, keepdims=True)
    out = xc * lax.rsqrt(var + LN_EPS) * g_ref[...] + b_ref[...]
    y_ref[0] = out[:tm]
    sy_ref[...] = out[tm:]
    buf_ref[0] = xp_ref[SUBLANES + tm - n_hist:SUBLANES + tm, :]
    xp_ref[0:SUBLANES, :] = xp_ref[tm:tm + SUBLANES, :]


def _ffn_block(h3, w_up, w_down, conv_w, conv_b, ln_g, ln_b, side_h, side_hist, *, tm, n_split, grp, row_lo):
    bsz, t, d = h3.shape
    f = w_down.shape[0]
    tm = min(tm, t)
    n_hist = FFN_CONV - 1
    n_steps = bsz * (t // tm)
    rs = side_h.shape[0]
    assert rs % (n_steps * grp) == 0
    ns = rs // n_steps
    once = pl.Buffered(1)
    const = lambda shape: pl.BlockSpec(shape, lambda i, r: (0,) * len(shape), pipeline_mode=once)
    rows = pl.BlockSpec((1, tm, d), lambda i, r: (i, r, 0))
    side = lambda width: pl.BlockSpec((ns, width), lambda i, r: (i * (t // tm) + r, 0))
    return pl.pallas_call(
        functools.partial(_ffn_block_kernel, tm=tm, n_split=n_split, grp=grp, row_lo=row_lo),
        grid=(bsz, t // tm),
        in_specs=[rows, const((d, f)), pl.BlockSpec((d, f), lambda i, r: (0, 1), pipeline_mode=once),
                  const((f, d)), const((FFN_CONV, f)), const((1, f)),
                  const((1, d)), const((1, d)), side(d), side(f)],
        out_specs=[rows, pl.BlockSpec((1, n_hist, f), lambda i, r: (i, 0, 0)), side(d), side(f)],
        out_shape=[jax.ShapeDtypeStruct((bsz, t, d), F32), jax.ShapeDtypeStruct((bsz, n_hist, f), F32),
                   jax.ShapeDtypeStruct((rs, d), F32), jax.ShapeDtypeStruct((rs, f), F32)],
        scratch_shapes=[pltpu.VMEM((SUBLANES + tm, f), F32), pltpu.VMEM((SUBLANES + ns, f), F32)],
        compiler_params=_params(2),
        name="ffn_block",
    )(h3, w_up, w_up, w_down, conv_w, conv_b.reshape(1, f), ln_g.reshape(1, d), ln_b.reshape(1, d),
      side_h, side_hist)


def _cache_rows(cache):
    b, m, h, d = cache.shape
    return cache.reshape(b, m, h, d // LANES, LANES).transpose(0, 1, 3, 2, 4).reshape(b, m * h * d // LANES, LANES)


def _cache_from_rows(rows):
    b = rows.shape[0]
    halves = MEM_HEAD_DIM // LANES
    return (rows.reshape(b, N_MEM, halves, MEM_HEADS, LANES).transpose(0, 1, 3, 2, 4)
            .reshape(b, N_MEM, MEM_HEADS, MEM_HEAD_DIM))


def kernel(x_prompt, x_sample, state_hgrn, state_gdn, state_gdn_conv, state_ffn_conv, cache_mem_k, cache_mem_v, mem_prompt, hgrn_lb_logits, w_in, w_gd_conv, gd_a_log, gd_dt_bias, hg_norm_g, gd_norm_g, w_out, ln1_g, ln1_b, w_mq, w_mkv, w_mo, ln2_g, ln2_b, w_up, w_ffn_conv, b_ffn_conv, w_down, ln3_g, ln3_b):
    bp, tp, d = x_prompt.shape
    bs, ts, _ = x_sample.shape
    tpad = SUBLANES
    row_lo = tpad - ts
    l = 0

    w_in_tail = jnp.pad(w_in[l][:, PROJ_COLS:], ((0, 0), (0, HEAD_DIM - 2 * N_HEADS))).astype(BF16)
    mixer_w = (hgrn_lb_logits, w_in[l], w_in_tail, w_gd_conv[l],
               gd_a_log[l], gd_dt_bias[l], hg_norm_g[l], gd_norm_g[l])
    (w_out_b, w_mq_b, w_mo_b, w_up_b, w_down_b) = (w.astype(BF16) for w in
                                                    (w_out[l], w_mq[l], w_mo[l], w_up[l], w_down[l]))

    n_hg = GD_CONV - 1
    n_hf = FFN_CONV - 1
    xs = jnp.pad(x_sample, ((0, 0), (row_lo, 0), (0, 0)))
    hist_gd = jnp.pad(state_gdn_conv[l], ((0, 0), (row_lo - n_hg, ts), (0, 0)))
    hist_ffn = jnp.pad(state_ffn_conv[l], ((0, 0), (row_lo - n_hf, ts), (0, 0)))

    mk, mv = _kv_proj(mem_prompt, w_mkv[l].astype(BF16))
    mix_p, p_hg, p_gd, p_bgd, proj_s, tail_s = _mixer(x_prompt, None, None, mixer_w, None, None, None,
                                                      c=128, nc=4, nb=1, row_lo=0,
                                                      side_x=xs.reshape(bs * tpad, d))
    mix_s, s_hg, s_gd, s_bgd = _mixer(None, proj_s.reshape(bs, tpad, PROJ_COLS), tail_s.reshape(bs, tpad, HEAD_DIM),
                                      mixer_w, hist_gd, state_hgrn[l], state_gdn[l],
                                      c=tpad, nc=1, nb=8, row_lo=row_lo)
    h2_p, h2_s = _attn_block(mix_p, x_prompt, mk, mv, w_out_b, w_mq_b, w_mo_b, ln1_g[l], ln1_b[l],
                             ln2_g[l], ln2_b[l], mix_s, xs, _cache_rows(cache_mem_k[l]),
                             _cache_rows(cache_mem_v[l]), tm=512, n_sub=2)
    yp, p_bff, ys, gate_s = _ffn_block(h2_p, w_up_b, w_down_b, w_ffn_conv[l], b_ffn_conv[l], ln3_g[l], ln3_b[l],
                                       h2_s.reshape(bs * tpad, d), hist_ffn.reshape(bs * tpad, D_FF),
                                       tm=512, n_split=2, grp=tpad, row_lo=row_lo)
    ys = ys.reshape(bs, tpad, d)[:, row_lo:]
    s_bff = gate_s.reshape(bs, tpad, D_FF)[:, tpad - n_hf:]

    return (yp, ys, p_hg[None], p_gd[None], p_bgd[None], p_bff[None],
            _cache_from_rows(mk)[None], _cache_from_rows(mv)[None],
            s_hg[None], s_gd[None], s_bgd[None], s_bff[None])
```

```python
import functools

import jax
import jax.numpy as jnp
from jax import lax
from jax.experimental import pallas as pl
from jax.experimental.pallas import tpu as pltpu

F32 = jnp.float32
BF16 = jnp.bfloat16

D_MODEL = 1024
HEAD_DIM = 128
N_HEADS = 4
MIX_HALF = N_HEADS * HEAD_DIM
GD_CONV = 4
FFN_CONV = 3
D_FF = 2816
N_MEM = 256
MEM_HEADS = 4
MEM_HEAD_DIM = D_MODEL // MEM_HEADS
LN_EPS = 1e-5
RMS_EPS = 1e-6
LOG2_E = 1.4426950408889634
DEPTH = 1
ALPHA = (2.0 * DEPTH) ** 0.25

SUBLANES = 8
LANES = 128
MXU_TILE = 256
VMEM_LIMIT = 56 * 1024 * 1024

ROW_BLOCK = 512
MIX_CHUNK = 128
ATTN_SUB_BLOCKS = 2
FFN_SPLITS = 2
SAMPLE_SEQS = 8


def _params(n_axes):
    return pltpu.CompilerParams(dimension_semantics=("arbitrary",) * n_axes,
                                vmem_limit_bytes=VMEM_LIMIT)


def _dot(a, b):
    return jnp.dot(a.astype(BF16), b.astype(BF16), preferred_element_type=F32)


def _dot_nt(a, b):
    return lax.dot_general(a.astype(BF16), b.astype(BF16), (((1,), (1,)), ((), ())),
                           preferred_element_type=F32)


def _dot_tn(a, b):
    return lax.dot_general(a.astype(BF16), b.astype(BF16), (((0,), (0,)), ((), ())),
                           preferred_element_type=F32)


def _split3(x):
    x1 = x.astype(BF16)
    r = x - x1.astype(F32)
    x2 = r.astype(BF16)
    x3 = (r - x2.astype(F32)).astype(BF16)
    return x1, x2, x3


def _cumsum_rows(tri_bf16, x):
    x1, x2, x3 = _split3(x)
    return (jnp.dot(tri_bf16, x1, preferred_element_type=F32)
            + jnp.dot(tri_bf16, x2, preferred_element_type=F32)
            + jnp.dot(tri_bf16, x3, preferred_element_type=F32))


def _multi_dot(a, b, keys, nt=False):
    dot = _dot_nt if nt else _dot
    return {k: dot(a[k], b[k]) for k in keys}


def _silu(x):
    return x * jax.nn.sigmoid(x)


def _softmax_rows(s):
    e = jnp.exp(s - jnp.max(s, axis=-1, keepdims=True))
    return e / jnp.sum(e, axis=-1, keepdims=True)


ROWS_PER_MEM = MEM_HEADS * MEM_HEAD_DIM // LANES


def _head_rows(h, j):
    return pl.ds(h + MEM_HEADS * j, N_MEM, stride=ROWS_PER_MEM)


def _head_block(ref, b, h):
    return jnp.concatenate([ref[b, _head_rows(h, j), :] for j in range(MEM_HEAD_DIM // LANES)], axis=1)


def _kv_proj_kernel(x_ref, wk_ref, wv_ref, k_ref, v_ref):
    xb = x_ref[0].astype(BF16)
    for w_ref, o_ref in ((wk_ref, k_ref), (wv_ref, v_ref)):
        res = jnp.dot(xb, w_ref[...], preferred_element_type=F32)
        for h in range(MEM_HEADS):
            for j in range(MEM_HEAD_DIM // LANES):
                c0 = h * MEM_HEAD_DIM + j * LANES
                o_ref[0, _head_rows(h, j), :] = res[:, c0:c0 + LANES]


def _kv_proj(mem3, w_kv):
    bsz, n_mem, d = mem3.shape
    n = w_kv.shape[1] // 2
    out_spec = pl.BlockSpec((1, n_mem * ROWS_PER_MEM, LANES), lambda i: (i, 0, 0))
    out_shape = jax.ShapeDtypeStruct((bsz, n_mem * ROWS_PER_MEM, LANES), F32)
    return pl.pallas_call(
        _kv_proj_kernel,
        grid=(bsz,),
        in_specs=[pl.BlockSpec((1, n_mem, d), lambda i: (i, 0, 0)),
                  pl.BlockSpec((d, n), lambda i: (0, 0)), pl.BlockSpec((d, n), lambda i: (0, 1))],
        out_specs=[out_spec, out_spec],
        out_shape=[out_shape, out_shape],
        compiler_params=_params(1),
        name="kv_proj",
    )(mem3, w_kv, w_kv)


PROJ_COLS = 8 * MIX_HALF
HG_COL = 0
GD_COL = 4 * MIX_HALF
HGRN_GROUP = 1
GDN_GROUP = 4


def _ref_rows(gc_ref, b, r0, col, m, c):
    blk = 2 * m
    if blk >= SUBLANES:
        parts = [jnp.broadcast_to(gc_ref[b, pl.ds(r0 + j * blk + m - 1, 1), col], (blk, HEAD_DIM))
                 for j in range(c // blk)]
    else:
        sub = lax.broadcasted_iota(jnp.int32, (SUBLANES, HEAD_DIM), 0)
        parts = []
        for i in range(c // SUBLANES):
            tile = None
            for j in range(SUBLANES // blk):
                row = jnp.broadcast_to(gc_ref[b, pl.ds(r0 + i * SUBLANES + j * blk + m - 1, 1), col],
                                       (SUBLANES, HEAD_DIM))
                tile = row if tile is None else jnp.where(sub >= j * blk, row, tile)
            parts.append(tile)
    return parts[0] if len(parts) == 1 else jnp.concatenate(parts, axis=0)


def _hgrn_rows(p_ref, mix_ref, lb, ng, st_ref, gc_ref, *, chunk_ids, c, nb, row_lo):
    rows = lax.broadcasted_iota(jnp.int32, (c, 1), 0)
    ti = lax.broadcasted_iota(jnp.int32, (c, c), 0)
    si = lax.broadcasted_iota(jnp.int32, (c, c), 1)
    tri = (si <= ti).astype(F32).astype(BF16)
    eye = ti == si
    xr = ti ^ si
    cols = [slice(h * HEAD_DIM, (h + 1) * HEAD_DIM) for h in range(N_HEADS)]
    chains = [(b, j, h) for b in range(nb) for j in chunk_ids for h in range(N_HEADS)]

    def proj(b, j, part):
        return p_ref[b, pl.ds(j * c, c), HG_COL + part * MIX_HALF:HG_COL + (part + 1) * MIX_HALF]

    q, k, gc = {}, {}, {}
    for b in range(nb):
        for j in chunk_ids:
            f = lb + (1.0 - lb) * jax.nn.sigmoid(proj(b, j, 1))
            g = jnp.log(f)
            kb = 1.0 - f
            if row_lo:
                g = jnp.where(rows >= row_lo, g, 0.0)
                kb = jnp.where(rows >= row_lo, kb, 0.0)
            qb = _silu(proj(b, j, 0))
            gcb = _cumsum_rows(tri, g)
            gc_ref[b, pl.ds(j * c, c), :] = gcb
            for h in range(N_HEADS):
                q[b, j, h], k[b, j, h], gc[b, j, h] = qb[:, cols[h]], kb[:, cols[h]], gcb[:, cols[h]]

    level_of = jnp.where(eye, 0, -1)
    m = 1
    while m < c:
        level_of = jnp.where((xr >= m) & (xr < 2 * m) & (si < ti), m, level_of)
        m *= 2
    rows_w = lax.broadcasted_iota(jnp.int32, (c, HEAD_DIM), 0)
    q_b = {ch: q[ch].astype(BF16) for ch in chains}
    k_b = {ch: k[ch].astype(BF16) for ch in chains}
    sc = _multi_dot(q_b, k_b, chains, nt=True)
    on_diag = level_of == 0
    sc = {ch: jnp.where(on_diag, sc[ch], 0.0) for ch in chains}
    m = c // 2
    while m >= 1:
        sign = jnp.where((rows_w & m) != 0, LOG2_E, -LOG2_E)
        qw, kw = {}, {}
        for ch in chains:
            b, j, h = ch
            w = jnp.exp2((gc[ch] - _ref_rows(gc_ref, b, j * c, cols[h], m, c)) * sign).astype(BF16)
            qw[ch], kw[ch] = q_b[ch] * w, k_b[ch] * w
        prod = _multi_dot(qw, kw, chains, nt=True)
        at_level = level_of == m
        sc = {ch: jnp.where(at_level, prod[ch], sc[ch]) for ch in chains}
        m //= 2

    v = {(b, j, h): p_ref[b, pl.ds(j * c, c),
                          HG_COL + 2 * MIX_HALF + h * HEAD_DIM:HG_COL + 2 * MIX_HALF + (h + 1) * HEAD_DIM]
         for b, j, h in chains}
    o_intra = _multi_dot(sc, v, chains)

    for j in chunk_ids:
        group = [(b, j, h) for b in range(nb) for h in range(N_HEADS)]
        st = {ch: st_ref[ch[0], ch[2]] for ch in group}
        o_inter = _multi_dot({ch: q[ch] * jnp.exp(gc[ch]) for ch in group}, st, group, nt=True)
        for ch in group:
            b, _, h = ch
            g_last = gc_ref[b, pl.ds(j * c + c - 1, 1), cols[h]]
            st_ref[b, h] = st[ch] * jnp.exp(g_last) + _dot_tn(v[ch], k[ch] * jnp.exp(g_last - gc[ch]))
        for ch in group:
            b, _, h = ch
            ob = o_intra[ch] + o_inter[ch]
            on = ob * lax.rsqrt(jnp.mean(ob * ob, axis=-1, keepdims=True) + RMS_EPS) * ng
            gate = p_ref[b, pl.ds(j * c, c),
                         HG_COL + 3 * MIX_HALF + h * HEAD_DIM:HG_COL + 3 * MIX_HALF + (h + 1) * HEAD_DIM]
            mix_ref[b, pl.ds(j * c, c), cols[h]] = on * _silu(gate)


def _gdn_rows(p_ref, t_ref, hist_ref, mix_ref, cb_ref, cw_ref, neg_a, dt_bias, ng, s_ref, xp_ref,
              *, c, nc, nb, row_lo):
    n_hist = GD_CONV - 1
    rb = c * nc
    rows = lax.broadcasted_iota(jnp.int32, (rb, 1), 0) % c
    ti = lax.broadcasted_iota(jnp.int32, (c, c), 0)
    si = lax.broadcasted_iota(jnp.int32, (c, c), 1)
    tri = (si <= ti).astype(F32).astype(BF16)
    eye_f = (ti == si).astype(F32)
    incl = si <= ti

    q_all, k_all, v_all, beta_all, la_all = {}, {}, {}, {}, {}
    for b in range(nb):
        qkv = []
        for j in range(3):
            col = slice(j * MIX_HALF, (j + 1) * MIX_HALF)
            x = p_ref[b, :, GD_COL + j * MIX_HALF:GD_COL + (j + 1) * MIX_HALF]
            if hist_ref is not None:
                x = jnp.where((rows >= row_lo - n_hist) & (rows < row_lo), hist_ref[b, :, col], x)
            xp_ref[b, SUBLANES:SUBLANES + rb, col] = x
            y = x * cw_ref[n_hist:n_hist + 1, col]
            for tap in range(n_hist):
                off = SUBLANES - n_hist + tap
                y = y + xp_ref[b, off:off + rb, col] * cw_ref[tap:tap + 1, col]
            qkv.append(_silu(y))
        cb_ref[b] = xp_ref[b, SUBLANES + rb - n_hist:SUBLANES + rb, :]
        xp_ref[b, 0:SUBLANES, :] = xp_ref[b, rb:rb + SUBLANES, :]
        q_all[b], k_all[b], v_all[b] = qkv

        tail = t_ref[b]
        beta_all[b] = jax.nn.sigmoid(tail)
        la_all[b] = neg_a * jax.nn.softplus(tail + dt_bias)
        if row_lo:
            beta_all[b] = jnp.where(rows >= row_lo, beta_all[b], 0.0)
            la_all[b] = jnp.where(rows >= row_lo, la_all[b], 0.0)

    for j0 in range(0, nc, GDN_GROUP):
        _gdn_chunks(list(range(j0, min(nc, j0 + GDN_GROUP))), q_all, k_all, v_all, beta_all, la_all,
                    p_ref, mix_ref, ng, s_ref, c=c, nb=nb)


def _gdn_chunks(chunk_ids, q_all, k_all, v_all, beta_all, la_all, p_ref, mix_ref, ng, s_ref, *, c, nb):
    ti = lax.broadcasted_iota(jnp.int32, (c, c), 0)
    si = lax.broadcasted_iota(jnp.int32, (c, c), 1)
    tri = (si <= ti).astype(F32).astype(BF16)
    eye_f = (ti == si).astype(F32)
    incl = si <= ti
    chains = [(b, j, h) for b in range(nb) for j in chunk_ids for h in range(N_HEADS)]

    gc = {(b, j): _cumsum_rows(tri, la_all[b][j * c:(j + 1) * c]) for b in range(nb) for j in chunk_ids}

    pre = {}
    for ch in chains:
        b, j, h = ch
        rs = slice(j * c, (j + 1) * c)
        col = slice(h * HEAD_DIM, (h + 1) * HEAD_DIM)
        qh, kh, vh = q_all[b][rs, col], k_all[b][rs, col], v_all[b][rs, col]
        qh = qh * lax.rsqrt(jnp.sum(qh * qh, axis=-1, keepdims=True) + RMS_EPS) * (HEAD_DIM ** -0.5)
        kh = kh * lax.rsqrt(jnp.sum(kh * kh, axis=-1, keepdims=True) + RMS_EPS)
        beta = beta_all[b][rs, h:h + 1]
        gcol = gc[b, j][:, N_HEADS + h:N_HEADS + h + 1]
        grow = jnp.sum(eye_f * gcol, axis=0, keepdims=True)
        decay = jnp.where(incl, jnp.exp(jnp.where(incl, gcol - grow, 0.0)), 0.0)
        gamma = jnp.exp(gcol)
        g_last = gcol[c - 1:c, :]
        pre[ch] = dict(
            q=qh, k=kh, beta=beta, decay=decay,
            rhs=jnp.concatenate([(beta * gamma) * kh, beta * vh], axis=1).astype(BF16),
            gq=(gamma * qh).astype(BF16),
            kd=(kh * jnp.exp(g_last - gcol)).astype(BF16),
            eg=jnp.exp(g_last))
    k_b = {ch: pre[ch]["k"].astype(BF16) for ch in chains}
    kk = _multi_dot(k_b, k_b, chains, nt=True)
    qk = _multi_dot({ch: pre[ch]["q"] for ch in chains}, k_b, chains, nt=True)
    a_mat = {ch: (pre[ch]["beta"] * kk[ch] * pre[ch]["decay"]).astype(BF16) for ch in chains}
    aqk = {ch: (qk[ch] * pre[ch]["decay"]).astype(BF16) for ch in chains}

    xr = ti ^ si
    first = (xr < 2) & (si < ti)
    zero_b = jnp.zeros((c, c), BF16)
    t_inv = {ch: (eye_f - jnp.where(first, a_mat[ch], zero_b).astype(F32)).astype(BF16) for ch in chains}
    n = 2
    while n < c:
        lower = (xr < 2 * n) & ((ti & n) != 0) & ((si & n) == 0)
        x = _multi_dot({ch: jnp.where(lower, a_mat[ch], zero_b) for ch in chains}, t_inv, chains)
        tx = _multi_dot(t_inv, x, chains)
        t_inv = {ch: (t_inv[ch].astype(F32) - tx[ch]).astype(BF16) for ch in chains}
        n *= 2
    wu = {ch: _dot(t_inv[ch], pre[ch]["rhs"]) for ch in chains}

    for j in chunk_ids:
        rs = slice(j * c, (j + 1) * c)
        group = [(b, j, h) for b in range(nb) for h in range(N_HEADS)]
        s_old = {ch: s_ref[ch[0], ch[2]] for ch in group}
        ws = _multi_dot({ch: jnp.concatenate([wu[ch][:, :HEAD_DIM].astype(BF16), pre[ch]["gq"]], axis=0)
                         for ch in group}, s_old, group)
        u = {ch: (wu[ch][:, HEAD_DIM:] - ws[ch][:c]).astype(BF16) for ch in group}
        au = _multi_dot(aqk, u, group)
        for ch in group:
            b, _, h = ch
            s_ref[b, h] = pre[ch]["eg"] * s_old[ch] + _dot_tn(pre[ch]["kd"], u[ch])
            o = ws[ch][c:] + au[ch]
            on = o * lax.rsqrt(jnp.mean(o * o, axis=-1, keepdims=True) + RMS_EPS) * ng
            gate = p_ref[b, rs, GD_COL + 3 * MIX_HALF + h * HEAD_DIM:GD_COL + 3 * MIX_HALF + (h + 1) * HEAD_DIM]
            mix_ref[b, rs, MIX_HALF + h * HEAD_DIM:MIX_HALF + (h + 1) * HEAD_DIM] = on * _silu(gate)


def _mixer_kernel(*refs, c, nc, nb, fuse_proj, has_state, row_lo, n_steps):
    refs = list(refs)
    if fuse_proj:
        x_ref, sx_ref, w_ref, wt_ref = refs[:4]
        del refs[:4]
    else:
        p_ref, t_ref = refs[:2]
        del refs[:2]
    lbl_ref, hng_ref, cw_ref, par_ref, gng_ref = refs[:5]
    del refs[:5]
    if has_state:
        hist_ref, shg0_ref, sgd0_ref = refs[:3]
        del refs[:3]
    else:
        hist_ref = shg0_ref = sgd0_ref = None
    if fuse_proj:
        mix_ref, shg_ref, sgd_ref, cb_ref, sp_ref, stl_ref, st_ref, s_ref, gc_ref, xp_ref, p_ref, t_ref = refs
    else:
        mix_ref, shg_ref, sgd_ref, cb_ref, st_ref, s_ref, gc_ref, xp_ref = refs
    step = pl.program_id(1)
    pairs = [(b, h) for b in range(nb) for h in range(N_HEADS)]

    @pl.when(step == 0)
    def _():
        for b, h in pairs:
            if has_state:
                st_ref[b, h] = shg0_ref[b, h].T
                s_ref[b, h] = sgd0_ref[b, h]
            else:
                st_ref[b, h] = jnp.zeros((HEAD_DIM, HEAD_DIM), F32)
                s_ref[b, h] = jnp.zeros((HEAD_DIM, HEAD_DIM), F32)
        for b in range(nb):
            xp_ref[b, 0:SUBLANES, :] = jnp.zeros((SUBLANES, 3 * MIX_HALF), F32)

    if fuse_proj:
        rb = x_ref.shape[1]
        xb = jnp.concatenate([x_ref[0], sx_ref[...]], axis=0).astype(BF16)
        proj = jnp.dot(xb, w_ref[...], preferred_element_type=F32)
        tail = jnp.dot(xb, wt_ref[...], preferred_element_type=F32)
        p_ref[0] = proj[:rb]
        t_ref[0] = tail[:rb]
        sp_ref[...] = proj[rb:]
        stl_ref[...] = tail[rb:]

    lbl = lbl_ref[...]
    e = jnp.exp(lbl - jnp.max(lbl, axis=0, keepdims=True))
    lb = e[0:1] / jnp.sum(e, axis=0, keepdims=True)
    neg_a = -jnp.exp(par_ref[0:1, :])
    dt_bias = par_ref[1:2, :]

    for j0 in range(0, nc, HGRN_GROUP):
        _hgrn_rows(p_ref, mix_ref, lb, hng_ref[...], st_ref, gc_ref,
                   chunk_ids=list(range(j0, min(nc, j0 + HGRN_GROUP))), c=c, nb=nb, row_lo=row_lo)
    _gdn_rows(p_ref, t_ref, hist_ref, mix_ref, cb_ref, cw_ref, neg_a, dt_bias, gng_ref[...], s_ref, xp_ref,
              c=c, nc=nc, nb=nb, row_lo=row_lo)

    @pl.when(step == n_steps - 1)
    def _():
        for b, h in pairs:
            shg_ref[b, h] = st_ref[b, h].T
            sgd_ref[b, h] = s_ref[b, h]


def _mixer(x3, proj, tail, weights, hist, s_hg, s_gd, *, c, nc, nb, row_lo, side_x=None):
    lb_logits, w_main, w_tail, conv_w, a_log, dt_bias, hg_norm_g, gd_norm_g = weights
    fuse_proj = proj is None
    bsz, t = (x3 if fuse_proj else proj).shape[:2]
    rb = c * nc
    n_steps = t // rb
    has_state = s_hg is not None
    n_hist = GD_CONV - 1
    const = lambda shape: pl.BlockSpec(shape, lambda i, si: (0,) * len(shape))
    rows3 = lambda width: pl.BlockSpec((nb, rb, width), lambda i, si: (i, si, 0))
    st_spec = pl.BlockSpec((nb, N_HEADS, HEAD_DIM, HEAD_DIM), lambda i, si: (i, 0, 0, 0))
    par = jnp.zeros((SUBLANES, HEAD_DIM), F32)
    par = par.at[0, N_HEADS:2 * N_HEADS].set(a_log).at[1, N_HEADS:2 * N_HEADS].set(dt_bias)
    if fuse_proj:
        assert nb == 1 and side_x.shape[0] % (bsz * n_steps * SUBLANES) == 0
        ns = side_x.shape[0] // (bsz * n_steps)
        side = lambda width: pl.BlockSpec((ns, width), lambda i, si: (i * n_steps + si, 0))
        in_specs = [rows3(D_MODEL), side(D_MODEL), const((D_MODEL, PROJ_COLS)), const(w_tail.shape)]
        args = [x3, side_x, w_main, w_tail]
    else:
        in_specs = [rows3(PROJ_COLS), rows3(HEAD_DIM)]
        args = [proj, tail]
    in_specs += [const(lb_logits.shape), const((1, HEAD_DIM)), const(conv_w.shape), const(par.shape),
                 const((1, HEAD_DIM))]
    args += [lb_logits, hg_norm_g.reshape(1, HEAD_DIM), conv_w, par, gd_norm_g.reshape(1, HEAD_DIM)]
    if has_state:
        in_specs += [rows3(3 * MIX_HALF), st_spec, st_spec]
        args += [hist, s_hg, s_gd]
    scratch = [pltpu.VMEM((nb, N_HEADS, HEAD_DIM, HEAD_DIM), F32),
               pltpu.VMEM((nb, N_HEADS, HEAD_DIM, HEAD_DIM), F32),
               pltpu.VMEM((nb, rb, MIX_HALF), F32),
               pltpu.VMEM((nb, SUBLANES + rb, 3 * MIX_HALF), F32)]
    if fuse_proj:
        scratch += [pltpu.VMEM((nb, rb, PROJ_COLS), F32), pltpu.VMEM((nb, rb, HEAD_DIM), F32)]
    out_specs = [rows3(D_MODEL), st_spec, st_spec,
                 pl.BlockSpec((nb, n_hist, 3 * MIX_HALF), lambda i, si: (i, 0, 0))]
    out_shape = [jax.ShapeDtypeStruct((bsz, t, D_MODEL), F32),
                 jax.ShapeDtypeStruct((bsz, N_HEADS, HEAD_DIM, HEAD_DIM), F32),
                 jax.ShapeDtypeStruct((bsz, N_HEADS, HEAD_DIM, HEAD_DIM), F32),
                 jax.ShapeDtypeStruct((bsz, n_hist, 3 * MIX_HALF), F32)]
    if fuse_proj:
        out_specs += [side(PROJ_COLS), side(HEAD_DIM)]
        out_shape += [jax.ShapeDtypeStruct((side_x.shape[0], PROJ_COLS), F32),
                      jax.ShapeDtypeStruct((side_x.shape[0], HEAD_DIM), F32)]
    return pl.pallas_call(
        functools.partial(_mixer_kernel, c=c, nc=nc, nb=nb, fuse_proj=fuse_proj, has_state=has_state,
                          row_lo=row_lo, n_steps=n_steps),
        grid=(bsz // nb, n_steps),
        in_specs=in_specs,
        out_specs=out_specs,
        out_shape=out_shape,
        scratch_shapes=scratch,
        compiler_params=_params(2),
        name="mixer",
    )(*args)


def _attn_block_kernel(mix_ref, x_ref, k_ref, v_ref, wo_ref, wq_ref, wm_ref, g1_ref, b1_ref, g2_ref, b2_ref,
                       smix_ref, sx_ref, sk_ref, sv_ref, o_ref, so_ref, *, n_sub):
    def layer_norm(acc, g_ref, b_ref):
        mu = jnp.mean(acc, axis=-1, keepdims=True)
        xc = acc - mu
        var = jnp.mean(xc * xc, axis=-1, keepdims=True)
        return xc * lax.rsqrt(var + LN_EPS) * g_ref[...] + b_ref[...]

    nb, st, d = sx_ref.shape
    ts = x_ref.shape[1] // n_sub
    subs = [pl.ds(i * ts, ts) for i in range(n_sub)]
    scale = MEM_HEAD_DIM ** -0.5
    cols = [slice(h * MEM_HEAD_DIM, (h + 1) * MEM_HEAD_DIM) for h in range(MEM_HEADS)]
    heads = range(MEM_HEADS)
    kh = [_head_block(k_ref, 0, h).astype(BF16) for h in heads]
    vh = [_head_block(v_ref, 0, h).astype(BF16) for h in heads]

    x_rows = [x_ref[0, r, :] for r in subs]
    mix_rows = [mix_ref[0, r, :] for r in subs]
    x_rows[0] = jnp.concatenate([x_rows[0], sx_ref[...].reshape(nb * st, d)], axis=0)
    mix_rows[0] = jnp.concatenate([mix_rows[0], smix_ref[...].reshape(nb * st, d)], axis=0)

    h1 = [layer_norm(ALPHA * xr + jnp.dot(mr.astype(BF16), wo_ref[...], preferred_element_type=F32),
                     g1_ref, b1_ref) for xr, mr in zip(x_rows, mix_rows)]
    q = [jnp.dot(hi.astype(BF16), wq_ref[...], preferred_element_type=F32).astype(BF16) for hi in h1]
    s_main = [[_dot_nt(qi[:ts, cols[h]], kh[h]) * scale for h in heads] for qi in q]
    pairs = [(b, h) for b in range(nb) for h in heads]
    s_side = {(b, h): _dot_nt(q[0][ts + b * st:ts + (b + 1) * st, cols[h]], _head_block(sk_ref, b, h)) * scale
              for b, h in pairs}
    p_main = [[_softmax_rows(sh) for sh in si] for si in s_main]
    p_side = {bh: _softmax_rows(s_side[bh]) for bh in pairs}
    att = [jnp.concatenate([_dot(pi[h], vh[h]).astype(BF16) for h in heads], axis=1) for pi in p_main]
    att_side = [jnp.concatenate([_dot(p_side[b, h], _head_block(sv_ref, b, h)).astype(BF16) for h in heads],
                                axis=1) for b in range(nb)]
    att[0] = jnp.concatenate([att[0]] + att_side, axis=0)
    h2 = [layer_norm(ALPHA * hi + jnp.dot(ai, wm_ref[...], preferred_element_type=F32), g2_ref, b2_ref)
          for hi, ai in zip(h1, att)]
    for r, hi in zip(subs, h2):
        o_ref[0, r, :] = hi[:ts]
    so_ref[...] = h2[0][ts:].reshape(nb, st, d)


def _attn_block(mix3, x3, mk, mv, w_out, w_mq, w_mo, ln1_g, ln1_b, ln2_g, ln2_b, side_mix, side_x, side_k,
                side_v, *, tm, n_sub):
    bsz, t, d = x3.shape
    tm = min(tm, t)
    n_steps = bsz * (t // tm)
    bs, ts, _ = side_x.shape
    assert bs % n_steps == 0
    nb = bs // n_steps
    flat = lambda i, r: (i * (t // tm) + r, 0, 0)
    side_rows = pl.BlockSpec((nb, ts, d), flat)
    side_kv = pl.BlockSpec((nb,) + side_k.shape[1:], flat)
    rows = pl.BlockSpec((1, tm, d), lambda i, r: (i, r, 0))
    kv = pl.BlockSpec((1,) + mk.shape[1:], lambda i, r: (i, 0, 0))
    wsp = pl.BlockSpec((d, d), lambda i, r: (0, 0))
    vec = pl.BlockSpec((1, d), lambda i, r: (0, 0))
    return pl.pallas_call(
        functools.partial(_attn_block_kernel, n_sub=n_sub),
        grid=(bsz, t // tm),
        in_specs=[rows, rows, kv, kv, wsp, wsp, wsp, vec, vec, vec, vec, side_rows, side_rows, side_kv, side_kv],
        out_specs=[rows, side_rows],
        out_shape=[jax.ShapeDtypeStruct((bsz, t, d), F32), jax.ShapeDtypeStruct((bs, ts, d), F32)],
        compiler_params=_params(2),
        name="attn_block",
    )(mix3, x3, mk, mv, w_out, w_mq, w_mo, ln1_g.reshape(1, d), ln1_b.reshape(1, d),
      ln2_g.reshape(1, d), ln2_b.reshape(1, d), side_mix, side_x, side_k, side_v)


def _ffn_block_kernel(h_ref, wg_ref, wv_ref, wd_ref, cw_ref, cb_ref, g_ref, b_ref, sh_ref, shist_ref,
                      y_ref, buf_ref, sy_ref, sgate_ref, xp_ref, sxp_ref, *, tm, n_split, grp, row_lo):
    n_hist = FFN_CONV - 1
    f = xp_ref.shape[1]
    ns = sh_ref.shape[0]
    tiles = f // MXU_TILE
    bounds = [MXU_TILE * -(-tiles * j // n_split) for j in range(n_split)] + [f]

    @pl.when(pl.program_id(1) == 0)
    def _():
        xp_ref[0:SUBLANES, :] = jnp.zeros((SUBLANES, f), F32)
        sxp_ref[0:SUBLANES, :] = jnp.zeros((SUBLANES, f), F32)

    h = jnp.concatenate([h_ref[0], sh_ref[...]], axis=0)
    hb = h.astype(BF16)
    cols = [slice(bounds[j], bounds[j + 1]) for j in range(n_split)]
    pos = lax.broadcasted_iota(jnp.int32, (ns, 1), 0) % grp
    is_hist = (pos >= row_lo - n_hist) & (pos < row_lo)

    def conv(ref, rows, gate, c):
        ref[SUBLANES:SUBLANES + rows, c] = gate
        y = gate * cw_ref[n_hist:n_hist + 1, c] + cb_ref[:, c]
        for tap in range(n_hist):
            off = SUBLANES - n_hist + tap
            y = y + ref[off:off + rows, c] * cw_ref[tap:tap + 1, c]
        return y

    gate = [jnp.dot(hb, wg_ref[:, c], preferred_element_type=F32) for c in cols]
    val = [jnp.dot(hb, wv_ref[:, c], preferred_element_type=F32) for c in cols]
    act = []
    for c, gj, vj in zip(cols, gate, val):
        side_gate = jnp.where(is_hist, shist_ref[:, c], gj[tm:])
        sgate_ref[:, c] = side_gate
        y = jnp.concatenate([conv(xp_ref, tm, gj[:tm], c), conv(sxp_ref, ns, side_gate, c)], axis=0)
        gelu = 0.5 * y * (1.0 + lax.erf(y * (2.0 ** -0.5)))
        act.append((gelu * vj).astype(BF16))
    acc = ALPHA * h
    for c, aj in zip(cols, act):
        acc = acc + jnp.dot(aj, wd_ref[c, :], preferred_element_type=F32)
    mu = jnp.mean(acc, axis=-1, keepdims=True)
    xc = acc - mu
    var = jnp.mean(xc * xc, axis=-1, keepdims=True)
    out = xc * lax.rsqrt(var + LN_EPS) * g_ref[...] + b_ref[...]
    y_ref[0] = out[:tm]
    sy_ref[...] = out[tm:]
    buf_ref[0] = xp_ref[SUBLANES + tm - n_hist:SUBLANES + tm, :]
    xp_ref[0:SUBLANES, :] = xp_ref[tm:tm + SUBLANES, :]


def _ffn_block(h3, w_up, w_down, conv_w, conv_b, ln_g, ln_b, side_h, side_hist, *, tm, n_split, grp, row_lo):
    bsz, t, d = h3.shape
    f = w_down.shape[0]
    tm = min(tm, t)
    n_hist = FFN_CONV - 1
    n_steps = bsz * (t // tm)
    rs = side_h.shape[0]
    assert rs % (n_steps * grp) == 0
    ns = rs // n_steps
    once = pl.Buffered(1)
    const = lambda shape: pl.BlockSpec(shape, lambda i, r: (0,) * len(shape), pipeline_mode=once)
    rows = pl.BlockSpec((1, tm, d), lambda i, r: (i, r, 0))
    side = lambda width: pl.BlockSpec((ns, width), lambda i, r: (i * (t // tm) + r, 0))
    return pl.pallas_call(
        functools.partial(_ffn_block_kernel, tm=tm, n_split=n_split, grp=grp, row_lo=row_lo),
        grid=(bsz, t // tm),
        in_specs=[rows, const((d, f)), pl.BlockSpec((d, f), lambda i, r: (0, 1), pipeline_mode=once),
                  const((f, d)), const((FFN_CONV, f)), const((1, f)),
                  const((1, d)), const((1, d)), side(d), side(f)],
        out_specs=[rows, pl.BlockSpec((1, n_hist, f), lambda i, r: (i, 0, 0)), side(d), side(f)],
        out_shape=[jax.ShapeDtypeStruct((bsz, t, d), F32), jax.ShapeDtypeStruct((bsz, n_hist, f), F32),
                   jax.ShapeDtypeStruct((rs, d), F32), jax.ShapeDtypeStruct((rs, f), F32)],
        scratch_shapes=[pltpu.VMEM((SUBLANES + tm, f), F32), pltpu.VMEM((SUBLANES + ns, f), F32)],
        compiler_params=_params(2),
        name="ffn_block",
    )(h3, w_up, w_up, w_down, conv_w, conv_b.reshape(1, f), ln_g.reshape(1, d), ln_b.reshape(1, d),
      side_h, side_hist)


def _cache_rows(cache):
    b, m, h, d = cache.shape
    return cache.reshape(b, m, h, d // LANES, LANES).transpose(0, 1, 3, 2, 4).reshape(b, m * h * d // LANES, LANES)


def _cache_from_rows(rows):
    b = rows.shape[0]
    halves = MEM_HEAD_DIM // LANES
    return (rows.reshape(b, N_MEM, halves, MEM_HEADS, LANES).transpose(0, 1, 3, 2, 4)
            .reshape(b, N_MEM, MEM_HEADS, MEM_HEAD_DIM))


def kernel(x_prompt, x_sample, state_hgrn, state_gdn, state_gdn_conv, state_ffn_conv, cache_mem_k, cache_mem_v, mem_prompt, hgrn_lb_logits, w_in, w_gd_conv, gd_a_log, gd_dt_bias, hg_norm_g, gd_norm_g, w_out, ln1_g, ln1_b, w_mq, w_mkv, w_mo, ln2_g, ln2_b, w_up, w_ffn_conv, b_ffn_conv, w_down, ln3_g, ln3_b):
    bp, tp, d = x_prompt.shape
    bs, ts, _ = x_sample.shape
    tpad = SUBLANES
    row_lo = tpad - ts
    l = 0

    w_in_b = w_in[l].astype(BF16)
    w_in_tail = jnp.pad(w_in_b[:, PROJ_COLS:], ((0, 0), (0, HEAD_DIM - 2 * N_HEADS)))
    mixer_w = (hgrn_lb_logits, w_in_b, w_in_tail, w_gd_conv[l],
               gd_a_log[l], gd_dt_bias[l], hg_norm_g[l], gd_norm_g[l])
    (w_out_b, w_mq_b, w_mo_b, w_up_b, w_down_b) = (w.astype(BF16) for w in
                                                    (w_out[l], w_mq[l], w_mo[l], w_up[l], w_down[l]))

    n_hg = GD_CONV - 1
    n_hf = FFN_CONV - 1
    xs = jnp.pad(x_sample, ((0, 0), (row_lo, 0), (0, 0)))
    hist_gd = jnp.pad(state_gdn_conv[l], ((0, 0), (row_lo - n_hg, ts), (0, 0)))
    hist_ffn = jnp.pad(state_ffn_conv[l], ((0, 0), (row_lo - n_hf, ts), (0, 0)))

    mk, mv = _kv_proj(mem_prompt, w_mkv[l].astype(BF16))
    mix_p, p_hg, p_gd, p_bgd, proj_s, tail_s = _mixer(x_prompt, None, None, mixer_w, None, None, None,
                                                      c=MIX_CHUNK, nc=ROW_BLOCK // MIX_CHUNK, nb=1, row_lo=0,
                                                      side_x=xs.reshape(bs * tpad, d))
    mix_s, s_hg, s_gd, s_bgd = _mixer(None, proj_s.reshape(bs, tpad, PROJ_COLS), tail_s.reshape(bs, tpad, HEAD_DIM),
                                      mixer_w, hist_gd, state_hgrn[l], state_gdn[l],
                                      c=tpad, nc=1, nb=SAMPLE_SEQS, row_lo=row_lo)
    h2_p, h2_s = _attn_block(mix_p, x_prompt, mk, mv, w_out_b, w_mq_b, w_mo_b, ln1_g[l], ln1_b[l],
                             ln2_g[l], ln2_b[l], mix_s, xs, _cache_rows(cache_mem_k[l]),
                             _cache_rows(cache_mem_v[l]), tm=ROW_BLOCK, n_sub=ATTN_SUB_BLOCKS)
    yp, p_bff, ys, gate_s = _ffn_block(h2_p, w_up_b, w_down_b, w_ffn_conv[l], b_ffn_conv[l], ln3_g[l], ln3_b[l],
                                       h2_s.reshape(bs * tpad, d), hist_ffn.reshape(bs * tpad, D_FF),
                                       tm=ROW_BLOCK, n_split=FFN_SPLITS, grp=tpad, row_lo=row_lo)
    ys = ys.reshape(bs, tpad, d)[:, row_lo:]
    s_bff = gate_s.reshape(bs, tpad, D_FF)[:, tpad - n_hf:]

    return (yp, ys, p_hg[None], p_gd[None], p_bgd[None], p_bff[None],
            _cache_from_rows(mk)[None], _cache_from_rows(mv)[None],
            s_hg[None], s_gd[None], s_bgd[None], s_bff[None])
```
